```python
import math
import jax
import jax.numpy as jnp
from jax import lax
import numpy as np

D_MODEL = 2048
BATCH = 4
SEQ = 2048
DEPTH = 4
DEC_BATCH = 8
DEC_SEQ = 64
PAST_LEN = 2048

CHUNK = 64
N_MIXERS = 3
N_SSD_LAYERS = (DEPTH + 2) // 3
N_DIFF_LAYERS = (DEPTH + 1) // 3
N_BAND_LAYERS = DEPTH // 3
NORM_EPS = 1e-6
FFN_DIM = 5632
FFN_RESIDUAL_WEIGHT = 0.5
PLE_DIM = 256
SSD_INNER = 2 * D_MODEL
SSD_HEAD_DIM = 64
SSD_HEADS = SSD_INNER // SSD_HEAD_DIM
SSD_GROUPS = 8
SSD_HEADS_PER_GROUP = SSD_HEADS // SSD_GROUPS
SSD_STATE = 128
SSD_CONV = 4
SSD_CONV_DIM = SSD_INNER + 2 * SSD_GROUPS * SSD_STATE
SSD_IN_WIDTH = SSD_INNER + SSD_CONV_DIM + SSD_HEADS
DIFF_HEAD_DIM = 128
DIFF_HEADS = D_MODEL // (2 * DIFF_HEAD_DIM)
DIFF_QK_WIDTH = DIFF_HEADS * 2 * DIFF_HEAD_DIM
DIFF_V_WIDTH = DIFF_HEADS * 2 * DIFF_HEAD_DIM
Q_BLOCK = 128
BAND_HEADS = 16
BAND_HEAD_DIM = D_MODEL // BAND_HEADS
BAND_PREV_CHUNKS = 8
BAND_LEN = (BAND_PREV_CHUNKS + 1) * CHUNK
BAND_WINDOW = BAND_PREV_CHUNKS * CHUNK
REL_CLIP = 256

kernel_name = 'hybrid_streaming_encoder_step'


def rms_norm(x, g):
    xf = x.astype(jnp.float32)
    y = xf * lax.rsqrt(jnp.mean(xf * xf, axis=-1, keepdims=True) + NORM_EPS)
    return (y * g.astype(jnp.float32)).astype(x.dtype)


def swiglu(x, wg, wu, wd):
    return (jax.nn.silu(x @ wg) * (x @ wu)) @ wd


def ffn_half(h, g, wg, wu, wd):
    return h + FFN_RESIDUAL_WEIGHT * swiglu(rms_norm(h, g), wg, wu, wd)


def ple_add(h, p_i, g, w_ple, w_gate):
    gate = jax.nn.sigmoid(rms_norm(h, g) @ w_gate)
    return h + (p_i @ w_ple) * gate


def causal_depthwise_conv(xpad, w, bias):
    out = lax.conv_general_dilated(xpad, w[:, None, :], window_strides=(1,), padding='VALID',
                                   dimension_numbers=('NWC', 'WIO', 'NWC'),
                                   feature_group_count=xpad.shape[-1])
    return out + bias


def ssd_scan(xs, dt, a, bm, cm, h0):
    b, l, g, e, p = xs.shape
    n = bm.shape[-1]
    cs = min(CHUNK, l)
    nc = l // cs
    f32 = jnp.float32
    xc = xs.astype(f32).reshape(b, nc, cs, g, e, p)
    dtc = dt.reshape(b, nc, cs, g, e)
    bc = bm.astype(f32).reshape(b, nc, cs, g, n)
    cc = cm.astype(f32).reshape(b, nc, cs, g, n)
    a_cum = jnp.cumsum(dtc * a, axis=2)
    causal = jnp.tril(jnp.ones((cs, cs), dtype=bool))[:, :, None, None]
    seg = a_cum[:, :, :, None] - a_cum[:, :, None, :]
    decay = jnp.exp(jnp.where(causal, seg, -jnp.inf))
    cb = jnp.einsum('bctgn,bcsgn->bctsg', cc, bc)
    y_diag = jnp.einsum('bctsge,bcsgep->bctgep', cb[..., None] * decay * dtc[:, :, None], xc)
    w_state = jnp.exp(a_cum[:, :, -1:] - a_cum) * dtc
    dec_in = jnp.exp(a_cum)
    dec_chunk = dec_in[:, :, -1]

    def step(h, inp):
        b_c, c_c, x_c, w_c, din_c, dch_c = inp
        y_off = jnp.einsum('btgn,bgepn,btge->btgep', c_c, h, din_c)
        h = h * dch_c[..., None, None] + jnp.einsum('bsgn,bsge,bsgep->bgepn', b_c, w_c, x_c)
        return h, y_off

    seq = tuple(jnp.moveaxis(t, 1, 0) for t in (bc, cc, xc, w_state, dec_in, dec_chunk))
    h_last, y_off = lax.scan(step, h0.astype(f32), seq)
    y = y_diag + jnp.moveaxis(y_off, 0, 1)
    return y.reshape(b, l, g, e, p).astype(xs.dtype), h_last.astype(h0.dtype)


def ssd_mixer(x, h0, conv0, w_in, conv_w, conv_b, dt_bias, a_log, d_skip, g_norm, w_out):
    b, l, _ = x.shape
    z, xbc, dt = jnp.split(x @ w_in, [SSD_INNER, SSD_INNER + SSD_CONV_DIM], axis=-1)
    xpad = jnp.concatenate([conv0, xbc], axis=1)
    new_conv = xpad[:, l:]
    xbc = jax.nn.silu(causal_depthwise_conv(xpad, conv_w, conv_b))
    xs, bm, cm = jnp.split(xbc, [SSD_INNER, SSD_INNER + SSD_GROUPS * SSD_STATE], axis=-1)
    xs = xs.reshape(b, l, SSD_GROUPS, SSD_HEADS_PER_GROUP, SSD_HEAD_DIM)
    bm = bm.reshape(b, l, SSD_GROUPS, SSD_STATE)
    cm = cm.reshape(b, l, SSD_GROUPS, SSD_STATE)
    dt = jax.nn.softplus((dt + dt_bias).astype(jnp.float32)).reshape(b, l, SSD_GROUPS, SSD_HEADS_PER_GROUP)
    a = -jnp.exp(a_log.astype(jnp.float32)).reshape(SSD_GROUPS, SSD_HEADS_PER_GROUP)
    h0g = h0.reshape(b, SSD_GROUPS, SSD_HEADS_PER_GROUP, SSD_HEAD_DIM, SSD_STATE)
    y, h_last = ssd_scan(xs, dt, a, bm, cm, h0g)
    y = y + d_skip.reshape(SSD_GROUPS, SSD_HEADS_PER_GROUP, 1) * xs
    y = y.reshape(b, l, SSD_INNER) * jax.nn.silu(z)
    gs = SSD_INNER // SSD_GROUPS
    y = rms_norm(y.reshape(b, l, SSD_GROUPS, gs), g_norm.reshape(SSD_GROUPS, gs)).reshape(b, l, SSD_INNER)
    return y @ w_out, h_last.reshape(b, SSD_HEADS, SSD_HEAD_DIM, SSD_STATE), new_conv


def alibi_slopes(n):
    return 2.0 ** (-8.0 * jnp.arange(1, n + 1, dtype=jnp.float32) / n)


def diff_core(q, k, v, q_pos, k_pos, lam, slopes):
    s = jnp.einsum('bqhmd,bkhmd->bhmqk', q, k).astype(jnp.float32) * (DIFF_HEAD_DIM ** -0.5)
    dist = jnp.abs(q_pos[:, None] - k_pos[None, :]).astype(jnp.float32)
    visible = (k_pos[None, :] // CHUNK) <= (q_pos[:, None] // CHUNK)
    bias = jnp.where(visible, -slopes[:, None, None] * dist, -jnp.inf)
    prob = jax.nn.softmax(s + bias[None, :, None], axis=-1)
    attn = prob[:, :, 0] - lam * prob[:, :, 1]
    return jnp.einsum('bhqk,bkhd->bqhd', attn.astype(v.dtype), v)


def diff_mixer(x, k_past, v_past, layer_idx, w_in, g_q, g_k, lam_p, g_sub, w_out):
    b, l, _ = x.shape
    q, k, v = jnp.split(x @ w_in, [DIFF_QK_WIDTH, 2 * DIFF_QK_WIDTH], axis=-1)
    q = rms_norm(q.reshape(b, l, DIFF_HEADS, 2, DIFF_HEAD_DIM), g_q)
    k = rms_norm(k.reshape(b, l, DIFF_HEADS, 2, DIFF_HEAD_DIM), g_k)
    v = v.reshape(b, l, DIFF_HEADS, 2 * DIFF_HEAD_DIM)
    lam_init = 0.8 - 0.6 * math.exp(-0.3 * layer_idx)
    lp = lam_p.astype(jnp.float32)
    lam = jnp.exp(jnp.sum(lp[0] * lp[1])) - jnp.exp(jnp.sum(lp[2] * lp[3])) + lam_init
    past = k_past.shape[1]
    k_all = jnp.concatenate([k_past, k], axis=1)
    v_all = jnp.concatenate([v_past, v], axis=1)
    k_pos = jnp.arange(past + l, dtype=jnp.int32)
    qb = min(Q_BLOCK, l)
    nqb = l // qb
    q_blocks = jnp.moveaxis(q.reshape(b, nqb, qb, DIFF_HEADS, 2, DIFF_HEAD_DIM), 1, 0)
    pos_blocks = (past + jnp.arange(l, dtype=jnp.int32)).reshape(nqb, qb)
    slopes = alibi_slopes(DIFF_HEADS)
    o = lax.map(lambda blk: diff_core(blk[0], k_all, v_all, blk[1], k_pos, lam, slopes),
                (q_blocks, pos_blocks))
    o = jnp.moveaxis(o, 0, 1).reshape(b, l, DIFF_HEADS, 2 * DIFF_HEAD_DIM)
    o = rms_norm(o, g_sub) * (1.0 - lam_init)
    return o.reshape(b, l, DIFF_V_WIDTH) @ w_out, k, v


def band_project(x, w_in, g_q, g_k):
    b, l, _ = x.shape
    q, k, v = jnp.split(x @ w_in, 3, axis=-1)
    shape = (b, l, BAND_HEADS, BAND_HEAD_DIM)
    return rms_norm(q.reshape(shape), g_q), rms_norm(k.reshape(shape), g_k), v.reshape(shape)


def band_bias(q_pos, k_pos, table):
    rel = q_pos[..., :, None] - k_pos[..., None, :]
    qc = q_pos[..., :, None] // CHUNK
    kc = k_pos[..., None, :] // CHUNK
    valid = (kc <= qc) & (kc >= qc - BAND_PREV_CHUNKS) & (k_pos[..., None, :] >= 0)
    bias = jnp.take(table.astype(jnp.float32), jnp.clip(rel, -REL_CLIP, REL_CLIP) + REL_CLIP, axis=1)
    bias = jnp.moveaxis(bias, 0, -3)
    return jnp.where(valid[..., None, :, :], bias, -jnp.inf)


def band_mixer_prompt(x, w_in, g_q, g_k, table, w_out):
    b, l, _ = x.shape
    nc = l // CHUNK
    q, k, v = band_project(x, w_in, g_q, g_k)
    pad = jnp.zeros((b, BAND_WINDOW, BAND_HEADS, BAND_HEAD_DIM), k.dtype)
    kb = jnp.concatenate([pad, k], axis=1).reshape(b, nc + BAND_PREV_CHUNKS, CHUNK, BAND_HEADS, BAND_HEAD_DIM)
    vb = jnp.concatenate([pad, v], axis=1).reshape(b, nc + BAND_PREV_CHUNKS, CHUNK, BAND_HEADS, BAND_HEAD_DIM)
    qb = q.reshape(b, nc, CHUNK, BAND_HEADS, BAND_HEAD_DIM)
    s = jnp.concatenate([jnp.einsum('bcqhd,bckhd->bchqk', qb, kb[:, j:j + nc])
                         for j in range(BAND_PREV_CHUNKS + 1)], axis=-1).astype(jnp.float32)
    s = s * (BAND_HEAD_DIM ** -0.5)
    cidx = jnp.arange(nc, dtype=jnp.int32)[:, None] * CHUNK
    q_pos = cidx + jnp.arange(CHUNK, dtype=jnp.int32)[None, :]
    k_pos = cidx - BAND_WINDOW + jnp.arange(BAND_LEN, dtype=jnp.int32)[None, :]
    prob = jax.nn.softmax(s + band_bias(q_pos, k_pos, table)[None], axis=-1).astype(v.dtype)
    o = sum(jnp.einsum('bchqk,bckhd->bcqhd', prob[..., j * CHUNK:(j + 1) * CHUNK], vb[:, j:j + nc])
            for j in range(BAND_PREV_CHUNKS + 1))
    return o.reshape(b, l, BAND_HEADS * BAND_HEAD_DIM) @ w_out, k, v


def band_mixer_sample(x, k_past, v_past, w_in, g_q, g_k, table, w_out):
    b, t, _ = x.shape
    w = k_past.shape[1]
    q, k, v = band_project(x, w_in, g_q, g_k)
    k_all = jnp.concatenate([k_past, k], axis=1)
    v_all = jnp.concatenate([v_past, v], axis=1)
    q_pos = PAST_LEN + jnp.arange(t, dtype=jnp.int32)
    k_pos = jnp.concatenate([PAST_LEN - w + jnp.arange(w, dtype=jnp.int32), q_pos])
    s = jnp.einsum('bqhd,bkhd->bhqk', q, k_all).astype(jnp.float32) * (BAND_HEAD_DIM ** -0.5)
    prob = jax.nn.softmax(s + band_bias(q_pos, k_pos, table)[None], axis=-1).astype(v.dtype)
    o = jnp.einsum('bhqk,bkhd->bqhd', prob, v_all)
    return o.reshape(b, t, BAND_HEADS * BAND_HEAD_DIM) @ w_out, k, v


def setup_inputs(seed: int = 0) -> dict:
    key = jax.random.key(seed)
    keys = iter(jax.random.split(key, 64))
    f32 = jnp.float32

    def normal(shape, scale=1.0):
        return scale * jax.random.normal(next(keys), shape, f32)

    def gain(shape):
        return 1.0 + 0.05 * jax.random.normal(next(keys), shape, f32)

    band_cache_len = min(BAND_WINDOW, PAST_LEN)
    dt0 = jnp.exp(jax.random.uniform(next(keys), (N_SSD_LAYERS, SSD_HEADS), f32,
                                     math.log(1e-3), math.log(1e-1)))
    a_init = jax.random.uniform(next(keys), (N_SSD_LAYERS, SSD_HEADS), f32, 1.0, 16.0)
    return {
        'x_prompt': normal((BATCH, SEQ, D_MODEL)),
        'x_sample': normal((DEC_BATCH, DEC_SEQ, D_MODEL)),
        'p_prompt': normal((DEPTH, BATCH, SEQ, PLE_DIM)),
        'p_sample': normal((DEPTH, DEC_BATCH, DEC_SEQ, PLE_DIM)),
        'state_ssm': normal((N_SSD_LAYERS, DEC_BATCH, SSD_HEADS, SSD_HEAD_DIM, SSD_STATE)),
        'state_conv': normal((N_SSD_LAYERS, DEC_BATCH, SSD_CONV - 1, SSD_CONV_DIM)),
        'cache_k_diff': normal((N_DIFF_LAYERS, DEC_BATCH, PAST_LEN, DIFF_HEADS, 2, DIFF_HEAD_DIM)),
        'cache_v_diff': normal((N_DIFF_LAYERS, DEC_BATCH, PAST_LEN, DIFF_HEADS, 2 * DIFF_HEAD_DIM)),
        'cache_k_band': normal((N_BAND_LAYERS, DEC_BATCH, band_cache_len, BAND_HEADS, BAND_HEAD_DIM)),
        'cache_v_band': normal((N_BAND_LAYERS, DEC_BATCH, band_cache_len, BAND_HEADS, BAND_HEAD_DIM)),
        'g_ffn': gain((DEPTH, 2, D_MODEL)),
        'w_ffn_gate': normal((DEPTH, 2, D_MODEL, FFN_DIM), D_MODEL ** -0.5),
        'w_ffn_up': normal((DEPTH, 2, D_MODEL, FFN_DIM), D_MODEL ** -0.5),
        'w_ffn_down': normal((DEPTH, 2, FFN_DIM, D_MODEL), FFN_DIM ** -0.5),
        'g_mix': gain((DEPTH, D_MODEL)),
        'ssd_w_in': normal((N_SSD_LAYERS, D_MODEL, SSD_IN_WIDTH), D_MODEL ** -0.5),
        'ssd_conv_w': normal((N_SSD_LAYERS, SSD_CONV, SSD_CONV_DIM), SSD_CONV ** -0.5),
        'ssd_conv_b': normal((N_SSD_LAYERS, SSD_CONV_DIM), 0.02),
        'ssd_dt_bias': dt0 + jnp.log(-jnp.expm1(-dt0)),
        'ssd_a_log': jnp.log(a_init),
        'ssd_d': gain((N_SSD_LAYERS, SSD_HEADS)),
        'ssd_g_norm': gain((N_SSD_LAYERS, SSD_INNER)),
        'ssd_w_out': normal((N_SSD_LAYERS, SSD_INNER, D_MODEL), SSD_INNER ** -0.5),
        'diff_w_in': normal((N_DIFF_LAYERS, D_MODEL, 2 * DIFF_QK_WIDTH + DIFF_V_WIDTH), D_MODEL ** -0.5),
        'diff_g_q': gain((N_DIFF_LAYERS, 2, DIFF_HEAD_DIM)),
        'diff_g_k': gain((N_DIFF_LAYERS, 2, DIFF_HEAD_DIM)),
        'diff_lambda': normal((N_DIFF_LAYERS, 4, DIFF_HEAD_DIM), 0.1),
        'diff_g_sub': gain((N_DIFF_LAYERS, 2 * DIFF_HEAD_DIM)),
        'diff_w_out': normal((N_DIFF_LAYERS, DIFF_V_WIDTH, D_MODEL), DIFF_V_WIDTH ** -0.5),
        'band_w_in': normal((N_BAND_LAYERS, D_MODEL, 3 * BAND_HEADS * BAND_HEAD_DIM), D_MODEL ** -0.5),
        'band_g_q': gain((N_BAND_LAYERS, BAND_HEAD_DIM)),
        'band_g_k': gain((N_BAND_LAYERS, BAND_HEAD_DIM)),
        'band_rel_bias': normal((N_BAND_LAYERS, BAND_HEADS, 2 * REL_CLIP + 1), 0.5),
        'band_w_out': normal((N_BAND_LAYERS, BAND_HEADS * BAND_HEAD_DIM, D_MODEL), (BAND_HEADS * BAND_HEAD_DIM) ** -0.5),
        'g_ple': gain((DEPTH, D_MODEL)),
        'w_ple': normal((DEPTH, PLE_DIM, D_MODEL), PLE_DIM ** -0.5),
        'w_ple_gate': normal((DEPTH, D_MODEL, D_MODEL), D_MODEL ** -0.5),
    }


def reference(x_prompt, x_sample, p_prompt, p_sample, state_ssm, state_conv, cache_k_diff, cache_v_diff,
              cache_k_band, cache_v_band, g_ffn, w_ffn_gate, w_ffn_up, w_ffn_down, g_mix,
              ssd_w_in, ssd_conv_w, ssd_conv_b, ssd_dt_bias, ssd_a_log, ssd_d, ssd_g_norm, ssd_w_out,
              diff_w_in, diff_g_q, diff_g_k, diff_lambda, diff_g_sub, diff_w_out,
              band_w_in, band_g_q, band_g_k, band_rel_bias, band_w_out,
              g_ple, w_ple, w_ple_gate):
    bp, l, _ = x_prompt.shape
    hp, hs = x_prompt, x_sample
    ssm_p, conv_p, kd_p, vd_p, kb_p, vb_p = [], [], [], [], [], []
    ssm_s, conv_s, kd_s, vd_s, kb_s, vb_s = [], [], [], [], [], []
    for i in range(DEPTH):
        kind, j = i % N_MIXERS, i // N_MIXERS
        ffn1 = (g_ffn[i, 0], w_ffn_gate[i, 0], w_ffn_up[i, 0], w_ffn_down[i, 0])
        ffn2 = (g_ffn[i, 1], w_ffn_gate[i, 1], w_ffn_up[i, 1], w_ffn_down[i, 1])
        hp = ffn_half(hp, *ffn1)
        hs = ffn_half(hs, *ffn1)
        xp = rms_norm(hp, g_mix[i])
        xs = rms_norm(hs, g_mix[i])
        if kind == 0:
            ssd_w = (ssd_w_in[j], ssd_conv_w[j], ssd_conv_b[j], ssd_dt_bias[j], ssd_a_log[j],
                     ssd_d[j], ssd_g_norm[j], ssd_w_out[j])
            h0 = jnp.zeros((bp, SSD_HEADS, SSD_HEAD_DIM, SSD_STATE), xp.dtype)
            c0 = jnp.zeros((bp, SSD_CONV - 1, SSD_CONV_DIM), xp.dtype)
            mp, st, cv = ssd_mixer(xp, h0, c0, *ssd_w)
            ssm_p.append(st)
            conv_p.append(cv)
            ms, st, cv = ssd_mixer(xs, state_ssm[j], state_conv[j], *ssd_w)
            ssm_s.append(st)
            conv_s.append(cv)
        elif kind == 1:
            diff_w = (diff_w_in[j], diff_g_q[j], diff_g_k[j], diff_lambda[j], diff_g_sub[j], diff_w_out[j])
            k0 = jnp.zeros((bp, 0, DIFF_HEADS, 2, DIFF_HEAD_DIM), xp.dtype)
            v0 = jnp.zeros((bp, 0, DIFF_HEADS, 2 * DIFF_HEAD_DIM), xp.dtype)
            mp, kn, vn = diff_mixer(xp, k0, v0, i, *diff_w)
            kd_p.append(kn)
            vd_p.append(vn)
            ms, kn, vn = diff_mixer(xs, cache_k_diff[j], cache_v_diff[j], i, *diff_w)
            kd_s.append(kn)
            vd_s.append(vn)
        else:
            band_w = (band_w_in[j], band_g_q[j], band_g_k[j], band_rel_bias[j], band_w_out[j])
            keep = min(BAND_WINDOW, l)
            mp, kn, vn = band_mixer_prompt(xp, *band_w)
            kb_p.append(kn[:, l - keep:])
            vb_p.append(vn[:, l - keep:])
            ms, kn, vn = band_mixer_sample(xs, cache_k_band[j], cache_v_band[j], *band_w)
            kb_s.append(kn)
            vb_s.append(vn)
        hp = hp + mp
        hs = hs + ms
        hp = ffn_half(hp, *ffn2)
        hs = ffn_half(hs, *ffn2)
        hp = ple_add(hp, p_prompt[i], g_ple[i], w_ple[i], w_ple_gate[i])
        hs = ple_add(hs, p_sample[i], g_ple[i], w_ple[i], w_ple_gate[i])
    return (hp, hs,
            jnp.stack(ssm_p), jnp.stack(conv_p), jnp.stack(kd_p), jnp.stack(vd_p), jnp.stack(kb_p), jnp.stack(vb_p),
            jnp.stack(ssm_s), jnp.stack(conv_s), jnp.stack(kd_s), jnp.stack(vd_s), jnp.stack(kb_s), jnp.stack(vb_s))
```

```python
import functools
import math

import jax
import jax.numpy as jnp
from jax import lax
from jax.experimental import pallas as pl
from jax.experimental.pallas import tpu as pltpu

F32 = jnp.float32
BF16 = jnp.bfloat16

D_MODEL = 2048
BATCH = 4
SEQ = 2048
DEPTH = 4
DEC_BATCH = 8
DEC_SEQ = 64
PAST_LEN = 2048
CHUNK = 64
NORM_EPS = 1e-6
FFN_DIM = 5632
PLE_DIM = 256
SSD_INNER = 4096
SSD_HEADS = 64
SSD_HEAD_DIM = 64
SSD_GROUPS = 8
SSD_STATE = 128
SSD_GROUP_WIDTH = SSD_INNER // SSD_GROUPS
SSD_CONV_DIM = SSD_INNER + 2 * SSD_GROUPS * SSD_STATE
SSD_MAIN_WIDTH = SSD_INNER + SSD_CONV_DIM
SSD_EXT_WIDTH = SSD_MAIN_WIDTH + SSD_INNER
DIFF_HEADS = 8
DIFF_HEAD_DIM = 128
BAND_HEADS = 16
BAND_HEAD_DIM = 128
BAND_WINDOW = 512
REL_CLIP = 256

N_PROMPT = BATCH * SEQ
N_SAMPLE = DEC_BATCH * DEC_SEQ
N_TOK = N_PROMPT + N_SAMPLE
CHUNKS_PER_SEQ = SEQ // CHUNK
N_PROMPT_CHUNKS = N_PROMPT // CHUNK
N_CHUNKS = N_TOK // CHUNK

TM = 544
VMEM_LIMIT = 56 * 1024 * 1024
NEG = -1e30


def _params(*sem):
    return pltpu.CompilerParams(dimension_semantics=sem, vmem_limit_bytes=VMEM_LIMIT)


def _rms_rows(x, g):
    ms = jnp.mean(x * x, axis=-1, keepdims=True)
    return x * lax.rsqrt(ms + NORM_EPS) * g


def _dot(a, b):
    return jnp.dot(a, b, preferred_element_type=F32)


def _dot_nt(a, b):
    return lax.dot_general(a, b, (((1,), (1,)), ((), ())), preferred_element_type=F32)


def _dot_tn(a, b):
    return lax.dot_general(a, b, (((0,), (0,)), ((), ())), preferred_element_type=F32)


def _ffn_kernel(h_ref, g_ref, wg_ref, wu_ref, wd_ref, o_ref, xn_ref):
    f = pl.program_id(1)

    @pl.when(f == 0)
    def _():
        x = h_ref[...]
        xn_ref[...] = _rms_rows(x, g_ref[...]).astype(BF16)
        o_ref[...] = x

    xn = xn_ref[...]
    gate = _dot(xn, wg_ref[...])
    up = _dot(xn, wu_ref[...])
    act = (0.5 * (gate * jax.nn.sigmoid(gate)) * up).astype(BF16)
    o_ref[...] += _dot(act, wd_ref[...])


def ffn_half(h, g, wg, wu, wd, *, tf=512):
    n_tok = h.shape[0]
    return pl.pallas_call(
        _ffn_kernel,
        out_shape=jax.ShapeDtypeStruct((n_tok, D_MODEL), F32),
        grid=(n_tok // TM, FFN_DIM // tf),
        in_specs=[
            pl.BlockSpec((TM, D_MODEL), lambda i, f: (i, 0)),
            pl.BlockSpec((1, D_MODEL), lambda i, f: (0, 0)),
            pl.BlockSpec((D_MODEL, tf), lambda i, f: (0, f)),
            pl.BlockSpec((D_MODEL, tf), lambda i, f: (0, f)),
            pl.BlockSpec((tf, D_MODEL), lambda i, f: (f, 0)),
        ],
        out_specs=pl.BlockSpec((TM, D_MODEL), lambda i, f: (i, 0)),
        scratch_shapes=[pltpu.VMEM((TM, D_MODEL), BF16)],
        compiler_params=_params("parallel", "arbitrary"),
        name="ffn_half",
    )(h, g.reshape(1, D_MODEL), wg, wu, wd)


def _inproj_kernel(h_ref, g_ref, w_ref, hg_ref, o_ref, xn_ref, *, norm_tiles, tn):
    j = pl.program_id(1)

    @pl.when(j == 0)
    def _():
        xn_ref[...] = _rms_rows(h_ref[...], g_ref[...]).astype(BF16)

    y = _dot(xn_ref[...], w_ref[...])
    if norm_tiles == 0:
        o_ref[...] = y
        return

    @pl.when(j >= norm_tiles)
    def _():
        o_ref[...] = y

    @pl.when(j < norm_tiles)
    def _():
        for c in range(tn // 128):
            sl = slice(c * 128, (c + 1) * 128)
            o_ref[:, sl] = _rms_rows(y[:, sl], hg_ref[:, sl])


def inproj(h, g, w, head_gain=None, *, norm_cols=0, tn=1024):
    n_tok = h.shape[0]
    n = w.shape[1]
    if head_gain is None:
        head_gain = jnp.ones((1, n), F32)
    kern = functools.partial(_inproj_kernel, norm_tiles=norm_cols // tn, tn=tn)
    return pl.pallas_call(
        kern,
        out_shape=jax.ShapeDtypeStruct((n_tok, n), F32),
        grid=(n_tok // TM, n // tn),
        in_specs=[
            pl.BlockSpec((TM, D_MODEL), lambda i, j: (i, 0)),
            pl.BlockSpec((1, D_MODEL), lambda i, j: (0, 0)),
            pl.BlockSpec((D_MODEL, tn), lambda i, j: (0, j)),
            pl.BlockSpec((1, tn), lambda i, j: (0, j)),
        ],
        out_specs=pl.BlockSpec((TM, tn), lambda i, j: (i, j)),
        scratch_shapes=[pltpu.VMEM((TM, D_MODEL), BF16)],
        compiler_params=_params("parallel", "arbitrary"),
        name="inproj",
    )(h, g.reshape(1, D_MODEL), w, head_gain)


def _outproj_kernel(a_ref, w_ref, h_ref, o_ref):
    o_ref[...] = h_ref[...] + _dot(a_ref[...], w_ref[...])


def outproj(h, a, w, *, tn=1024):
    n_tok, k = a.shape
    return pl.pallas_call(
        _outproj_kernel,
        out_shape=jax.ShapeDtypeStruct((n_tok, D_MODEL), F32),
        grid=(D_MODEL // tn, n_tok // TM),
        in_specs=[
            pl.BlockSpec((TM, k), lambda j, i: (i, 0)),
            pl.BlockSpec((k, tn), lambda j, i: (0, j)),
            pl.BlockSpec((TM, tn), lambda j, i: (i, j)),
        ],
        out_specs=pl.BlockSpec((TM, tn), lambda j, i: (i, j)),
        compiler_params=_params("parallel", "parallel"),
        name="outproj",
    )(a, w, h)


def _ple_kernel(h_ref, p_ref, g_ref, wp_ref, wgt_ref, o_ref):
    x = h_ref[...]
    xn = _rms_rows(x, g_ref[...]).astype(BF16)
    gate = jax.nn.sigmoid(_dot(xn, wgt_ref[...]))
    emb = _dot(p_ref[...].astype(BF16), wp_ref[...])
    o_ref[...] = x + emb * gate


def ple_add(h, p, g, w_ple, w_gate):
    n_tok = h.shape[0]
    return pl.pallas_call(
        _ple_kernel,
        out_shape=jax.ShapeDtypeStruct((n_tok, D_MODEL), F32),
        grid=(n_tok // TM,),
        in_specs=[
            pl.BlockSpec((TM, D_MODEL), lambda i: (i, 0)),
            pl.BlockSpec((TM, PLE_DIM), lambda i: (i, 0)),
            pl.BlockSpec((1, D_MODEL), lambda i: (0, 0)),
            pl.BlockSpec((PLE_DIM, D_MODEL), lambda i: (0, 0)),
            pl.BlockSpec((D_MODEL, D_MODEL), lambda i: (0, 0)),
        ],
        out_specs=pl.BlockSpec((TM, D_MODEL), lambda i: (i, 0)),
        compiler_params=_params("parallel"),
        name="ple_add",
    )(h, p, g.reshape(1, D_MODEL), w_ple, w_gate)


def _ssd_kernel(zxd_ref, st0_ref, cv0_ref, cw_ref, cb_ref, dtb_ref, alog_ref, dsk_ref, gn_ref,
                y_ref, st_ref, s_scr, xpad, act, pcum, dts):
    s = pl.program_id(0)
    is_sample = s >= N_PROMPT_CHUNKS
    pos = s % CHUNKS_PER_SEQ
    first = jnp.logical_or(is_sample, pos == 0)
    last = jnp.logical_or(is_sample, pos == CHUNKS_PER_SEQ - 1)

    @pl.when(jnp.logical_and(first, jnp.logical_not(is_sample)))
    def _():
        s_scr[...] = jnp.zeros_like(s_scr)
        xpad[0:8, :] = jnp.zeros((8, SSD_CONV_DIM), F32)

    @pl.when(is_sample)
    def _():
        for g in range(SSD_GROUPS):
            s_scr[g] = st0_ref[0, g].T
        xpad[0:8, :] = jnp.zeros((8, SSD_CONV_DIM), F32)
        xpad[5:8, :] = cv0_ref[0]

    xpad[8:72, :] = zxd_ref[:, SSD_INNER:SSD_MAIN_WIDTH]
    conv = cb_ref[...] + cw_ref[3:4, :] * xpad[8:72, :]
    conv = conv + cw_ref[2:3, :] * xpad[7:71, :]
    conv = conv + cw_ref[1:2, :] * xpad[6:70, :]
    conv = conv + cw_ref[0:1, :] * xpad[5:69, :]
    act[...] = conv * jax.nn.sigmoid(conv)
    xpad[0:8, :] = xpad[64:72, :]

    x = zxd_ref[:, SSD_MAIN_WIDTH:SSD_EXT_WIDTH] + dtb_ref[...]
    dt = jnp.maximum(x, 0.0) + jnp.log1p(jnp.exp(-jnp.abs(x)))
    dts[...] = dt
    run = dt * (-jnp.exp(alog_ref[...]))
    row = lax.broadcasted_iota(jnp.int32, (CHUNK, SSD_INNER), 0)
    for k in (1, 2, 4, 8, 16, 32):
        run = run + jnp.where(row >= k, pltpu.roll(run, k, 0), 0.0)
    pcum[...] = run

    gw = SSD_GROUP_WIDTH
    row_g = lax.broadcasted_iota(jnp.int32, (CHUNK, gw), 0)
    lane_g = lax.broadcasted_iota(jnp.int32, (CHUNK, gw), 1) % CHUNK
    diag = row_g == lane_g
    causal = row_g >= lane_g
    r4 = lax.broadcasted_iota(jnp.int32, (256, 256), 0) // CHUNK
    c4 = lax.broadcasted_iota(jnp.int32, (256, 256), 1) // SSD_HEAD_DIM
    head_diag = r4 == c4

    for g in range(SSD_GROUPS):
        cs = slice(g * gw, (g + 1) * gw)
        p_g = pcum[:, cs]
        dt_g = dts[:, cs]
        x_g = act[:, cs]
        b_g = act[:, SSD_INNER + g * SSD_STATE:SSD_INNER + (g + 1) * SSD_STATE].astype(BF16)
        c_g = act[:, SSD_INNER + SSD_GROUPS * SSD_STATE + g * SSD_STATE:
                  SSD_INNER + SSD_GROUPS * SSD_STATE + (g + 1) * SSD_STATE].astype(BF16)
        p_s = jnp.sum(jnp.where(diag, p_g, 0.0), axis=0, keepdims=True)
        dt_s = jnp.sum(jnp.where(diag, dt_g, 0.0), axis=0, keepdims=True)
        p_last = p_g[CHUNK - 1:CHUNK, :]
        cb = _dot_nt(c_g, jnp.concatenate([b_g] * 8, axis=0))
        m = (cb * jnp.exp(jnp.where(causal, p_g - p_s, NEG)) * dt_s).astype(BF16)
        x_bf = x_g.astype(BF16)
        halves = []
        for hh in range(2):
            hs = slice(hh * 256, (hh + 1) * 256)
            xh = x_bf[:, hs]
            x_bd = jnp.where(head_diag, jnp.concatenate([xh] * 4, axis=0), jnp.zeros((), BF16))
            halves.append(_dot(m[:, hs], x_bd))
        y = jnp.concatenate(halves, axis=1)
        st = s_scr[g]
        y = y + _dot(c_g, st.astype(BF16)) * jnp.exp(p_g)
        wx = (jnp.exp(p_last - p_g) * dt_g * x_g).astype(BF16)
        s_scr[g] = st * jnp.exp(p_last) + _dot_tn(b_g, wx)
        y = y + dsk_ref[:, cs] * x_g
        z = zxd_ref[:, cs]
        y = y * (z * jax.nn.sigmoid(z))
        y_ref[:, cs] = _rms_rows(y, gn_ref[:, cs]).astype(BF16)

    @pl.when(last)
    def _():
        for g in range(SSD_GROUPS):
            st_ref[0, g] = s_scr[g].T


def _ssd_seq(s):
    return jnp.where(s < N_PROMPT_CHUNKS, s // CHUNKS_PER_SEQ, BATCH + s - N_PROMPT_CHUNKS)


def ssd_core(zxd, state0, conv0, conv_w, conv_b, dt_bias, a_log, d_skip, g_norm):
    rep = lambda v: jnp.repeat(v, SSD_HEAD_DIM).reshape(1, SSD_INNER)
    samp = lambda s: jnp.maximum(s - N_PROMPT_CHUNKS, 0)
    row_spec = lambda w: pl.BlockSpec((1, w), lambda s: (0, 0))
    st0 = state0.reshape(DEC_BATCH, SSD_GROUPS, SSD_GROUP_WIDTH, SSD_STATE)
    y, st = pl.pallas_call(
        _ssd_kernel,
        out_shape=(jax.ShapeDtypeStruct((N_TOK, SSD_INNER), BF16),
                   jax.ShapeDtypeStruct((BATCH + DEC_BATCH, SSD_GROUPS, SSD_GROUP_WIDTH, SSD_STATE), F32)),
        grid=(N_CHUNKS,),
        in_specs=[
            pl.BlockSpec((CHUNK, SSD_EXT_WIDTH), lambda s: (s, 0)),
            pl.BlockSpec((1, SSD_GROUPS, SSD_GROUP_WIDTH, SSD_STATE), lambda s: (samp(s), 0, 0, 0)),
            pl.BlockSpec((1, 3, SSD_CONV_DIM), lambda s: (samp(s), 0, 0)),
            pl.BlockSpec((4, SSD_CONV_DIM), lambda s: (0, 0)),
            row_spec(SSD_CONV_DIM), row_spec(SSD_INNER), row_spec(SSD_INNER), row_spec(SSD_INNER),
            row_spec(SSD_INNER),
        ],
        out_specs=(pl.BlockSpec((CHUNK, SSD_INNER), lambda s: (s, 0)),
                   pl.BlockSpec((1, SSD_GROUPS, SSD_GROUP_WIDTH, SSD_STATE), lambda s: (_ssd_seq(s), 0, 0, 0))),
        scratch_shapes=[
            pltpu.VMEM((SSD_GROUPS, SSD_STATE, SSD_GROUP_WIDTH), F32),
            pltpu.VMEM((CHUNK + 8, SSD_CONV_DIM), F32),
            pltpu.VMEM((CHUNK, SSD_CONV_DIM), F32),
            pltpu.VMEM((CHUNK, SSD_INNER), F32),
            pltpu.VMEM((CHUNK, SSD_INNER), F32),
        ],
        compiler_params=_params("arbitrary"),
        name="ssd_core",
    )(zxd, st0, conv0, conv_w, conv_b.reshape(1, SSD_CONV_DIM), rep(dt_bias), rep(a_log), rep(d_skip),
      g_norm.reshape(1, SSD_INNER))
    return y, st.reshape(BATCH + DEC_BATCH, SSD_HEADS, SSD_HEAD_DIM, SSD_STATE)


DIFF_SCALE = DIFF_HEAD_DIM ** -0.5


def _diff_lambda(lp_ref, lam_init):
    lp = lp_ref[...]
    a = jnp.sum(lp[0:1] * lp[1:2], axis=-1, keepdims=True)
    b = jnp.sum(lp[2:3] * lp[3:4], axis=-1, keepdims=True)
    return jnp.exp(a) - jnp.exp(b) + lam_init


def _alibi_slope(h):
    return jnp.exp2(-(jnp.zeros((1, 1), F32) + (h + 1).astype(F32)))


def _diff_prompt_kernel(q_ref, k_ref, v_ref, lp_ref, gs_ref, o_ref, *, lam_init, tq):
    h = pl.program_id(1)
    lam = _diff_lambda(lp_ref, lam_init)
    slope = _alibi_slope(h)
    d = DIFF_HEAD_DIM
    for qb in range(SEQ // tq):
        nk = (qb + 1) * tq
        rows = slice(qb * tq, (qb + 1) * tq)
        qpos = qb * tq + lax.broadcasted_iota(jnp.int32, (tq, nk), 0)
        kpos = lax.broadcasted_iota(jnp.int32, (tq, nk), 1)
        dist = jnp.abs(qpos - kpos).astype(F32)
        bias = jnp.where(kpos // CHUNK <= qpos // CHUNK, -slope * dist, NEG)
        probs = []
        for m in range(2):
            cols = slice(m * d, (m + 1) * d)
            sc = _dot_nt(q_ref[rows, cols].astype(BF16), k_ref[0:nk, cols].astype(BF16)) * DIFF_SCALE + bias
            e = jnp.exp(sc - jnp.max(sc, axis=-1, keepdims=True))
            probs.append(e / jnp.sum(e, axis=-1, keepdims=True))
        attn = (probs[0] - lam * probs[1]).astype(BF16)
        o = _dot(attn, v_ref[0:nk, :].astype(BF16))
        o_ref[rows, :] = (_rms_rows(o, gs_ref[...]) * (1.0 - lam_init)).astype(BF16)


def diff_attn_prompt(qkv, lam_p, g_sub, lam_init, *, tq=512):
    w = 2 * DIFF_HEAD_DIM
    kern = functools.partial(_diff_prompt_kernel, lam_init=lam_init, tq=tq)
    return pl.pallas_call(
        kern,
        out_shape=jax.ShapeDtypeStruct((N_PROMPT, D_MODEL), BF16),
        grid=(BATCH, DIFF_HEADS),
        in_specs=[
            pl.BlockSpec((SEQ, w), lambda b, h: (b, h)),
            pl.BlockSpec((SEQ, w), lambda b, h: (b, DIFF_HEADS + h)),
            pl.BlockSpec((SEQ, w), lambda b, h: (b, 2 * DIFF_HEADS + h)),
            pl.BlockSpec((4, DIFF_HEAD_DIM), lambda b, h: (0, 0)),
            pl.BlockSpec((1, w), lambda b, h: (0, 0)),
        ],
        out_specs=pl.BlockSpec((SEQ, w), lambda b, h: (b, h)),
        compiler_params=_params("parallel", "parallel"),
        name="diff_attn_prompt",
    )(qkv, qkv, qkv, lam_p, g_sub.reshape(1, w))


def _diff_sample_kernel(q_ref, kn_ref, vn_ref, kc_ref, vc_ref, lp_ref, gs_ref, o_ref, *, lam_init):
    h = pl.program_id(1)
    lam = _diff_lambda(lp_ref, lam_init)
    slope = _alibi_slope(h)
    d = DIFF_HEAD_DIM
    t = DEC_SEQ
    tq_p = lax.broadcasted_iota(jnp.int32, (t, PAST_LEN), 0)
    kp_p = lax.broadcasted_iota(jnp.int32, (t, PAST_LEN), 1)
    bias_p = -slope * (PAST_LEN + tq_p - kp_p).astype(F32)
    tq_n = lax.broadcasted_iota(jnp.int32, (t, t), 0)
    kp_n = lax.broadcasted_iota(jnp.int32, (t, t), 1)
    bias_n = -slope * jnp.abs(tq_n - kp_n).astype(F32)
    pp, pn = [], []
    for m in range(2):
        cols = slice(m * d, (m + 1) * d)
        q = q_ref[:, cols].astype(BF16)
        sp = _dot_nt(q, kc_ref[0, :, cols].astype(BF16)) * DIFF_SCALE + bias_p
        sn = _dot_nt(q, kn_ref[:, cols].astype(BF16)) * DIFF_SCALE + bias_n
        mx = jnp.maximum(jnp.max(sp, axis=-1, keepdims=True), jnp.max(sn, axis=-1, keepdims=True))
        ep = jnp.exp(sp - mx)
        en = jnp.exp(sn - mx)
        den = jnp.sum(ep, axis=-1, keepdims=True) + jnp.sum(en, axis=-1, keepdims=True)
        pp.append(ep / den)
        pn.append(en / den)
    ap = (pp[0] - lam * pp[1]).astype(BF16)
    an = (pn[0] - lam * pn[1]).astype(BF16)
    o = _dot(ap, vc_ref[0].astype(BF16)) + _dot(an, vn_ref[...].astype(BF16))
    o_ref[...] = (_rms_rows(o, gs_ref[...]) * (1.0 - lam_init)).astype(BF16)


def diff_attn_sample(qkv, cache_k, cache_v, lam_p, g_sub, lam_init):
    w = 2 * DIFF_HEAD_DIM
    r0 = N_PROMPT // DEC_SEQ
    kern = functools.partial(_diff_sample_kernel, lam_init=lam_init)
    return pl.pallas_call(
        kern,
        out_shape=jax.ShapeDtypeStruct((N_SAMPLE, D_MODEL), BF16),
        grid=(DEC_BATCH, DIFF_HEADS),
        in_specs=[
            pl.BlockSpec((DEC_SEQ, w), lambda b, h: (r0 + b, h)),
            pl.BlockSpec((DEC_SEQ, w), lambda b, h: (r0 + b, DIFF_HEADS + h)),
            pl.BlockSpec((DEC_SEQ, w), lambda b, h: (r0 + b, 2 * DIFF_HEADS + h)),
            pl.BlockSpec((1, PAST_LEN, w), lambda b, h: (b, 0, h)),
            pl.BlockSpec((1, PAST_LEN, w), lambda b, h: (b, 0, h)),
            pl.BlockSpec((4, DIFF_HEAD_DIM), lambda b, h: (0, 0)),
            pl.BlockSpec((1, w), lambda b, h: (0, 0)),
        ],
        out_specs=pl.BlockSpec((DEC_SEQ, w), lambda b, h: (b, h)),
        compiler_params=_params("parallel", "parallel"),
        name="diff_attn_sample",
    )(qkv, qkv, qkv, cache_k.reshape(DEC_BATCH, PAST_LEN, D_MODEL), cache_v.reshape(DEC_BATCH, PAST_LEN, D_MODEL),
      lam_p, g_sub.reshape(1, w))


BAND_SCALE = BAND_HEAD_DIM ** -0.5
BAND_TQ = 256
BAND_E = 1024


def _band_bias(e_ref, tq):
    eb = jnp.broadcast_to(e_ref[0], (tq, BAND_E))
    return pltpu.roll(eb, BAND_E - (BAND_TQ - 1), 1, stride=1, stride_axis=0)


def _band_prompt_kernel(q_ref, k_ref, v_ref, e_ref, o_ref):
    tq = BAND_TQ
    win = BAND_WINDOW + tq
    bias = _band_bias(e_ref, tq)[:, :win]
    qc = lax.broadcasted_iota(jnp.int32, (tq, win), 0) // CHUNK
    jc = lax.broadcasted_iota(jnp.int32, (tq, win), 1) // CHUNK
    bias = jnp.where(jnp.logical_and(jc >= qc, jc <= qc + BAND_WINDOW // CHUNK), bias, NEG)
    for qb in range(SEQ // tq):
        q0 = qb * tq
        k0 = max(0, q0 - BAND_WINDOW)
        nk = q0 + tq - k0
        rows = slice(q0, q0 + tq)
        sc = _dot_nt(q_ref[rows, :].astype(BF16), k_ref[k0:k0 + nk, :].astype(BF16)) * BAND_SCALE
        sc = sc + bias[:, win - nk:]
        e = jnp.exp(sc - jnp.max(sc, axis=-1, keepdims=True))
        p = (e / jnp.sum(e, axis=-1, keepdims=True)).astype(BF16)
        o_ref[rows, :] = _dot(p, v_ref[k0:k0 + nk, :].astype(BF16)).astype(BF16)


def _band_e_rows(table):
    edge = jnp.broadcast_to(table[:, 2 * REL_CLIP:], (BAND_HEADS, BAND_E - (2 * REL_CLIP + 1)))
    return jnp.concatenate([edge, table[:, ::-1]], axis=1).reshape(BAND_HEADS, 1, BAND_E)


def band_attn_prompt(qkv, table):
    d = BAND_HEAD_DIM
    return pl.pallas_call(
        _band_prompt_kernel,
        out_shape=jax.ShapeDtypeStruct((N_PROMPT, D_MODEL), BF16),
        grid=(BATCH, BAND_HEADS),
        in_specs=[
            pl.BlockSpec((SEQ, d), lambda b, h: (b, h)),
            pl.BlockSpec((SEQ, d), lambda b, h: (b, BAND_HEADS + h)),
            pl.BlockSpec((SEQ, d), lambda b, h: (b, 2 * BAND_HEADS + h)),
            pl.BlockSpec((1, 1, BAND_E), lambda b, h: (h, 0, 0)),
        ],
        out_specs=pl.BlockSpec((SEQ, d), lambda b, h: (b, h)),
        compiler_params=_params("parallel", "parallel"),
        name="band_attn_prompt",
    )(qkv, qkv, qkv, _band_e_rows(table))


def _band_sample_kernel(q_ref, kn_ref, vn_ref, kc_ref, vc_ref, e_ref, o_ref):
    t = DEC_SEQ
    bias = _band_bias(e_ref, t)
    q = q_ref[...].astype(BF16)
    sp = _dot_nt(q, kc_ref[0].astype(BF16)) * BAND_SCALE + bias[:, :BAND_WINDOW]
    sn = _dot_nt(q, kn_ref[...].astype(BF16)) * BAND_SCALE + bias[:, BAND_WINDOW:BAND_WINDOW + t]
    mx = jnp.maximum(jnp.max(sp, axis=-1, keepdims=True), jnp.max(sn, axis=-1, keepdims=True))
    ep = jnp.exp(sp - mx)
    en = jnp.exp(sn - mx)
    den = jnp.sum(ep, axis=-1, keepdims=True) + jnp.sum(en, axis=-1, keepdims=True)
    o = _dot((ep / den).astype(BF16), vc_ref[0].astype(BF16)) + _dot((en / den).astype(BF16), vn_ref[...].astype(BF16))
    o_ref[...] = o.astype(BF16)


def band_attn_sample(qkv, cache_k, cache_v, table):
    d = BAND_HEAD_DIM
    r0 = N_PROMPT // DEC_SEQ
    return pl.pallas_call(
        _band_sample_kernel,
        out_shape=jax.ShapeDtypeStruct((N_SAMPLE, D_MODEL), BF16),
        grid=(DEC_BATCH, BAND_HEADS),
        in_specs=[
            pl.BlockSpec((DEC_SEQ, d), lambda b, h: (r0 + b, h)),
            pl.BlockSpec((DEC_SEQ, d), lambda b, h: (r0 + b, BAND_HEADS + h)),
            pl.BlockSpec((DEC_SEQ, d), lambda b, h: (r0 + b, 2 * BAND_HEADS + h)),
            pl.BlockSpec((1, BAND_WINDOW, d), lambda b, h: (b, 0, h)),
            pl.BlockSpec((1, BAND_WINDOW, d), lambda b, h: (b, 0, h)),
            pl.BlockSpec((1, 1, BAND_E), lambda b, h: (h, 0, 0)),
        ],
        out_specs=pl.BlockSpec((DEC_SEQ, d), lambda b, h: (b, h)),
        compiler_params=_params("parallel", "parallel"),
        name="band_attn_sample",
    )(qkv, qkv, qkv, cache_k.reshape(DEC_BATCH, BAND_WINDOW, D_MODEL), cache_v.reshape(DEC_BATCH, BAND_WINDOW, D_MODEL),
      _band_e_rows(table))


def kernel(x_prompt, x_sample, p_prompt, p_sample, state_ssm, state_conv, cache_k_diff, cache_v_diff,
           cache_k_band, cache_v_band, g_ffn, w_ffn_gate, w_ffn_up, w_ffn_down, g_mix,
           ssd_w_in, ssd_conv_w, ssd_conv_b, ssd_dt_bias, ssd_a_log, ssd_d, ssd_g_norm, ssd_w_out,
           diff_w_in, diff_g_q, diff_g_k, diff_lambda, diff_g_sub, diff_w_out,
           band_w_in, band_g_q, band_g_k, band_rel_bias, band_w_out,
           g_ple, w_ple, w_ple_gate):
    h = jnp.concatenate([x_prompt.reshape(N_PROMPT, D_MODEL), x_sample.reshape(N_SAMPLE, D_MODEL)], axis=0)
    p_all = jnp.concatenate([p_prompt.reshape(DEPTH, N_PROMPT, PLE_DIM),
                             p_sample.reshape(DEPTH, N_SAMPLE, PLE_DIM)], axis=1)
    bf = lambda w: w.astype(BF16)

    def prompt_rows(a):
        return a[:N_PROMPT].reshape(BATCH, SEQ, -1)

    def sample_rows(a):
        return a[N_PROMPT:].reshape(DEC_BATCH, DEC_SEQ, -1)

    ssm_p, conv_p, ssm_s, conv_s = [], [], [], []
    kd_p = vd_p = kb_p = vb_p = kd_s = vd_s = kb_s = vb_s = None
    for i in range(DEPTH):
        kind, j = i % 3, i // 3
        h = ffn_half(h, g_ffn[i, 0], bf(w_ffn_gate[i, 0]), bf(w_ffn_up[i, 0]), bf(w_ffn_down[i, 0]))
        if kind == 0:
            w_in = ssd_w_in[j]
            w_ext = jnp.concatenate([w_in[:, :SSD_MAIN_WIDTH],
                                     jnp.repeat(w_in[:, SSD_MAIN_WIDTH:], SSD_HEAD_DIM, axis=1)], axis=1)
            zxd = inproj(h, g_mix[i], bf(w_ext))
            y, st = ssd_core(zxd, state_ssm[j], state_conv[j], ssd_conv_w[j], ssd_conv_b[j], ssd_dt_bias[j],
                             ssd_a_log[j], ssd_d[j], ssd_g_norm[j])
            h = outproj(h, y, bf(ssd_w_out[j]))
            xbc = zxd[:, SSD_INNER:SSD_MAIN_WIDTH]
            ssm_p.append(st[:BATCH])
            ssm_s.append(st[BATCH:])
            conv_p.append(prompt_rows(xbc)[:, SEQ - 3:])
            conv_s.append(sample_rows(xbc)[:, DEC_SEQ - 3:])
        elif kind == 1:
            lam_init = 0.8 - 0.6 * math.exp(-0.3 * i)
            gains = jnp.concatenate([jnp.tile(diff_g_q[j].reshape(-1), DIFF_HEADS),
                                     jnp.tile(diff_g_k[j].reshape(-1), DIFF_HEADS),
                                     jnp.ones((D_MODEL,), F32)]).reshape(1, 3 * D_MODEL)
            qkv = inproj(h, g_mix[i], bf(diff_w_in[j]), gains, norm_cols=2 * D_MODEL)
            o_p = diff_attn_prompt(qkv, diff_lambda[j], diff_g_sub[j], lam_init)
            o_s = diff_attn_sample(qkv, cache_k_diff[j], cache_v_diff[j], diff_lambda[j], diff_g_sub[j], lam_init)
            h = outproj(h, jnp.concatenate([o_p, o_s], axis=0), bf(diff_w_out[j]))
            k_new, v_new = qkv[:, D_MODEL:2 * D_MODEL], qkv[:, 2 * D_MODEL:]
            kd_p = prompt_rows(k_new).reshape(1, BATCH, SEQ, DIFF_HEADS, 2, DIFF_HEAD_DIM)
            vd_p = prompt_rows(v_new).reshape(1, BATCH, SEQ, DIFF_HEADS, 2 * DIFF_HEAD_DIM)
            kd_s = sample_rows(k_new).reshape(1, DEC_BATCH, DEC_SEQ, DIFF_HEADS, 2, DIFF_HEAD_DIM)
            vd_s = sample_rows(v_new).reshape(1, DEC_BATCH, DEC_SEQ, DIFF_HEADS, 2 * DIFF_HEAD_DIM)
        else:
            gains = jnp.concatenate([jnp.tile(band_g_q[j], BAND_HEADS), jnp.tile(band_g_k[j], BAND_HEADS),
                                     jnp.ones((D_MODEL,), F32)]).reshape(1, 3 * D_MODEL)
            qkv = inproj(h, g_mix[i], bf(band_w_in[j]), gains, norm_cols=2 * D_MODEL)
            o_p = band_attn_prompt(qkv, band_rel_bias[j])
            o_s = band_attn_sample(qkv, cache_k_band[j], cache_v_band[j], band_rel_bias[j])
            h = outproj(h, jnp.concatenate([o_p, o_s], axis=0), bf(band_w_out[j]))
            k_new, v_new = qkv[:, D_MODEL:2 * D_MODEL], qkv[:, 2 * D_MODEL:]
            kb_p = prompt_rows(k_new)[:, SEQ - BAND_WINDOW:].reshape(1, BATCH, BAND_WINDOW, BAND_HEADS, BAND_HEAD_DIM)
            vb_p = prompt_rows(v_new)[:, SEQ - BAND_WINDOW:].reshape(1, BATCH, BAND_WINDOW, BAND_HEADS, BAND_HEAD_DIM)
            kb_s = sample_rows(k_new).reshape(1, DEC_BATCH, DEC_SEQ, BAND_HEADS, BAND_HEAD_DIM)
            vb_s = sample_rows(v_new).reshape(1, DEC_BATCH, DEC_SEQ, BAND_HEADS, BAND_HEAD_DIM)
        h = ffn_half(h, g_ffn[i, 1], bf(w_ffn_gate[i, 1]), bf(w_ffn_up[i, 1]), bf(w_ffn_down[i, 1]))
        h = ple_add(h, p_all[i], g_ple[i], bf(w_ple[i]), bf(w_ple_gate[i]))
    return (h[:N_PROMPT].reshape(BATCH, SEQ, D_MODEL), h[N_PROMPT:].reshape(DEC_BATCH, DEC_SEQ, D_MODEL),
            jnp.stack(ssm_p), jnp.stack(conv_p), kd_p, vd_p, kb_p, vb_p,
            jnp.stack(ssm_s), jnp.stack(conv_s), kd_s, vd_s, kb_s, vb_s)
```

```python
import functools
import math

import jax
import jax.numpy as jnp
from jax import lax
from jax.experimental import pallas as pl
from jax.experimental.pallas import tpu as pltpu

F32 = jnp.float32
BF16 = jnp.bfloat16

D_MODEL = 2048
BATCH = 4
SEQ = 2048
DEPTH = 4
DEC_BATCH = 8
DEC_SEQ = 64
PAST_LEN = 2048
CHUNK = 64
NORM_EPS = 1e-6
FFN_DIM = 5632
PLE_DIM = 256
SSD_INNER = 4096
SSD_HEADS = 64
SSD_HEAD_DIM = 64
SSD_GROUPS = 8
SSD_STATE = 128
SSD_GROUP_WIDTH = SSD_INNER // SSD_GROUPS
SSD_CONV_DIM = SSD_INNER + 2 * SSD_GROUPS * SSD_STATE
SSD_MAIN_WIDTH = SSD_INNER + SSD_CONV_DIM
SSD_EXT_WIDTH = SSD_MAIN_WIDTH + SSD_INNER
DIFF_HEADS = 8
DIFF_HEAD_DIM = 128
BAND_HEADS = 16
BAND_HEAD_DIM = 128
BAND_WINDOW = 512
REL_CLIP = 256

N_PROMPT = BATCH * SEQ
N_SAMPLE = DEC_BATCH * DEC_SEQ
N_TOK = N_PROMPT + N_SAMPLE
CHUNKS_PER_SEQ = SEQ // CHUNK
N_PROMPT_CHUNKS = N_PROMPT // CHUNK
N_CHUNKS = N_TOK // CHUNK

TM_WIDE = 1088
TM = 544
TM_SPLIT = 512
N_PROMPT_TILES = N_PROMPT // TM_SPLIT
VMEM_LIMIT = 56 * 1024 * 1024
NEG = -1e30


def _params(*sem, vmem=VMEM_LIMIT):
    return pltpu.CompilerParams(dimension_semantics=sem, vmem_limit_bytes=vmem)


def _rms_rows(x, g):
    ms = jnp.mean(x * x, axis=-1, keepdims=True)
    return x * lax.rsqrt(ms + NORM_EPS) * g


def _dot(a, b):
    return jnp.dot(a, b, preferred_element_type=F32)


def _dot_nt(a, b):
    return lax.dot_general(a, b, (((1,), (1,)), ((), ())), preferred_element_type=F32)


def _dot_tn(a, b):
    return lax.dot_general(a, b, (((0,), (0,)), ((), ())), preferred_element_type=F32)


def _ffn_kernel(h_ref, g_ref, wg_ref, wu_ref, wd_ref, o_ref, xn_ref):
    f = pl.program_id(1)

    @pl.when(f == 0)
    def _():
        x = h_ref[...]
        xn_ref[...] = _rms_rows(x, g_ref[...]).astype(BF16)
        o_ref[...] = x

    xn = xn_ref[...]
    gate = _dot(xn, wg_ref[...])
    up = _dot(xn, wu_ref[...])
    act = (0.5 * (gate * jax.nn.sigmoid(gate)) * up).astype(BF16)
    o_ref[...] += _dot(act, wd_ref[...])


def ffn_half(h, g_all, wg_all, wu_all, wd_all, layer, half, *, tf=512):
    tm = TM_WIDE
    return pl.pallas_call(
        _ffn_kernel,
        out_shape=jax.ShapeDtypeStruct((N_TOK, D_MODEL), F32),
        grid=(N_TOK // tm, FFN_DIM // tf),
        in_specs=[
            pl.BlockSpec((tm, D_MODEL), lambda i, f: (i, 0), pipeline_mode=pl.Buffered(1)),
            pl.BlockSpec((None, None, 1, D_MODEL), lambda i, f: (layer, half, 0, 0)),
            pl.BlockSpec((None, None, D_MODEL, tf), lambda i, f: (layer, half, 0, f)),
            pl.BlockSpec((None, None, D_MODEL, tf), lambda i, f: (layer, half, 0, f)),
            pl.BlockSpec((None, None, tf, D_MODEL), lambda i, f: (layer, half, f, 0)),
        ],
        out_specs=pl.BlockSpec((tm, D_MODEL), lambda i, f: (i, 0)),
        scratch_shapes=[pltpu.VMEM((tm, D_MODEL), BF16)],
        compiler_params=_params("parallel", "arbitrary"),
        name="ffn_half",
    )(h, g_all.reshape(DEPTH, 2, 1, D_MODEL), wg_all, wu_all, wd_all)


def _inproj_kernel(h_ref, g_ref, w_ref, hg_ref, *rest, norm_tiles, tn, has_extra):
    if has_extra:
        wx_ref, o_ref, ox_ref, xn_ref = rest
    else:
        o_ref, xn_ref = rest
    j = pl.program_id(1)

    @pl.when(j == 0)
    def _():
        xn = _rms_rows(h_ref[...], g_ref[...]).astype(BF16)
        xn_ref[...] = xn
        if has_extra:
            ox_ref[...] = _dot(xn, wx_ref[...])

    y = _dot(xn_ref[...], w_ref[...])
    if norm_tiles == 0:
        o_ref[...] = y
        return

    @pl.when(j >= norm_tiles)
    def _():
        o_ref[...] = y

    @pl.when(j < norm_tiles)
    def _():
        for c in range(tn // 128):
            sl = slice(c * 128, (c + 1) * 128)
            o_ref[:, sl] = _rms_rows(y[:, sl], hg_ref[:, sl])


def inproj(h, g, w_all, layer, head_gain=None, *, norm_cols=0, w_extra=None, tn=1024):
    tm = TM_WIDE
    n = w_all.shape[2]
    if head_gain is None:
        head_gain = jnp.ones((1, n), F32)
    has_extra = w_extra is not None
    kern = functools.partial(_inproj_kernel, norm_tiles=norm_cols // tn, tn=tn, has_extra=has_extra)
    in_specs = [
        pl.BlockSpec((tm, D_MODEL), lambda i, j: (i, 0)),
        pl.BlockSpec((1, D_MODEL), lambda i, j: (0, 0)),
        pl.BlockSpec((None, D_MODEL, tn), lambda i, j: (layer, 0, j)),
        pl.BlockSpec((1, tn), lambda i, j: (0, j)),
    ]
    args = [h, g.reshape(1, D_MODEL), w_all, head_gain]
    out_shape = jax.ShapeDtypeStruct((N_TOK, n), F32)
    out_specs = pl.BlockSpec((tm, tn), lambda i, j: (i, j))
    if has_extra:
        nx = w_extra.shape[2]
        in_specs.append(pl.BlockSpec((None, D_MODEL, nx), lambda i, j: (layer, 0, 0)))
        args.append(w_extra)
        out_shape = (out_shape, jax.ShapeDtypeStruct((N_TOK, nx), F32))
        out_specs = (out_specs, pl.BlockSpec((tm, nx), lambda i, j: (i, 0)))
    return pl.pallas_call(
        kern,
        out_shape=out_shape,
        grid=(N_TOK // tm, n // tn),
        in_specs=in_specs,
        out_specs=out_specs,
        scratch_shapes=[pltpu.VMEM((tm, D_MODEL), BF16)],
        compiler_params=_params("parallel", "arbitrary"),
        name="inproj",
    )(*args)


def _outproj_kernel(a_ref, w_ref, h_ref, o_ref):
    o_ref[...] = h_ref[...] + _dot(a_ref[...], w_ref[...])


def _outproj_split_kernel(ap_ref, as_ref, w_ref, h_ref, o_ref):
    i = pl.program_id(1)

    @pl.when(i < N_PROMPT_TILES)
    def _():
        o_ref[...] = h_ref[...] + _dot(ap_ref[...], w_ref[...])

    @pl.when(i >= N_PROMPT_TILES)
    def _():
        o_ref[...] = h_ref[...] + _dot(as_ref[...], w_ref[...])


def outproj(h, a, w_all, layer, *, tn=1024):
    tm = TM_SPLIT
    k = w_all.shape[1]
    w_spec = pl.BlockSpec((None, k, tn), lambda j, i: (layer, 0, j))
    h_spec = pl.BlockSpec((tm, tn), lambda j, i: (i, j))
    if isinstance(a, tuple):
        kern = _outproj_split_kernel
        a_specs = [pl.BlockSpec((tm, k), lambda j, i: (jnp.minimum(i, N_PROMPT_TILES - 1), 0)),
                   pl.BlockSpec((tm, k), lambda j, i: (0, 0))]
        args = list(a)
    else:
        kern = _outproj_kernel
        a_specs = [pl.BlockSpec((tm, k), lambda j, i: (i, 0))]
        args = [a]
    return pl.pallas_call(
        kern,
        out_shape=jax.ShapeDtypeStruct((N_TOK, D_MODEL), F32),
        grid=(D_MODEL // tn, N_TOK // tm),
        in_specs=a_specs + [w_spec, h_spec],
        out_specs=h_spec,
        compiler_params=_params("parallel", "arbitrary"),
        name="outproj",
    )(*args, w_all, h)


def _ple_update(h_ref, p_ref, g_ref, wp_ref, wgt_ref):
    x = h_ref[...]
    xn = _rms_rows(x, g_ref[...]).astype(BF16)
    gate = jax.nn.sigmoid(_dot(xn, wgt_ref[...]))
    emb = _dot(p_ref[...].astype(BF16), wp_ref[...])
    return x + emb * gate


def _ple_kernel(h_ref, p_ref, g_ref, wp_ref, wgt_ref, o_ref):
    o_ref[...] = _ple_update(h_ref, p_ref, g_ref, wp_ref, wgt_ref)


def _ple_split_kernel(h_ref, p_ref, g_ref, wp_ref, wgt_ref, op_ref, os_ref):
    i = pl.program_id(0)

    @pl.when(i < N_PROMPT_TILES)
    def _():
        op_ref[...] = _ple_update(h_ref, p_ref, g_ref, wp_ref, wgt_ref)

    @pl.when(i >= N_PROMPT_TILES)
    def _():
        os_ref[...] = _ple_update(h_ref, p_ref, g_ref, wp_ref, wgt_ref)


def ple_add(h, p_all, g_all, w_ple_all, w_gate_all, layer, *, split=False):
    tm = TM_SPLIT if split else TM
    in_specs = [
        pl.BlockSpec((tm, D_MODEL), lambda i: (i, 0)),
        pl.BlockSpec((None, tm, PLE_DIM), lambda i: (layer, i, 0)),
        pl.BlockSpec((None, 1, D_MODEL), lambda i: (layer, 0, 0)),
        pl.BlockSpec((None, PLE_DIM, D_MODEL), lambda i: (layer, 0, 0)),
        pl.BlockSpec((None, D_MODEL, D_MODEL), lambda i: (layer, 0, 0)),
    ]
    if split:
        kern = _ple_split_kernel
        out_shape = (jax.ShapeDtypeStruct((N_PROMPT, D_MODEL), F32), jax.ShapeDtypeStruct((N_SAMPLE, D_MODEL), F32))
        out_specs = (pl.BlockSpec((tm, D_MODEL), lambda i: (jnp.minimum(i, N_PROMPT_TILES - 1), 0)),
                     pl.BlockSpec((tm, D_MODEL), lambda i: (0, 0)))
    else:
        kern = _ple_kernel
        out_shape = jax.ShapeDtypeStruct((N_TOK, D_MODEL), F32)
        out_specs = pl.BlockSpec((tm, D_MODEL), lambda i: (i, 0))
    return pl.pallas_call(
        kern,
        out_shape=out_shape,
        grid=(N_TOK // tm,),
        in_specs=in_specs,
        out_specs=out_specs,
        compiler_params=_params("arbitrary"),
        name="ple_add",
    )(h, p_all, g_all.reshape(DEPTH, 1, D_MODEL), w_ple_all, w_gate_all)


def _ssd_kernel(zx_ref, dtr_ref, st0_ref, cv0_ref, cw_ref, cb_ref, dtb_ref, alog_ref, rep_ref, dsk_ref, gn_ref,
                y_ref, st_ref, cvt_ref, s_scr, xpad, act, pcum, dts):
    s = pl.program_id(0)
    is_sample = s >= N_PROMPT_CHUNKS
    pos = s % CHUNKS_PER_SEQ
    first = jnp.logical_or(is_sample, pos == 0)
    last = jnp.logical_or(is_sample, pos == CHUNKS_PER_SEQ - 1)

    @pl.when(jnp.logical_and(first, jnp.logical_not(is_sample)))
    def _():
        s_scr[...] = jnp.zeros_like(s_scr)
        xpad[0:8, :] = jnp.zeros((8, SSD_CONV_DIM), F32)

    @pl.when(is_sample)
    def _():
        for g in range(SSD_GROUPS):
            s_scr[g] = st0_ref[0, g].T
        xpad[0:8, :] = jnp.zeros((8, SSD_CONV_DIM), F32)
        xpad[5:8, :] = cv0_ref[0]

    xpad[8:72, :] = zx_ref[:, SSD_INNER:SSD_MAIN_WIDTH]
    conv = cb_ref[...] + cw_ref[3:4, :] * xpad[8:72, :]
    conv = conv + cw_ref[2:3, :] * xpad[7:71, :]
    conv = conv + cw_ref[1:2, :] * xpad[6:70, :]
    conv = conv + cw_ref[0:1, :] * xpad[5:69, :]
    act[...] = conv * jax.nn.sigmoid(conv)
    xpad[0:8, :] = xpad[64:72, :]

    @pl.when(last)
    def _():
        cvt_ref[0] = xpad[64:72, :]

    x = dtr_ref[...] + dtb_ref[...]
    dt = jnp.maximum(x, 0.0) + jnp.log1p(jnp.exp(-jnp.abs(x)))
    run = dt * (-jnp.exp(alog_ref[...]))
    row = lax.broadcasted_iota(jnp.int32, (CHUNK, 128), 0)
    for k in (1, 2, 4, 8, 16, 32):
        run = run + jnp.where(row >= k, pltpu.roll(run, k, 0), 0.0)
    both = jnp.concatenate([dt, run], axis=0)
    hi = both.astype(BF16)
    r1 = both - hi.astype(F32)
    mid = r1.astype(BF16)
    lo = (r1 - mid.astype(F32)).astype(BF16)
    wide = _dot(jnp.concatenate([hi, mid, lo], axis=0), rep_ref[...])
    wide = (wide[0:2 * CHUNK] + wide[2 * CHUNK:4 * CHUNK]) + wide[4 * CHUNK:6 * CHUNK]
    dts[...] = wide[0:CHUNK]
    pcum[...] = wide[CHUNK:2 * CHUNK]

    gw = SSD_GROUP_WIDTH
    row_g = lax.broadcasted_iota(jnp.int32, (CHUNK, gw), 0)
    lane_g = lax.broadcasted_iota(jnp.int32, (CHUNK, gw), 1) % CHUNK
    diag = row_g == lane_g
    causal = row_g >= lane_g
    r4 = lax.broadcasted_iota(jnp.int32, (256, 256), 0) // CHUNK
    c4 = lax.broadcasted_iota(jnp.int32, (256, 256), 1) // SSD_HEAD_DIM
    head_diag = r4 == c4

    for g in range(SSD_GROUPS):
        cs = slice(g * gw, (g + 1) * gw)
        p_g = pcum[:, cs]
        dt_g = dts[:, cs]
        x_g = act[:, cs]
        b_g = act[:, SSD_INNER + g * SSD_STATE:SSD_INNER + (g + 1) * SSD_STATE].astype(BF16)
        c_g = act[:, SSD_INNER + SSD_GROUPS * SSD_STATE + g * SSD_STATE:
                  SSD_INNER + SSD_GROUPS * SSD_STATE + (g + 1) * SSD_STATE].astype(BF16)
        p_s = jnp.sum(jnp.where(diag, p_g, 0.0), axis=0, keepdims=True)
        dt_s = jnp.sum(jnp.where(diag, dt_g, 0.0), axis=0, keepdims=True)
        p_last = p_g[CHUNK - 1:CHUNK, :]
        cb = _dot_nt(c_g, jnp.concatenate([b_g] * 8, axis=0))
        m = (cb * jnp.exp(jnp.where(causal, p_g - p_s, NEG)) * dt_s).astype(BF16)
        x_bf = x_g.astype(BF16)
        halves = []
        for hh in range(2):
            hs = slice(hh * 256, (hh + 1) * 256)
            xh = x_bf[:, hs]
            x_bd = jnp.where(head_diag, jnp.concatenate([xh] * 4, axis=0), jnp.zeros((), BF16))
            halves.append(_dot(m[:, hs], x_bd))
        y = jnp.concatenate(halves, axis=1)
        st = s_scr[g]
        y = y + _dot(c_g, st.astype(BF16)) * jnp.exp(p_g)
        wx = (jnp.exp(p_last - p_g) * dt_g * x_g).astype(BF16)
        s_scr[g] = st * jnp.exp(p_last) + _dot_tn(b_g, wx)
        y = y + dsk_ref[:, cs] * x_g
        z = zx_ref[:, cs]
        y = y * (z * jax.nn.sigmoid(z))
        y_ref[:, cs] = _rms_rows(y, gn_ref[:, cs]).astype(BF16)

    @pl.when(last)
    def _():
        for g in range(SSD_GROUPS):
            st_ref[0, g] = s_scr[g].T


def _ssd_seq(s):
    return jnp.where(s < N_PROMPT_CHUNKS, s // CHUNKS_PER_SEQ, BATCH + s - N_PROMPT_CHUNKS)


def ssd_core(zx, dt_raw, state_all, conv_all, conv_w_all, layer, conv_b, dt_bias, a_log, d_skip, g_norm):
    n_seq = BATCH + DEC_BATCH
    pad_heads = lambda v: jnp.pad(v, (0, 128 - SSD_HEADS)).reshape(1, 128)
    samp = lambda s: jnp.maximum(s - N_PROMPT_CHUNKS, 0)
    row_spec = lambda w: pl.BlockSpec((1, w), lambda s: (0, 0))
    st_all = state_all.reshape(-1, DEC_BATCH, SSD_GROUPS, SSD_GROUP_WIDTH, SSD_STATE)
    head_of_channel = jnp.arange(SSD_INNER, dtype=jnp.int32) // SSD_HEAD_DIM
    rep = (jnp.arange(128, dtype=jnp.int32)[:, None] == head_of_channel[None, :]).astype(BF16)
    y, st, cvt = pl.pallas_call(
        _ssd_kernel,
        out_shape=(jax.ShapeDtypeStruct((N_TOK, SSD_INNER), BF16),
                   jax.ShapeDtypeStruct((n_seq, SSD_GROUPS, SSD_GROUP_WIDTH, SSD_STATE), F32),
                   jax.ShapeDtypeStruct((n_seq, 8, SSD_CONV_DIM), F32)),
        grid=(N_CHUNKS,),
        in_specs=[
            pl.BlockSpec((CHUNK, SSD_MAIN_WIDTH), lambda s: (s, 0)),
            pl.BlockSpec((CHUNK, 128), lambda s: (s, 0)),
            pl.BlockSpec((None, 1, SSD_GROUPS, SSD_GROUP_WIDTH, SSD_STATE), lambda s: (layer, samp(s), 0, 0, 0)),
            pl.BlockSpec((None, 1, 3, SSD_CONV_DIM), lambda s: (layer, samp(s), 0, 0)),
            pl.BlockSpec((None, 4, SSD_CONV_DIM), lambda s: (layer, 0, 0)),
            row_spec(SSD_CONV_DIM), row_spec(128), row_spec(128),
            pl.BlockSpec((128, SSD_INNER), lambda s: (0, 0)),
            row_spec(SSD_INNER), row_spec(SSD_INNER),
        ],
        out_specs=(pl.BlockSpec((CHUNK, SSD_INNER), lambda s: (s, 0)),
                   pl.BlockSpec((1, SSD_GROUPS, SSD_GROUP_WIDTH, SSD_STATE), lambda s: (_ssd_seq(s), 0, 0, 0)),
                   pl.BlockSpec((1, 8, SSD_CONV_DIM), lambda s: (_ssd_seq(s), 0, 0))),
        scratch_shapes=[
            pltpu.VMEM((SSD_GROUPS, SSD_STATE, SSD_GROUP_WIDTH), F32),
            pltpu.VMEM((CHUNK + 8, SSD_CONV_DIM), F32),
            pltpu.VMEM((CHUNK, SSD_CONV_DIM), F32),
            pltpu.VMEM((CHUNK, SSD_INNER), F32),
            pltpu.VMEM((CHUNK, SSD_INNER), F32),
        ],
        compiler_params=_params("arbitrary"),
        name="ssd_core",
    )(zx, dt_raw, st_all, conv_all, conv_w_all, conv_b.reshape(1, SSD_CONV_DIM), pad_heads(dt_bias),
      pad_heads(a_log), rep, jnp.repeat(d_skip, SSD_HEAD_DIM).reshape(1, SSD_INNER), g_norm.reshape(1, SSD_INNER))
    return y, st.reshape(n_seq, SSD_HEADS, SSD_HEAD_DIM, SSD_STATE), cvt[:, 5:8]


DIFF_SCALE = DIFF_HEAD_DIM ** -0.5


def _diff_lambda(lp_ref, lam_init):
    lp = lp_ref[...]
    a = jnp.sum(lp[0:1] * lp[1:2], axis=-1, keepdims=True)
    b = jnp.sum(lp[2:3] * lp[3:4], axis=-1, keepdims=True)
    return jnp.exp(a) - jnp.exp(b) + lam_init


def _alibi_slope(h):
    return jnp.exp2(-(jnp.zeros((1, 1), F32) + (h + 1).astype(F32)))


def _diff_prompt_kernel(q_ref, k_ref, v_ref, lp_ref, gs_ref, o_ref, *, lam_init, tq):
    h = pl.program_id(1)
    lam = _diff_lambda(lp_ref, lam_init)
    slope = _alibi_slope(h)
    d = DIFF_HEAD_DIM
    for qb in range(SEQ // tq):
        nk = (qb + 1) * tq
        rows = slice(qb * tq, (qb + 1) * tq)
        qpos = qb * tq + lax.broadcasted_iota(jnp.int32, (tq, nk), 0)
        kpos = lax.broadcasted_iota(jnp.int32, (tq, nk), 1)
        dist = jnp.abs(qpos - kpos).astype(F32)
        bias = jnp.where(kpos // CHUNK <= qpos // CHUNK, -slope * dist, NEG)
        probs = []
        for m in range(2):
            cols = slice(m * d, (m + 1) * d)
            sc = _dot_nt(q_ref[rows, cols].astype(BF16), k_ref[0:nk, cols].astype(BF16)) * DIFF_SCALE + bias
            e = jnp.exp(sc - jnp.max(sc, axis=-1, keepdims=True))
            probs.append(e / jnp.sum(e, axis=-1, keepdims=True))
        attn = (probs[0] - lam * probs[1]).astype(BF16)
        o = _dot(attn, v_ref[0:nk, :].astype(BF16))
        o_ref[rows, :] = (_rms_rows(o, gs_ref[...]) * (1.0 - lam_init)).astype(BF16)


def diff_attn_prompt(qkv, lam_p, g_sub, lam_init, *, tq=512):
    w = 2 * DIFF_HEAD_DIM
    kern = functools.partial(_diff_prompt_kernel, lam_init=lam_init, tq=tq)
    return pl.pallas_call(
        kern,
        out_shape=jax.ShapeDtypeStruct((N_PROMPT, D_MODEL), BF16),
        grid=(BATCH, DIFF_HEADS),
        in_specs=[
            pl.BlockSpec((SEQ, w), lambda b, h: (b, h)),
            pl.BlockSpec((SEQ, w), lambda b, h: (b, DIFF_HEADS + h)),
            pl.BlockSpec((SEQ, w), lambda b, h: (b, 2 * DIFF_HEADS + h)),
            pl.BlockSpec((4, DIFF_HEAD_DIM), lambda b, h: (0, 0)),
            pl.BlockSpec((1, w), lambda b, h: (0, 0)),
        ],
        out_specs=pl.BlockSpec((SEQ, w), lambda b, h: (b, h)),
        compiler_params=_params("parallel", "parallel"),
        name="diff_attn_prompt",
    )(qkv, qkv, qkv, lam_p, g_sub.reshape(1, w))


def _diff_sample_kernel(q_ref, kn_ref, vn_ref, kc_ref, vc_ref, lp_ref, gs_ref, o_ref, *, lam_init):
    h = pl.program_id(1)
    lam = _diff_lambda(lp_ref, lam_init)
    slope = _alibi_slope(h)
    d = DIFF_HEAD_DIM
    t = DEC_SEQ
    tq_p = lax.broadcasted_iota(jnp.int32, (t, PAST_LEN), 0)
    kp_p = lax.broadcasted_iota(jnp.int32, (t, PAST_LEN), 1)
    bias_p = -slope * (PAST_LEN + tq_p - kp_p).astype(F32)
    tq_n = lax.broadcasted_iota(jnp.int32, (t, t), 0)
    kp_n = lax.broadcasted_iota(jnp.int32, (t, t), 1)
    bias_n = -slope * jnp.abs(tq_n - kp_n).astype(F32)
    pp, pn = [], []
    for m in range(2):
        cols = slice(m * d, (m + 1) * d)
        q = q_ref[:, cols].astype(BF16)
        sp = _dot_nt(q, kc_ref[0, :, cols].astype(BF16)) * DIFF_SCALE + bias_p
        sn = _dot_nt(q, kn_ref[:, cols].astype(BF16)) * DIFF_SCALE + bias_n
        mx = jnp.maximum(jnp.max(sp, axis=-1, keepdims=True), jnp.max(sn, axis=-1, keepdims=True))
        ep = jnp.exp(sp - mx)
        en = jnp.exp(sn - mx)
        den = jnp.sum(ep, axis=-1, keepdims=True) + jnp.sum(en, axis=-1, keepdims=True)
        pp.append(ep / den)
        pn.append(en / den)
    ap = (pp[0] - lam * pp[1]).astype(BF16)
    an = (pn[0] - lam * pn[1]).astype(BF16)
    o = _dot(ap, vc_ref[0].astype(BF16)) + _dot(an, vn_ref[...].astype(BF16))
    o_ref[...] = (_rms_rows(o, gs_ref[...]) * (1.0 - lam_init)).astype(BF16)


def diff_attn_sample(qkv, cache_k, cache_v, lam_p, g_sub, lam_init):
    w = 2 * DIFF_HEAD_DIM
    r0 = N_PROMPT // DEC_SEQ
    kern = functools.partial(_diff_sample_kernel, lam_init=lam_init)
    return pl.pallas_call(
        kern,
        out_shape=jax.ShapeDtypeStruct((N_SAMPLE, D_MODEL), BF16),
        grid=(DEC_BATCH, DIFF_HEADS),
        in_specs=[
            pl.BlockSpec((DEC_SEQ, w), lambda b, h: (r0 + b, h)),
            pl.BlockSpec((DEC_SEQ, w), lambda b, h: (r0 + b, DIFF_HEADS + h)),
            pl.BlockSpec((DEC_SEQ, w), lambda b, h: (r0 + b, 2 * DIFF_HEADS + h)),
            pl.BlockSpec((1, PAST_LEN, w), lambda b, h: (b, 0, h)),
            pl.BlockSpec((1, PAST_LEN, w), lambda b, h: (b, 0, h)),
            pl.BlockSpec((4, DIFF_HEAD_DIM), lambda b, h: (0, 0)),
            pl.BlockSpec((1, w), lambda b, h: (0, 0)),
        ],
        out_specs=pl.BlockSpec((DEC_SEQ, w), lambda b, h: (b, h)),
        compiler_params=_params("parallel", "parallel"),
        name="diff_attn_sample",
    )(qkv, qkv, qkv, cache_k.reshape(DEC_BATCH, PAST_LEN, D_MODEL), cache_v.reshape(DEC_BATCH, PAST_LEN, D_MODEL),
      lam_p, g_sub.reshape(1, w))


BAND_SCALE = BAND_HEAD_DIM ** -0.5
BAND_TQ = 256
BAND_E = 1024


def _band_bias(e_ref, tq):
    eb = jnp.broadcast_to(e_ref[0], (tq, BAND_E))
    return pltpu.roll(eb, BAND_E - (BAND_TQ - 1), 1, stride=1, stride_axis=0)


def _band_prompt_kernel(q_ref, k_ref, v_ref, e_ref, o_ref):
    tq = BAND_TQ
    win = BAND_WINDOW + tq
    bias = _band_bias(e_ref, tq)[:, :win]
    qc = lax.broadcasted_iota(jnp.int32, (tq, win), 0) // CHUNK
    jc = lax.broadcasted_iota(jnp.int32, (tq, win), 1) // CHUNK
    bias = jnp.where(jnp.logical_and(jc >= qc, jc <= qc + BAND_WINDOW // CHUNK), bias, NEG)
    for qb in range(SEQ // tq):
        q0 = qb * tq
        k0 = max(0, q0 - BAND_WINDOW)
        nk = q0 + tq - k0
        rows = slice(q0, q0 + tq)
        sc = _dot_nt(q_ref[rows, :].astype(BF16), k_ref[k0:k0 + nk, :].astype(BF16)) * BAND_SCALE
        sc = sc + bias[:, win - nk:]
        e = jnp.exp(sc - jnp.max(sc, axis=-1, keepdims=True))
        p = (e / jnp.sum(e, axis=-1, keepdims=True)).astype(BF16)
        o_ref[rows, :] = _dot(p, v_ref[k0:k0 + nk, :].astype(BF16)).astype(BF16)


def _band_e_rows(table):
    edge = jnp.broadcast_to(table[:, 2 * REL_CLIP:], (BAND_HEADS, BAND_E - (2 * REL_CLIP + 1)))
    return jnp.concatenate([edge, table[:, ::-1]], axis=1).reshape(BAND_HEADS, 1, BAND_E)


def band_attn_prompt(qkv, table):
    d = BAND_HEAD_DIM
    return pl.pallas_call(
        _band_prompt_kernel,
        out_shape=jax.ShapeDtypeStruct((N_PROMPT, D_MODEL), BF16),
        grid=(BATCH, BAND_HEADS),
        in_specs=[
            pl.BlockSpec((SEQ, d), lambda b, h: (b, h)),
            pl.BlockSpec((SEQ, d), lambda b, h: (b, BAND_HEADS + h)),
            pl.BlockSpec((SEQ, d), lambda b, h: (b, 2 * BAND_HEADS + h)),
            pl.BlockSpec((1, 1, BAND_E), lambda b, h: (h, 0, 0)),
        ],
        out_specs=pl.BlockSpec((SEQ, d), lambda b, h: (b, h)),
        compiler_params=_params("parallel", "parallel"),
        name="band_attn_prompt",
    )(qkv, qkv, qkv, _band_e_rows(table))


def _band_sample_kernel(q_ref, kn_ref, vn_ref, kc_ref, vc_ref, e_ref, o_ref):
    t = DEC_SEQ
    bias = _band_bias(e_ref, t)
    q = q_ref[...].astype(BF16)
    sp = _dot_nt(q, kc_ref[0].astype(BF16)) * BAND_SCALE + bias[:, :BAND_WINDOW]
    sn = _dot_nt(q, kn_ref[...].astype(BF16)) * BAND_SCALE + bias[:, BAND_WINDOW:BAND_WINDOW + t]
    mx = jnp.maximum(jnp.max(sp, axis=-1, keepdims=True), jnp.max(sn, axis=-1, keepdims=True))
    ep = jnp.exp(sp - mx)
    en = jnp.exp(sn - mx)
    den = jnp.sum(ep, axis=-1, keepdims=True) + jnp.sum(en, axis=-1, keepdims=True)
    o = _dot((ep / den).astype(BF16), vc_ref[0].astype(BF16)) + _dot((en / den).astype(BF16), vn_ref[...].astype(BF16))
    o_ref[...] = o.astype(BF16)


def band_attn_sample(qkv, cache_k, cache_v, table):
    d = BAND_HEAD_DIM
    r0 = N_PROMPT // DEC_SEQ
    return pl.pallas_call(
        _band_sample_kernel,
        out_shape=jax.ShapeDtypeStruct((N_SAMPLE, D_MODEL), BF16),
        grid=(DEC_BATCH, BAND_HEADS),
        in_specs=[
            pl.BlockSpec((DEC_SEQ, d), lambda b, h: (r0 + b, h)),
            pl.BlockSpec((DEC_SEQ, d), lambda b, h: (r0 + b, BAND_HEADS + h)),
            pl.BlockSpec((DEC_SEQ, d), lambda b, h: (r0 + b, 2 * BAND_HEADS + h)),
            pl.BlockSpec((1, BAND_WINDOW, d), lambda b, h: (b, 0, h)),
            pl.BlockSpec((1, BAND_WINDOW, d), lambda b, h: (b, 0, h)),
            pl.BlockSpec((1, 1, BAND_E), lambda b, h: (h, 0, 0)),
        ],
        out_specs=pl.BlockSpec((DEC_SEQ, d), lambda b, h: (b, h)),
        compiler_params=_params("parallel", "parallel"),
        name="band_attn_sample",
    )(qkv, qkv, qkv, cache_k.reshape(DEC_BATCH, BAND_WINDOW, D_MODEL), cache_v.reshape(DEC_BATCH, BAND_WINDOW, D_MODEL),
      _band_e_rows(table))


def kernel(x_prompt, x_sample, p_prompt, p_sample, state_ssm, state_conv, cache_k_diff, cache_v_diff,
           cache_k_band, cache_v_band, g_ffn, w_ffn_gate, w_ffn_up, w_ffn_down, g_mix,
           ssd_w_in, ssd_conv_w, ssd_conv_b, ssd_dt_bias, ssd_a_log, ssd_d, ssd_g_norm, ssd_w_out,
           diff_w_in, diff_g_q, diff_g_k, diff_lambda, diff_g_sub, diff_w_out,
           band_w_in, band_g_q, band_g_k, band_rel_bias, band_w_out,
           g_ple, w_ple, w_ple_gate):
    h = jnp.concatenate([x_prompt.reshape(N_PROMPT, D_MODEL), x_sample.reshape(N_SAMPLE, D_MODEL)], axis=0)
    p_all = jnp.concatenate([p_prompt.reshape(DEPTH, N_PROMPT, PLE_DIM),
                             p_sample.reshape(DEPTH, N_SAMPLE, PLE_DIM)], axis=1)
    bf = lambda w: w.astype(BF16)
    wg_all, wu_all, wd_all = bf(w_ffn_gate), bf(w_ffn_up), bf(w_ffn_down)
    w_ple_all, w_gate_all = bf(w_ple), bf(w_ple_gate)
    ssd_w_main = bf(ssd_w_in[:, :, :SSD_MAIN_WIDTH])
    ssd_w_dt = bf(jnp.pad(ssd_w_in[:, :, SSD_MAIN_WIDTH:], ((0, 0), (0, 0), (0, 128 - SSD_HEADS))))
    ssd_w_out_all, diff_w_in_all, diff_w_out_all = bf(ssd_w_out), bf(diff_w_in), bf(diff_w_out)
    band_w_in_all, band_w_out_all = bf(band_w_in), bf(band_w_out)

    def prompt_rows(a):
        return a[:N_PROMPT].reshape(BATCH, SEQ, -1)

    def sample_rows(a):
        return a[N_PROMPT:].reshape(DEC_BATCH, DEC_SEQ, -1)

    ssm_p, conv_p, ssm_s, conv_s = [], [], [], []
    kd_p = vd_p = kb_p = vb_p = kd_s = vd_s = kb_s = vb_s = None
    for i in range(DEPTH):
        kind, j = i % 3, i // 3
        h = ffn_half(h, g_ffn, wg_all, wu_all, wd_all, i, 0)
        if kind == 0:
            zx, dt_raw = inproj(h, g_mix[i], ssd_w_main, j, w_extra=ssd_w_dt)
            y, st, cvt = ssd_core(zx, dt_raw, state_ssm, state_conv, ssd_conv_w, j, ssd_conv_b[j], ssd_dt_bias[j],
                                  ssd_a_log[j], ssd_d[j], ssd_g_norm[j])
            h = outproj(h, y, ssd_w_out_all, j)
            ssm_p.append(st[:BATCH])
            ssm_s.append(st[BATCH:])
            conv_p.append(cvt[:BATCH])
            conv_s.append(cvt[BATCH:])
        elif kind == 1:
            lam_init = 0.8 - 0.6 * math.exp(-0.3 * i)
            gains = jnp.concatenate([jnp.tile(diff_g_q[j].reshape(-1), DIFF_HEADS),
                                     jnp.tile(diff_g_k[j].reshape(-1), DIFF_HEADS),
                                     jnp.ones((D_MODEL,), F32)]).reshape(1, 3 * D_MODEL)
            qkv = inproj(h, g_mix[i], diff_w_in_all, j, gains, norm_cols=2 * D_MODEL)
            o_p = diff_attn_prompt(qkv, diff_lambda[j], diff_g_sub[j], lam_init)
            o_s = diff_attn_sample(qkv, cache_k_diff[j], cache_v_diff[j], diff_lambda[j], diff_g_sub[j], lam_init)
            h = outproj(h, (o_p, o_s), diff_w_out_all, j)
            k_new, v_new = qkv[:, D_MODEL:2 * D_MODEL], qkv[:, 2 * D_MODEL:]
            kd_p = prompt_rows(k_new).reshape(1, BATCH, SEQ, DIFF_HEADS, 2, DIFF_HEAD_DIM)
            vd_p = prompt_rows(v_new).reshape(1, BATCH, SEQ, DIFF_HEADS, 2 * DIFF_HEAD_DIM)
            kd_s = sample_rows(k_new).reshape(1, DEC_BATCH, DEC_SEQ, DIFF_HEADS, 2, DIFF_HEAD_DIM)
            vd_s = sample_rows(v_new).reshape(1, DEC_BATCH, DEC_SEQ, DIFF_HEADS, 2 * DIFF_HEAD_DIM)
        else:
            gains = jnp.concatenate([jnp.tile(band_g_q[j], BAND_HEADS), jnp.tile(band_g_k[j], BAND_HEADS),
                                     jnp.ones((D_MODEL,), F32)]).reshape(1, 3 * D_MODEL)
            qkv = inproj(h, g_mix[i], band_w_in_all, j, gains, norm_cols=2 * D_MODEL)
            o_p = band_attn_prompt(qkv, band_rel_bias[j])
            o_s = band_attn_sample(qkv, cache_k_band[j], cache_v_band[j], band_rel_bias[j])
            h = outproj(h, (o_p, o_s), band_w_out_all, j)
            k_new, v_new = qkv[:, D_MODEL:2 * D_MODEL], qkv[:, 2 * D_MODEL:]
            kb_p = prompt_rows(k_new)[:, SEQ - BAND_WINDOW:].reshape(1, BATCH, BAND_WINDOW, BAND_HEADS, BAND_HEAD_DIM)
            vb_p = prompt_rows(v_new)[:, SEQ - BAND_WINDOW:].reshape(1, BATCH, BAND_WINDOW, BAND_HEADS, BAND_HEAD_DIM)
            kb_s = sample_rows(k_new).reshape(1, DEC_BATCH, DEC_SEQ, BAND_HEADS, BAND_HEAD_DIM)
            vb_s = sample_rows(v_new).reshape(1, DEC_BATCH, DEC_SEQ, BAND_HEADS, BAND_HEAD_DIM)
        h = ffn_half(h, g_ffn, wg_all, wu_all, wd_all, i, 1)
        h = ple_add(h, p_all, g_ple, w_ple_all, w_gate_all, i, split=(i == DEPTH - 1))
    h_p, h_s = h
    return (h_p.reshape(BATCH, SEQ, D_MODEL), h_s.reshape(DEC_BATCH, DEC_SEQ, D_MODEL),
            jnp.stack(ssm_p), jnp.stack(conv_p), kd_p, vd_p, kb_p, vb_p,
            jnp.stack(ssm_s), jnp.stack(conv_s), kd_s, vd_s, kb_s, vb_s)
```

```python
import functools
import math

import jax
import jax.numpy as jnp
from jax import lax
from jax.experimental import pallas as pl
from jax.experimental.pallas import tpu as pltpu

F32 = jnp.float32
BF16 = jnp.bfloat16

D_MODEL = 2048
BATCH = 4
SEQ = 2048
DEPTH = 4
DEC_BATCH = 8
DEC_SEQ = 64
PAST_LEN = 2048
CHUNK = 64
NORM_EPS = 1e-6
FFN_DIM = 5632
PLE_DIM = 256
SSD_INNER = 4096
SSD_HEADS = 64
SSD_HEAD_DIM = 64
SSD_GROUPS = 8
SSD_STATE = 128
SSD_GROUP_WIDTH = SSD_INNER // SSD_GROUPS
SSD_CONV_DIM = SSD_INNER + 2 * SSD_GROUPS * SSD_STATE
SSD_MAIN_WIDTH = SSD_INNER + SSD_CONV_DIM
SSD_EXT_WIDTH = SSD_MAIN_WIDTH + SSD_INNER
DIFF_HEADS = 8
DIFF_HEAD_DIM = 128
BAND_HEADS = 16
BAND_HEAD_DIM = 128
BAND_WINDOW = 512
REL_CLIP = 256

N_PROMPT = BATCH * SEQ
N_SAMPLE = DEC_BATCH * DEC_SEQ
N_TOK = N_PROMPT + N_SAMPLE
CHUNKS_PER_SEQ = SEQ // CHUNK
N_PROMPT_CHUNKS = N_PROMPT // CHUNK
N_CHUNKS = N_TOK // CHUNK

TM_WIDE = 1088
TM = 544
TM_SPLIT = 512
N_PROMPT_TILES = N_PROMPT // TM_SPLIT
VMEM_LIMIT = 56 * 1024 * 1024
NEG = -1e30


def _params(*sem, vmem=VMEM_LIMIT):
    return pltpu.CompilerParams(dimension_semantics=sem, vmem_limit_bytes=vmem)


def _rms_rows(x, g):
    ms = jnp.mean(x * x, axis=-1, keepdims=True)
    return x * lax.rsqrt(ms + NORM_EPS) * g


def _dot(a, b):
    return jnp.dot(a, b, preferred_element_type=F32)


def _dot_nt(a, b):
    return lax.dot_general(a, b, (((1,), (1,)), ((), ())), preferred_element_type=F32)


def _dot_tn(a, b):
    return lax.dot_general(a, b, (((0,), (0,)), ((), ())), preferred_element_type=F32)


def _ffn_kernel(h_ref, g_ref, wg_ref, wu_ref, wd_ref, o_ref, xn_ref):
    f = pl.program_id(1)

    @pl.when(f == 0)
    def _():
        x = h_ref[...]
        xn_ref[...] = _rms_rows(x, g_ref[...]).astype(BF16)
        o_ref[...] = x

    xn = xn_ref[...]
    gate = _dot(xn, wg_ref[...])
    up = _dot(xn, wu_ref[...])
    act = (0.5 * (gate * jax.nn.sigmoid(gate)) * up).astype(BF16)
    o_ref[...] += _dot(act, wd_ref[...])


def ffn_half(h, g_all, wg_all, wu_all, wd_all, layer, half, *, tf=512):
    tm = TM
    return pl.pallas_call(
        _ffn_kernel,
        out_shape=jax.ShapeDtypeStruct((N_TOK, D_MODEL), F32),
        grid=(N_TOK // tm, FFN_DIM // tf),
        in_specs=[
            pl.BlockSpec((tm, D_MODEL), lambda i, f: (i, 0)),
            pl.BlockSpec((None, None, 1, D_MODEL), lambda i, f: (layer, half, 0, 0)),
            pl.BlockSpec((None, None, D_MODEL, tf), lambda i, f: (layer, half, 0, f)),
            pl.BlockSpec((None, None, D_MODEL, tf), lambda i, f: (layer, half, 0, f)),
            pl.BlockSpec((None, None, tf, D_MODEL), lambda i, f: (layer, half, f, 0)),
        ],
        out_specs=pl.BlockSpec((tm, D_MODEL), lambda i, f: (i, 0)),
        scratch_shapes=[pltpu.VMEM((tm, D_MODEL), BF16)],
        compiler_params=_params("parallel", "arbitrary"),
        name="ffn_half",
    )(h, g_all.reshape(DEPTH, 2, 1, D_MODEL), wg_all, wu_all, wd_all)


def _inproj_kernel(h_ref, g_ref, w_ref, hg_ref, *rest, norm_tiles, tn, has_extra):
    if has_extra:
        wx_ref, o_ref, ox_ref, xn_ref = rest
    else:
        o_ref, xn_ref = rest
    j = pl.program_id(1)

    @pl.when(j == 0)
    def _():
        xn = _rms_rows(h_ref[...], g_ref[...]).astype(BF16)
        xn_ref[...] = xn
        if has_extra:
            ox_ref[...] = _dot(xn, wx_ref[...])

    y = _dot(xn_ref[...], w_ref[...])
    if norm_tiles == 0:
        o_ref[...] = y
        return

    @pl.when(j >= norm_tiles)
    def _():
        o_ref[...] = y

    @pl.when(j < norm_tiles)
    def _():
        for c in range(tn // 128):
            sl = slice(c * 128, (c + 1) * 128)
            o_ref[:, sl] = _rms_rows(y[:, sl], hg_ref[:, sl])


def inproj(h, g, w_all, layer, head_gain=None, *, n_cols=None, norm_cols=0, w_extra=None, tn=1024):
    tm = TM_WIDE
    n = w_all.shape[2] if n_cols is None else n_cols
    if head_gain is None:
        head_gain = jnp.ones((1, n), F32)
    has_extra = w_extra is not None
    kern = functools.partial(_inproj_kernel, norm_tiles=norm_cols // tn, tn=tn, has_extra=has_extra)
    in_specs = [
        pl.BlockSpec((tm, D_MODEL), lambda i, j: (i, 0)),
        pl.BlockSpec((1, D_MODEL), lambda i, j: (0, 0)),
        pl.BlockSpec((None, D_MODEL, tn), lambda i, j: (layer, 0, j)),
        pl.BlockSpec((1, tn), lambda i, j: (0, j)),
    ]
    args = [h, g.reshape(1, D_MODEL), w_all, head_gain]
    out_shape = jax.ShapeDtypeStruct((N_TOK, n), F32)
    out_specs = pl.BlockSpec((tm, tn), lambda i, j: (i, j))
    if has_extra:
        nx = w_extra.shape[2]
        in_specs.append(pl.BlockSpec((None, D_MODEL, nx), lambda i, j: (layer, 0, 0)))
        args.append(w_extra)
        out_shape = (out_shape, jax.ShapeDtypeStruct((N_TOK, nx), F32))
        out_specs = (out_specs, pl.BlockSpec((tm, nx), lambda i, j: (i, 0)))
    return pl.pallas_call(
        kern,
        out_shape=out_shape,
        grid=(N_TOK // tm, n // tn),
        in_specs=in_specs,
        out_specs=out_specs,
        scratch_shapes=[pltpu.VMEM((tm, D_MODEL), BF16)],
        compiler_params=_params("parallel", "arbitrary"),
        name="inproj",
    )(*args)


def _outproj_kernel(a_ref, w_ref, h_ref, o_ref):
    o_ref[...] = h_ref[...] + _dot(a_ref[...], w_ref[...])


def _outproj_split_kernel(ap_ref, as_ref, w_ref, h_ref, o_ref):
    i = pl.program_id(1)

    @pl.when(i < N_PROMPT_TILES)
    def _():
        o_ref[...] = h_ref[...] + _dot(ap_ref[...], w_ref[...])

    @pl.when(i >= N_PROMPT_TILES)
    def _():
        o_ref[...] = h_ref[...] + _dot(as_ref[...], w_ref[...])


def outproj(h, a, w_all, layer, *, tn=1024):
    tm = TM_SPLIT
    k = w_all.shape[1]
    w_spec = pl.BlockSpec((None, k, tn), lambda j, i: (layer, 0, j))
    h_spec = pl.BlockSpec((tm, tn), lambda j, i: (i, j))
    if isinstance(a, tuple):
        kern = _outproj_split_kernel
        a_specs = [pl.BlockSpec((tm, k), lambda j, i: (jnp.minimum(i, N_PROMPT_TILES - 1), 0)),
                   pl.BlockSpec((tm, k), lambda j, i: (0, 0))]
        args = list(a)
    else:
        kern = _outproj_kernel
        a_specs = [pl.BlockSpec((tm, k), lambda j, i: (i, 0))]
        args = [a]
    return pl.pallas_call(
        kern,
        out_shape=jax.ShapeDtypeStruct((N_TOK, D_MODEL), F32),
        grid=(D_MODEL // tn, N_TOK // tm),
        in_specs=a_specs + [w_spec, h_spec],
        out_specs=h_spec,
        compiler_params=_params("parallel", "arbitrary"),
        name="outproj",
    )(*args, w_all, h)


def _ple_update(h_ref, p_ref, g_ref, wp_ref, wgt_ref):
    x = h_ref[...]
    xn = _rms_rows(x, g_ref[...]).astype(BF16)
    gate = jax.nn.sigmoid(_dot(xn, wgt_ref[...]))
    emb = _dot(p_ref[...].astype(BF16), wp_ref[...])
    return x + emb * gate


def _ple_kernel(h_ref, p_ref, g_ref, wp_ref, wgt_ref, o_ref):
    o_ref[...] = _ple_update(h_ref, p_ref, g_ref, wp_ref, wgt_ref)


def _ple_split_kernel(h_ref, p_ref, g_ref, wp_ref, wgt_ref, op_ref, os_ref):
    i = pl.program_id(0)

    @pl.when(i < N_PROMPT_TILES)
    def _():
        op_ref[...] = _ple_update(h_ref, p_ref, g_ref, wp_ref, wgt_ref)

    @pl.when(i >= N_PROMPT_TILES)
    def _():
        os_ref[...] = _ple_update(h_ref, p_ref, g_ref, wp_ref, wgt_ref)


def ple_add(h, p_all, g_all, w_ple_all, w_gate_all, layer, *, split=False):
    tm = TM_SPLIT if split else TM
    in_specs = [
        pl.BlockSpec((tm, D_MODEL), lambda i: (i, 0)),
        pl.BlockSpec((None, tm, PLE_DIM), lambda i: (layer, i, 0)),
        pl.BlockSpec((None, 1, D_MODEL), lambda i: (layer, 0, 0)),
        pl.BlockSpec((None, PLE_DIM, D_MODEL), lambda i: (layer, 0, 0)),
        pl.BlockSpec((None, D_MODEL, D_MODEL), lambda i: (layer, 0, 0)),
    ]
    if split:
        kern = _ple_split_kernel
        out_shape = (jax.ShapeDtypeStruct((N_PROMPT, D_MODEL), F32), jax.ShapeDtypeStruct((N_SAMPLE, D_MODEL), F32))
        out_specs = (pl.BlockSpec((tm, D_MODEL), lambda i: (jnp.minimum(i, N_PROMPT_TILES - 1), 0)),
                     pl.BlockSpec((tm, D_MODEL), lambda i: (0, 0)))
    else:
        kern = _ple_kernel
        out_shape = jax.ShapeDtypeStruct((N_TOK, D_MODEL), F32)
        out_specs = pl.BlockSpec((tm, D_MODEL), lambda i: (i, 0))
    return pl.pallas_call(
        kern,
        out_shape=out_shape,
        grid=(N_TOK // tm,),
        in_specs=in_specs,
        out_specs=out_specs,
        compiler_params=_params("arbitrary"),
        name="ple_add",
    )(h, p_all, g_all.reshape(DEPTH, 1, D_MODEL), w_ple_all, w_gate_all)


def _ssd_kernel(zx_ref, dtr_ref, st0_ref, cv0_ref, cw_ref, cb_ref, dtb_ref, alog_ref, rep_ref, dsk_ref, gn_ref,
                y_ref, st_ref, cvt_ref, s_scr, xpad, act, pcum, dts):
    s = pl.program_id(0)
    is_sample = s >= N_PROMPT_CHUNKS
    pos = s % CHUNKS_PER_SEQ
    first = jnp.logical_or(is_sample, pos == 0)
    last = jnp.logical_or(is_sample, pos == CHUNKS_PER_SEQ - 1)

    @pl.when(jnp.logical_and(first, jnp.logical_not(is_sample)))
    def _():
        s_scr[...] = jnp.zeros_like(s_scr)
        xpad[0:8, :] = jnp.zeros((8, SSD_CONV_DIM), F32)

    @pl.when(is_sample)
    def _():
        for g in range(SSD_GROUPS):
            s_scr[g] = st0_ref[0, g].T
        xpad[0:8, :] = jnp.zeros((8, SSD_CONV_DIM), F32)
        xpad[5:8, :] = cv0_ref[0]

    xpad[8:72, :] = zx_ref[:, SSD_INNER:SSD_MAIN_WIDTH]
    conv = cb_ref[...] + cw_ref[3:4, :] * xpad[8:72, :]
    conv = conv + cw_ref[2:3, :] * xpad[7:71, :]
    conv = conv + cw_ref[1:2, :] * xpad[6:70, :]
    conv = conv + cw_ref[0:1, :] * xpad[5:69, :]
    act[...] = conv * jax.nn.sigmoid(conv)
    xpad[0:8, :] = xpad[64:72, :]

    x = dtr_ref[...] + dtb_ref[...]
    dt = jnp.maximum(x, 0.0) + jnp.log1p(jnp.exp(-jnp.abs(x)))
    run = dt * (-jnp.exp(alog_ref[...]))
    row = lax.broadcasted_iota(jnp.int32, (CHUNK, 128), 0)
    for k in (1, 2, 4, 8, 16, 32):
        run = run + jnp.where(row >= k, pltpu.roll(run, k, 0), 0.0)
    both = jnp.concatenate([dt, run], axis=0)
    hi = both.astype(BF16)
    r1 = both - hi.astype(F32)
    mid = r1.astype(BF16)
    lo = (r1 - mid.astype(F32)).astype(BF16)
    wide = _dot(jnp.concatenate([hi, mid, lo], axis=0), rep_ref[...])
    wide = (wide[0:2 * CHUNK] + wide[2 * CHUNK:4 * CHUNK]) + wide[4 * CHUNK:6 * CHUNK]
    dts[...] = wide[0:CHUNK]
    pcum[...] = wide[CHUNK:2 * CHUNK]

    gw = SSD_GROUP_WIDTH
    row_g = lax.broadcasted_iota(jnp.int32, (CHUNK, gw), 0)
    lane_g = lax.broadcasted_iota(jnp.int32, (CHUNK, gw), 1) % CHUNK
    diag = row_g == lane_g
    causal = row_g >= lane_g
    r4 = lax.broadcasted_iota(jnp.int32, (256, 256), 0) // CHUNK
    c4 = lax.broadcasted_iota(jnp.int32, (256, 256), 1) // SSD_HEAD_DIM
    head_diag = r4 == c4

    for g in range(SSD_GROUPS):
        cs = slice(g * gw, (g + 1) * gw)
        p_g = pcum[:, cs]
        dt_g = dts[:, cs]
        x_g = act[:, cs]
        b_g = act[:, SSD_INNER + g * SSD_STATE:SSD_INNER + (g + 1) * SSD_STATE].astype(BF16)
        c_g = act[:, SSD_INNER + SSD_GROUPS * SSD_STATE + g * SSD_STATE:
                  SSD_INNER + SSD_GROUPS * SSD_STATE + (g + 1) * SSD_STATE].astype(BF16)
        p_s = jnp.sum(jnp.where(diag, p_g, 0.0), axis=0, keepdims=True)
        dt_s = jnp.sum(jnp.where(diag, dt_g, 0.0), axis=0, keepdims=True)
        p_last = p_g[CHUNK - 1:CHUNK, :]
        cb = _dot_nt(c_g, jnp.concatenate([b_g] * 8, axis=0))
        m = (cb * jnp.exp(jnp.where(causal, p_g - p_s, NEG)) * dt_s).astype(BF16)
        x_bf = x_g.astype(BF16)
        halves = []
        for hh in range(2):
            hs = slice(hh * 256, (hh + 1) * 256)
            xh = x_bf[:, hs]
            x_bd = jnp.where(head_diag, jnp.concatenate([xh] * 4, axis=0), jnp.zeros((), BF16))
            halves.append(_dot(m[:, hs], x_bd))
        y = jnp.concatenate(halves, axis=1)
        st = s_scr[g]
        y = y + _dot(c_g, st.astype(BF16)) * jnp.exp(p_g)
        wx = (jnp.exp(p_last - p_g) * dt_g * x_g).astype(BF16)
        s_scr[g] = st * jnp.exp(p_last) + _dot_tn(b_g, wx)
        y = y + dsk_ref[:, cs] * x_g
        z = zx_ref[:, cs]
        y = y * (z * jax.nn.sigmoid(z))
        y_ref[:, cs] = _rms_rows(y, gn_ref[:, cs]).astype(BF16)

    @pl.when(last)
    def _():
        cvt_ref[0] = xpad[0:8, :]
        for g in range(SSD_GROUPS):
            st_ref[0, g] = s_scr[g].T


def _ssd_seq(s):
    return jnp.where(s < N_PROMPT_CHUNKS, s // CHUNKS_PER_SEQ, BATCH + s - N_PROMPT_CHUNKS)


def ssd_core(zx, dt_raw, state_all, conv_all, conv_w_all, layer, conv_b, dt_bias, a_log, d_skip, g_norm):
    n_seq = BATCH + DEC_BATCH
    pad_heads = lambda v: jnp.pad(v, (0, 128 - SSD_HEADS)).reshape(1, 128)
    samp = lambda s: jnp.maximum(s - N_PROMPT_CHUNKS, 0)
    row_spec = lambda w: pl.BlockSpec((1, w), lambda s: (0, 0))
    st_all = state_all.reshape(-1, DEC_BATCH, SSD_GROUPS, SSD_GROUP_WIDTH, SSD_STATE)
    head_of_channel = jnp.arange(SSD_INNER, dtype=jnp.int32) // SSD_HEAD_DIM
    rep = (jnp.arange(128, dtype=jnp.int32)[:, None] == head_of_channel[None, :]).astype(BF16)
    y, st, cvt = pl.pallas_call(
        _ssd_kernel,
        out_shape=(jax.ShapeDtypeStruct((N_TOK, SSD_INNER), BF16),
                   jax.ShapeDtypeStruct((n_seq, SSD_GROUPS, SSD_GROUP_WIDTH, SSD_STATE), F32),
                   jax.ShapeDtypeStruct((n_seq, 8, SSD_CONV_DIM), F32)),
        grid=(N_CHUNKS,),
        in_specs=[
            pl.BlockSpec((CHUNK, SSD_MAIN_WIDTH), lambda s: (s, 0)),
            pl.BlockSpec((CHUNK, 128), lambda s: (s, 0)),
            pl.BlockSpec((None, 1, SSD_GROUPS, SSD_GROUP_WIDTH, SSD_STATE), lambda s: (layer, samp(s), 0, 0, 0)),
            pl.BlockSpec((None, 1, 3, SSD_CONV_DIM), lambda s: (layer, samp(s), 0, 0)),
            pl.BlockSpec((None, 4, SSD_CONV_DIM), lambda s: (layer, 0, 0)),
            row_spec(SSD_CONV_DIM), row_spec(128), row_spec(128),
            pl.BlockSpec((128, SSD_INNER), lambda s: (0, 0)),
            row_spec(SSD_INNER), row_spec(SSD_INNER),
        ],
        out_specs=(pl.BlockSpec((CHUNK, SSD_INNER), lambda s: (s, 0)),
                   pl.BlockSpec((1, SSD_GROUPS, SSD_GROUP_WIDTH, SSD_STATE), lambda s: (_ssd_seq(s), 0, 0, 0)),
                   pl.BlockSpec((1, 8, SSD_CONV_DIM), lambda s: (_ssd_seq(s), 0, 0))),
        scratch_shapes=[
            pltpu.VMEM((SSD_GROUPS, SSD_STATE, SSD_GROUP_WIDTH), F32),
            pltpu.VMEM((CHUNK + 8, SSD_CONV_DIM), F32),
            pltpu.VMEM((CHUNK, SSD_CONV_DIM), F32),
            pltpu.VMEM((CHUNK, SSD_INNER), F32),
            pltpu.VMEM((CHUNK, SSD_INNER), F32),
        ],
        compiler_params=_params("arbitrary"),
        name="ssd_core",
    )(zx, dt_raw, st_all, conv_all, conv_w_all, conv_b.reshape(1, SSD_CONV_DIM), pad_heads(dt_bias),
      pad_heads(a_log), rep, jnp.repeat(d_skip, SSD_HEAD_DIM).reshape(1, SSD_INNER), g_norm.reshape(1, SSD_INNER))
    return y, st.reshape(n_seq, SSD_HEADS, SSD_HEAD_DIM, SSD_STATE), cvt[:, 5:8]


DIFF_SCALE = DIFF_HEAD_DIM ** -0.5
LOG2E = math.log2(math.e)


def _diff_lambda(lp_ref, lam_init):
    lp = lp_ref[...]
    a = jnp.sum(lp[0:1] * lp[1:2], axis=-1, keepdims=True)
    b = jnp.sum(lp[2:3] * lp[3:4], axis=-1, keepdims=True)
    return jnp.exp(a) - jnp.exp(b) + lam_init


def _alibi_slope(h):
    return jnp.exp2(-(jnp.zeros((1, 1), F32) + (h + 1).astype(F32)))


def _diff_prompt_kernel(q_ref, k_ref, v_ref, lp_ref, gs_ref, o_ref, *, lam_init, tq):
    h = pl.program_id(1)
    lam = _diff_lambda(lp_ref, lam_init)
    slope2 = _alibi_slope(h) * LOG2E
    c = DIFF_SCALE * LOG2E
    d = DIFF_HEAD_DIM
    t_i = lax.broadcasted_iota(jnp.int32, (tq, tq), 0)
    j_i = lax.broadcasted_iota(jnp.int32, (tq, tq), 1)
    bias_own = jnp.where(j_i // CHUNK <= t_i // CHUNK,
                         slope2 * (t_i - jnp.abs(t_i - j_i)).astype(F32), NEG)
    for qb in range(SEQ // tq):
        q0 = qb * tq
        rows = slice(q0, q0 + tq)
        if qb:
            bias_past = slope2 * (lax.broadcasted_iota(jnp.int32, (1, q0), 1) - q0).astype(F32)
        weights = []
        for m in range(2):
            cols = slice(m * d, (m + 1) * d)
            q = q_ref[rows, cols].astype(BF16)
            s_own = _dot_nt(q, k_ref[rows, cols].astype(BF16)) * c + bias_own
            mx = jnp.max(s_own, axis=-1, keepdims=True)
            if qb:
                s_past = _dot_nt(q, k_ref[0:q0, cols].astype(BF16)) * c + bias_past
                mx = jnp.maximum(mx, jnp.max(s_past, axis=-1, keepdims=True))
                e_past = jnp.exp2(s_past - mx)
            e_own = jnp.exp2(s_own - mx)
            den = jnp.sum(e_own, axis=-1, keepdims=True)
            if qb:
                den = den + jnp.sum(e_past, axis=-1, keepdims=True)
            weights.append((e_own, e_past if qb else None, 1.0 / den))
        (eo0, ep0, r0), (eo1, ep1, r1) = weights
        r1 = lam * r1
        o = _dot((eo0 * r0 - eo1 * r1).astype(BF16), v_ref[rows, :].astype(BF16))
        if qb:
            o = o + _dot((ep0 * r0 - ep1 * r1).astype(BF16), v_ref[0:q0, :].astype(BF16))
        o_ref[rows, :] = (_rms_rows(o, gs_ref[...]) * (1.0 - lam_init)).astype(BF16)


def diff_attn_prompt(qkv, lam_p, g_sub, lam_init, *, tq=256):
    w = 2 * DIFF_HEAD_DIM
    kern = functools.partial(_diff_prompt_kernel, lam_init=lam_init, tq=tq)
    return pl.pallas_call(
        kern,
        out_shape=jax.ShapeDtypeStruct((N_PROMPT, D_MODEL), BF16),
        grid=(BATCH, DIFF_HEADS),
        in_specs=[
            pl.BlockSpec((SEQ, w), lambda b, h: (b, h)),
            pl.BlockSpec((SEQ, w), lambda b, h: (b, DIFF_HEADS + h)),
            pl.BlockSpec((SEQ, w), lambda b, h: (b, 2 * DIFF_HEADS + h)),
            pl.BlockSpec((4, DIFF_HEAD_DIM), lambda b, h: (0, 0)),
            pl.BlockSpec((1, w), lambda b, h: (0, 0)),
        ],
        out_specs=pl.BlockSpec((SEQ, w), lambda b, h: (b, h)),
        compiler_params=_params("parallel", "parallel"),
        name="diff_attn_prompt",
    )(qkv, qkv, qkv, lam_p, g_sub.reshape(1, w))


def _diff_sample_kernel(q_ref, kn_ref, vn_ref, kc0_ref, kc1_ref, vc0_ref, vc1_ref, lp_ref, gs_ref, o_ref, vh_ref,
                        *, lam_init):
    kc_refs = (kc0_ref, kc1_ref)
    h = pl.program_id(1)
    lam = _diff_lambda(lp_ref, lam_init)
    slope = _alibi_slope(h)
    d = DIFF_HEAD_DIM
    t = DEC_SEQ
    tq_p = lax.broadcasted_iota(jnp.int32, (t, PAST_LEN), 0)
    kp_p = lax.broadcasted_iota(jnp.int32, (t, PAST_LEN), 1)
    bias_p = -slope * (PAST_LEN + tq_p - kp_p).astype(F32)
    tq_n = lax.broadcasted_iota(jnp.int32, (t, t), 0)
    kp_n = lax.broadcasted_iota(jnp.int32, (t, t), 1)
    bias_n = -slope * jnp.abs(tq_n - kp_n).astype(F32)
    pp, pn = [], []
    for m in range(2):
        cols = slice(m * d, (m + 1) * d)
        q = q_ref[:, cols].astype(BF16)
        kc = kc_refs[m].reshape(PAST_LEN, d)[...]
        sp = _dot_nt(q, kc.astype(BF16)) * DIFF_SCALE + bias_p
        sn = _dot_nt(q, kn_ref[:, cols].astype(BF16)) * DIFF_SCALE + bias_n
        mx = jnp.maximum(jnp.max(sp, axis=-1, keepdims=True), jnp.max(sn, axis=-1, keepdims=True))
        ep = jnp.exp(sp - mx)
        en = jnp.exp(sn - mx)
        den = jnp.sum(ep, axis=-1, keepdims=True) + jnp.sum(en, axis=-1, keepdims=True)
        pp.append(ep / den)
        pn.append(en / den)
    ap = (pp[0] - lam * pp[1]).astype(BF16)
    an = (pn[0] - lam * pn[1]).astype(BF16)
    for k in range(DIFF_HEADS):
        @pl.when(h == k)
        def _(k=k):
            for half, vc_ref in enumerate((vc0_ref, vc1_ref)):
                rows = vc_ref.reshape(PAST_LEN * DIFF_HEADS, d)
                vh_ref[:, half * d:(half + 1) * d] = rows[pl.ds(k, PAST_LEN, stride=DIFF_HEADS), :].astype(BF16)

    o = _dot(ap, vh_ref[...]) + _dot(an, vn_ref[...].astype(BF16))
    o_ref[...] = (_rms_rows(o, gs_ref[...]) * (1.0 - lam_init)).astype(BF16)


def diff_attn_sample(qkv, cache_k, cache_v, lam_p, g_sub, lam_init):
    w = 2 * DIFF_HEAD_DIM
    r0 = N_PROMPT // DEC_SEQ
    kern = functools.partial(_diff_sample_kernel, lam_init=lam_init)
    k_rows = cache_k.reshape(DEC_BATCH, PAST_LEN, 2 * DIFF_HEADS, 1, DIFF_HEAD_DIM)
    k_spec = lambda m: pl.BlockSpec((None, PAST_LEN, None, 1, DIFF_HEAD_DIM), lambda b, h: (b, 0, 2 * h + m, 0, 0))
    v_spec = lambda half: pl.BlockSpec((None, PAST_LEN, DIFF_HEADS, DIFF_HEAD_DIM), lambda b, h: (b, 0, 0, half))
    return pl.pallas_call(
        kern,
        out_shape=jax.ShapeDtypeStruct((N_SAMPLE, D_MODEL), BF16),
        grid=(DEC_BATCH, DIFF_HEADS),
        in_specs=[
            pl.BlockSpec((DEC_SEQ, w), lambda b, h: (r0 + b, h)),
            pl.BlockSpec((DEC_SEQ, w), lambda b, h: (r0 + b, DIFF_HEADS + h)),
            pl.BlockSpec((DEC_SEQ, w), lambda b, h: (r0 + b, 2 * DIFF_HEADS + h)),
            k_spec(0),
            k_spec(1),
            v_spec(0),
            v_spec(1),
            pl.BlockSpec((4, DIFF_HEAD_DIM), lambda b, h: (0, 0)),
            pl.BlockSpec((1, w), lambda b, h: (0, 0)),
        ],
        out_specs=pl.BlockSpec((DEC_SEQ, w), lambda b, h: (b, h)),
        scratch_shapes=[pltpu.VMEM((PAST_LEN, w), BF16)],
        compiler_params=_params("parallel", "arbitrary"),
        name="diff_attn_sample",
    )(qkv, qkv, qkv, k_rows, k_rows, cache_v, cache_v, lam_p, g_sub.reshape(1, w))


BAND_SCALE = BAND_HEAD_DIM ** -0.5
BAND_TQ = 256
BAND_E = 1024


def _band_bias(e_ref, tq):
    eb = jnp.broadcast_to(e_ref[0], (tq, BAND_E))
    return pltpu.roll(eb, BAND_E - (BAND_TQ - 1), 1, stride=1, stride_axis=0)


def _band_prompt_kernel(q_ref, k_ref, v_ref, e_ref, o_ref):
    tq = BAND_TQ
    win = BAND_WINDOW + tq
    bias = _band_bias(e_ref, tq)[:, :win]
    qc = lax.broadcasted_iota(jnp.int32, (tq, win), 0) // CHUNK
    jc = lax.broadcasted_iota(jnp.int32, (tq, win), 1) // CHUNK
    bias = jnp.where(jnp.logical_and(jc >= qc, jc <= qc + BAND_WINDOW // CHUNK), bias, NEG)
    for qb in range(SEQ // tq):
        q0 = qb * tq
        k0 = max(0, q0 - BAND_WINDOW)
        nk = q0 + tq - k0
        rows = slice(q0, q0 + tq)
        sc = _dot_nt(q_ref[rows, :].astype(BF16), k_ref[k0:k0 + nk, :].astype(BF16)) * BAND_SCALE
        sc = sc + bias[:, win - nk:]
        e = jnp.exp(sc - jnp.max(sc, axis=-1, keepdims=True))
        p = (e / jnp.sum(e, axis=-1, keepdims=True)).astype(BF16)
        o_ref[rows, :] = _dot(p, v_ref[k0:k0 + nk, :].astype(BF16)).astype(BF16)


def _band_e_rows(table):
    edge = jnp.broadcast_to(table[:, 2 * REL_CLIP:], (BAND_HEADS, BAND_E - (2 * REL_CLIP + 1)))
    return jnp.concatenate([edge, table[:, ::-1]], axis=1).reshape(BAND_HEADS, 1, BAND_E)


def band_attn_prompt(qkv, table):
    d = BAND_HEAD_DIM
    return pl.pallas_call(
        _band_prompt_kernel,
        out_shape=jax.ShapeDtypeStruct((N_PROMPT, D_MODEL), BF16),
        grid=(BATCH, BAND_HEADS),
        in_specs=[
            pl.BlockSpec((SEQ, d), lambda b, h: (b, h)),
            pl.BlockSpec((SEQ, d), lambda b, h: (b, BAND_HEADS + h)),
            pl.BlockSpec((SEQ, d), lambda b, h: (b, 2 * BAND_HEADS + h)),
            pl.BlockSpec((1, 1, BAND_E), lambda b, h: (h, 0, 0)),
        ],
        out_specs=pl.BlockSpec((SEQ, d), lambda b, h: (b, h)),
        compiler_params=_params("parallel", "parallel"),
        name="band_attn_prompt",
    )(qkv, qkv, qkv, _band_e_rows(table))


def _band_sample_kernel(q_ref, kn_ref, vn_ref, kc_ref, vc_ref, e_ref, o_ref):
    t = DEC_SEQ
    bias = _band_bias(e_ref, t)
    q = q_ref[...].astype(BF16)
    kc = kc_ref.reshape(BAND_WINDOW, BAND_HEAD_DIM)[...]
    vc = vc_ref.reshape(BAND_WINDOW, BAND_HEAD_DIM)[...]
    sp = _dot_nt(q, kc.astype(BF16)) * BAND_SCALE + bias[:, :BAND_WINDOW]
    sn = _dot_nt(q, kn_ref[...].astype(BF16)) * BAND_SCALE + bias[:, BAND_WINDOW:BAND_WINDOW + t]
    mx = jnp.maximum(jnp.max(sp, axis=-1, keepdims=True), jnp.max(sn, axis=-1, keepdims=True))
    ep = jnp.exp(sp - mx)
    en = jnp.exp(sn - mx)
    den = jnp.sum(ep, axis=-1, keepdims=True) + jnp.sum(en, axis=-1, keepdims=True)
    o = (_dot((ep / den).astype(BF16), vc.astype(BF16))
         + _dot((en / den).astype(BF16), vn_ref[...].astype(BF16)))
    o_ref[...] = o.astype(BF16)


def band_attn_sample(qkv, cache_k, cache_v, table):
    d = BAND_HEAD_DIM
    r0 = N_PROMPT // DEC_SEQ
    rows_view = (DEC_BATCH, BAND_WINDOW, BAND_HEADS, 1, d)
    cache_spec = pl.BlockSpec((None, BAND_WINDOW, None, 1, d), lambda b, h: (b, 0, h, 0, 0))
    return pl.pallas_call(
        _band_sample_kernel,
        out_shape=jax.ShapeDtypeStruct((N_SAMPLE, D_MODEL), BF16),
        grid=(DEC_BATCH, BAND_HEADS),
        in_specs=[
            pl.BlockSpec((DEC_SEQ, d), lambda b, h: (r0 + b, h)),
            pl.BlockSpec((DEC_SEQ, d), lambda b, h: (r0 + b, BAND_HEADS + h)),
            pl.BlockSpec((DEC_SEQ, d), lambda b, h: (r0 + b, 2 * BAND_HEADS + h)),
            cache_spec,
            cache_spec,
            pl.BlockSpec((1, 1, BAND_E), lambda b, h: (h, 0, 0)),
        ],
        out_specs=pl.BlockSpec((DEC_SEQ, d), lambda b, h: (b, h)),
        compiler_params=_params("parallel", "parallel"),
        name="band_attn_sample",
    )(qkv, qkv, qkv, cache_k.reshape(rows_view), cache_v.reshape(rows_view), _band_e_rows(table))


def kernel(x_prompt, x_sample, p_prompt, p_sample, state_ssm, state_conv, cache_k_diff, cache_v_diff,
           cache_k_band, cache_v_band, g_ffn, w_ffn_gate, w_ffn_up, w_ffn_down, g_mix,
           ssd_w_in, ssd_conv_w, ssd_conv_b, ssd_dt_bias, ssd_a_log, ssd_d, ssd_g_norm, ssd_w_out,
           diff_w_in, diff_g_q, diff_g_k, diff_lambda, diff_g_sub, diff_w_out,
           band_w_in, band_g_q, band_g_k, band_rel_bias, band_w_out,
           g_ple, w_ple, w_ple_gate):
    h = jnp.concatenate([x_prompt.reshape(N_PROMPT, D_MODEL), x_sample.reshape(N_SAMPLE, D_MODEL)], axis=0)
    p_all = jnp.concatenate([p_prompt.reshape(DEPTH, N_PROMPT, PLE_DIM),
                             p_sample.reshape(DEPTH, N_SAMPLE, PLE_DIM)], axis=1)
    bf = lambda w: w.astype(BF16)
    wg_all, wu_all, wd_all = bf(w_ffn_gate), bf(w_ffn_up), bf(w_ffn_down)
    w_ple_all, w_gate_all = bf(w_ple), bf(w_ple_gate)
    ssd_w_main = bf(ssd_w_in)
    ssd_w_dt = bf(jnp.pad(ssd_w_in[:, :, SSD_MAIN_WIDTH:], ((0, 0), (0, 0), (0, 128 - SSD_HEADS))))
    ssd_w_out_all, diff_w_in_all, diff_w_out_all = bf(ssd_w_out), bf(diff_w_in), bf(diff_w_out)
    band_w_in_all, band_w_out_all = bf(band_w_in), bf(band_w_out)

    def prompt_rows(a):
        return a[:N_PROMPT].reshape(BATCH, SEQ, -1)

    def sample_rows(a):
        return a[N_PROMPT:].reshape(DEC_BATCH, DEC_SEQ, -1)

    ssm_p, conv_p, ssm_s, conv_s = [], [], [], []
    kd_p = vd_p = kb_p = vb_p = kd_s = vd_s = kb_s = vb_s = None
    for i in range(DEPTH):
        kind, j = i % 3, i // 3
        h = ffn_half(h, g_ffn, wg_all, wu_all, wd_all, i, 0)
        if kind == 0:
            zx, dt_raw = inproj(h, g_mix[i], ssd_w_main, j, n_cols=SSD_MAIN_WIDTH, w_extra=ssd_w_dt)
            y, st, cvt = ssd_core(zx, dt_raw, state_ssm, state_conv, ssd_conv_w, j, ssd_conv_b[j], ssd_dt_bias[j],
                                  ssd_a_log[j], ssd_d[j], ssd_g_norm[j])
            h = outproj(h, y, ssd_w_out_all, j)
            ssm_p.append(st[:BATCH])
            ssm_s.append(st[BATCH:])
            conv_p.append(cvt[:BATCH])
            conv_s.append(cvt[BATCH:])
        elif kind == 1:
            lam_init = 0.8 - 0.6 * math.exp(-0.3 * i)
            gains = jnp.concatenate([jnp.tile(diff_g_q[j].reshape(-1), DIFF_HEADS),
                                     jnp.tile(diff_g_k[j].reshape(-1), DIFF_HEADS),
                                     jnp.ones((D_MODEL,), F32)]).reshape(1, 3 * D_MODEL)
            qkv = inproj(h, g_mix[i], diff_w_in_all, j, gains, norm_cols=2 * D_MODEL)
            o_p = diff_attn_prompt(qkv, diff_lambda[j], diff_g_sub[j], lam_init)
            o_s = diff_attn_sample(qkv, cache_k_diff[j], cache_v_diff[j], diff_lambda[j], diff_g_sub[j], lam_init)
            h = outproj(h, (o_p, o_s), diff_w_out_all, j)
            k_new, v_new = qkv[:, D_MODEL:2 * D_MODEL], qkv[:, 2 * D_MODEL:]
            kd_p = prompt_rows(k_new).reshape(1, BATCH, SEQ, DIFF_HEADS, 2, DIFF_HEAD_DIM)
            vd_p = prompt_rows(v_new).reshape(1, BATCH, SEQ, DIFF_HEADS, 2 * DIFF_HEAD_DIM)
            kd_s = sample_rows(k_new).reshape(1, DEC_BATCH, DEC_SEQ, DIFF_HEADS, 2, DIFF_HEAD_DIM)
            vd_s = sample_rows(v_new).reshape(1, DEC_BATCH, DEC_SEQ, DIFF_HEADS, 2 * DIFF_HEAD_DIM)
        else:
            gains = jnp.concatenate([jnp.tile(band_g_q[j], BAND_HEADS), jnp.tile(band_g_k[j], BAND_HEADS),
                                     jnp.ones((D_MODEL,), F32)]).reshape(1, 3 * D_MODEL)
            qkv = inproj(h, g_mix[i], band_w_in_all, j, gains, norm_cols=2 * D_MODEL)
            o_p = band_attn_prompt(qkv, band_rel_bias[j])
            o_s = band_attn_sample(qkv, cache_k_band[j], cache_v_band[j], band_rel_bias[j])
            h = outproj(h, (o_p, o_s), band_w_out_all, j)
            k_new, v_new = qkv[:, D_MODEL:2 * D_MODEL], qkv[:, 2 * D_MODEL:]
            kb_p = prompt_rows(k_new)[:, SEQ - BAND_WINDOW:].reshape(1, BATCH, BAND_WINDOW, BAND_HEADS, BAND_HEAD_DIM)
            vb_p = prompt_rows(v_new)[:, SEQ - BAND_WINDOW:].reshape(1, BATCH, BAND_WINDOW, BAND_HEADS, BAND_HEAD_DIM)
            kb_s = sample_rows(k_new).reshape(1, DEC_BATCH, DEC_SEQ, BAND_HEADS, BAND_HEAD_DIM)
            vb_s = sample_rows(v_new).reshape(1, DEC_BATCH, DEC_SEQ, BAND_HEADS, BAND_HEAD_DIM)
        h = ffn_half(h, g_ffn, wg_all, wu_all, wd_all, i, 1)
        h = ple_add(h, p_all, g_ple, w_ple_all, w_gate_all, i, split=(i == DEPTH - 1))
    h_p, h_s = h
    return (h_p.reshape(BATCH, SEQ, D_MODEL), h_s.reshape(DEC_BATCH, DEC_SEQ, D_MODEL),
            jnp.stack(ssm_p), jnp.stack(conv_p), kd_p, vd_p, kb_p, vb_p,
            jnp.stack(ssm_s), jnp.stack(conv_s), kd_s, vd_s, kb_s, vb_s)
```

```python
import functools
import math

import jax
import jax.numpy as jnp
from jax import lax
from jax.experimental import pallas as pl
from jax.experimental.pallas import tpu as pltpu

F32 = jnp.float32
BF16 = jnp.bfloat16

D_MODEL = 2048
BATCH = 4
SEQ = 2048
DEPTH = 4
DEC_BATCH = 8
DEC_SEQ = 64
PAST_LEN = 2048
CHUNK = 64
NORM_EPS = 1e-6
FFN_DIM = 5632
PLE_DIM = 256
SSD_INNER = 4096
SSD_HEADS = 64
SSD_HEAD_DIM = 64
SSD_GROUPS = 8
SSD_STATE = 128
SSD_GROUP_WIDTH = SSD_INNER // SSD_GROUPS
SSD_CONV_DIM = SSD_INNER + 2 * SSD_GROUPS * SSD_STATE
SSD_MAIN_WIDTH = SSD_INNER + SSD_CONV_DIM
SSD_EXT_WIDTH = SSD_MAIN_WIDTH + SSD_INNER
DIFF_HEADS = 8
DIFF_HEAD_DIM = 128
BAND_HEADS = 16
BAND_HEAD_DIM = 128
BAND_WINDOW = 512
REL_CLIP = 256

N_PROMPT = BATCH * SEQ
N_SAMPLE = DEC_BATCH * DEC_SEQ
N_TOK = N_PROMPT + N_SAMPLE
CHUNKS_PER_SEQ = SEQ // CHUNK
N_PROMPT_CHUNKS = N_PROMPT // CHUNK
N_CHUNKS = N_TOK // CHUNK

TM_WIDE = 1088
TM = 544
TM_SPLIT = 512
N_PROMPT_TILES = N_PROMPT // TM_SPLIT
VMEM_LIMIT = 56 * 1024 * 1024
NEG = -1e30


def _params(*sem, vmem=VMEM_LIMIT):
    return pltpu.CompilerParams(dimension_semantics=sem, vmem_limit_bytes=vmem)


def _rms_rows(x, g):
    ms = jnp.mean(x * x, axis=-1, keepdims=True)
    return x * lax.rsqrt(ms + NORM_EPS) * g


def _dot(a, b):
    return jnp.dot(a, b, preferred_element_type=F32)


def _dot_nt(a, b):
    return lax.dot_general(a, b, (((1,), (1,)), ((), ())), preferred_element_type=F32)


def _dot_tn(a, b):
    return lax.dot_general(a, b, (((0,), (0,)), ((), ())), preferred_element_type=F32)


def _ffn_step(first, h_ref, g_ref, wg, wu, wd, o_ref, xn_ref):
    @pl.when(first)
    def _():
        x = h_ref[...]
        xn_ref[...] = _rms_rows(x, g_ref[...]).astype(BF16)
        o_ref[...] = x

    xn = xn_ref[...]
    gate = _dot(xn, wg)
    up = _dot(xn, wu)
    act = (0.5 * (gate * jax.nn.sigmoid(gate)) * up).astype(BF16)
    o_ref[...] += _dot(act, wd)


def _ffn_first_kernel(h_ref, g_ref, wg_ref, wu_ref, wd_ref, o_ref, wg_bf_ref, wu_bf_ref, wd_bf_ref, xn_ref):
    wg = wg_ref[...].astype(BF16)
    wu = wu_ref[...].astype(BF16)
    wd = wd_ref[...].astype(BF16)
    wg_bf_ref[...] = wg
    wu_bf_ref[...] = wu
    wd_bf_ref[...] = wd
    _ffn_step(pl.program_id(0) == 0, h_ref, g_ref, wg, wu, wd, o_ref, xn_ref)


def _ffn_rest_kernel(h_ref, g_ref, wg_ref, wu_ref, wd_ref, partial_ref, o_ref, xn_ref):
    del partial_ref
    _ffn_step(pl.program_id(1) == 0, h_ref, g_ref, wg_ref[...], wu_ref[...], wd_ref[...], o_ref, xn_ref)


def ffn_half(h, g_all, wg_all, wu_all, wd_all, layer, half, *, tf=512, tf_first=256):
    tm = TM
    g4 = g_all.reshape(DEPTH, 2, 1, D_MODEL)
    g_spec1 = pl.BlockSpec((None, None, 1, D_MODEL), lambda f: (layer, half, 0, 0))
    partial, wg_bf, wu_bf, wd_bf = pl.pallas_call(
        _ffn_first_kernel,
        out_shape=(jax.ShapeDtypeStruct((N_TOK, D_MODEL), F32),
                   jax.ShapeDtypeStruct((D_MODEL, FFN_DIM), BF16),
                   jax.ShapeDtypeStruct((D_MODEL, FFN_DIM), BF16),
                   jax.ShapeDtypeStruct((FFN_DIM, D_MODEL), BF16)),
        grid=(FFN_DIM // tf_first,),
        in_specs=[
            pl.BlockSpec((tm, D_MODEL), lambda f: (0, 0)),
            g_spec1,
            pl.BlockSpec((None, None, D_MODEL, tf_first), lambda f: (layer, half, 0, f)),
            pl.BlockSpec((None, None, D_MODEL, tf_first), lambda f: (layer, half, 0, f)),
            pl.BlockSpec((None, None, tf_first, D_MODEL), lambda f: (layer, half, f, 0)),
        ],
        out_specs=(pl.BlockSpec((tm, D_MODEL), lambda f: (0, 0)),
                   pl.BlockSpec((D_MODEL, tf_first), lambda f: (0, f)),
                   pl.BlockSpec((D_MODEL, tf_first), lambda f: (0, f)),
                   pl.BlockSpec((tf_first, D_MODEL), lambda f: (f, 0))),
        scratch_shapes=[pltpu.VMEM((tm, D_MODEL), BF16)],
        compiler_params=_params("arbitrary"),
        name="ffn_first",
    )(h, g4, wg_all, wu_all, wd_all)
    return pl.pallas_call(
        _ffn_rest_kernel,
        out_shape=jax.ShapeDtypeStruct((N_TOK, D_MODEL), F32),
        grid=(N_TOK // tm - 1, FFN_DIM // tf),
        in_specs=[
            pl.BlockSpec((tm, D_MODEL), lambda i, f: (i + 1, 0)),
            pl.BlockSpec((None, None, 1, D_MODEL), lambda i, f: (layer, half, 0, 0)),
            pl.BlockSpec((D_MODEL, tf), lambda i, f: (0, f)),
            pl.BlockSpec((D_MODEL, tf), lambda i, f: (0, f)),
            pl.BlockSpec((tf, D_MODEL), lambda i, f: (f, 0)),
            pl.BlockSpec(memory_space=pl.ANY),
        ],
        out_specs=pl.BlockSpec((tm, D_MODEL), lambda i, f: (i + 1, 0)),
        scratch_shapes=[pltpu.VMEM((tm, D_MODEL), BF16)],
        input_output_aliases={5: 0},
        compiler_params=_params("parallel", "arbitrary"),
        name="ffn_rest",
    )(h, g4, wg_bf, wu_bf, wd_bf, partial)


def _inproj_kernel(h_ref, g_ref, w_ref, hg_ref, *rest, norm_tiles, tn, has_extra, first, n_alias=0):
    rest = list(rest)
    wx_ref = rest.pop(0) if has_extra else None
    del rest[:n_alias]
    o_ref = rest.pop(0)
    ox_ref = rest.pop(0) if has_extra else None
    wbf_ref = rest.pop(0) if first else None
    (xn_ref,) = rest
    j = pl.program_id(0 if first else 1)

    @pl.when(j == 0)
    def _():
        xn = _rms_rows(h_ref[...], g_ref[...]).astype(BF16)
        xn_ref[...] = xn
        if has_extra:
            ox_ref[...] = _dot(xn, wx_ref[...])

    if first:
        w = w_ref[...].astype(BF16)
        wbf_ref[...] = w
    else:
        w = w_ref[...]
    y = _dot(xn_ref[...], w)
    if norm_tiles == 0:
        o_ref[...] = y
        return

    @pl.when(j >= norm_tiles)
    def _():
        o_ref[...] = y

    @pl.when(j < norm_tiles)
    def _():
        for c in range(tn // 128):
            sl = slice(c * 128, (c + 1) * 128)
            o_ref[:, sl] = _rms_rows(y[:, sl], hg_ref[:, sl])


def inproj(h, g, w_all, layer, head_gain=None, *, n_cols=None, norm_cols=0, w_extra=None, tn=1024, tn_first=512):
    tm = TM_WIDE
    n = w_all.shape[2] if n_cols is None else n_cols
    if head_gain is None:
        head_gain = jnp.ones((1, n), F32)
    has_extra = w_extra is not None
    g2 = g.reshape(1, D_MODEL)
    nx = w_extra.shape[2] if has_extra else 0
    out_shapes = [jax.ShapeDtypeStruct((N_TOK, n), F32)]
    if has_extra:
        out_shapes.append(jax.ShapeDtypeStruct((N_TOK, nx), F32))

    partials = []
    if w_all.dtype == BF16:
        w_bf = w_all
        w_spec = pl.BlockSpec((None, D_MODEL, tn), lambda i, j: (layer, 0, j))
    else:
        kern = functools.partial(_inproj_kernel, norm_tiles=norm_cols // tn_first, tn=tn_first, has_extra=has_extra,
                                 first=True)
        in_specs = [
            pl.BlockSpec((tm, D_MODEL), lambda j: (0, 0)),
            pl.BlockSpec((1, D_MODEL), lambda j: (0, 0)),
            pl.BlockSpec((None, D_MODEL, tn_first), lambda j: (layer, 0, j)),
            pl.BlockSpec((1, tn_first), lambda j: (0, j)),
        ]
        args = [h, g2, w_all, head_gain]
        out_specs = [pl.BlockSpec((tm, tn_first), lambda j: (0, j))]
        if has_extra:
            in_specs.append(pl.BlockSpec((None, D_MODEL, nx), lambda j: (layer, 0, 0)))
            args.append(w_extra)
            out_specs.append(pl.BlockSpec((tm, nx), lambda j: (0, 0)))
        out_specs.append(pl.BlockSpec((D_MODEL, tn_first), lambda j: (0, j)))
        *partials, w_bf = pl.pallas_call(
            kern,
            out_shape=tuple(out_shapes) + (jax.ShapeDtypeStruct((D_MODEL, n), BF16),),
            grid=(n // tn_first,),
            in_specs=in_specs,
            out_specs=tuple(out_specs),
            scratch_shapes=[pltpu.VMEM((tm, D_MODEL), BF16)],
            compiler_params=_params("arbitrary"),
            name="inproj_first",
        )(*args)
        w_spec = pl.BlockSpec((D_MODEL, tn), lambda i, j: (0, j))

    r0 = len(partials) and 1
    kern = functools.partial(_inproj_kernel, norm_tiles=norm_cols // tn, tn=tn, has_extra=has_extra, first=False,
                             n_alias=len(partials))
    in_specs = [
        pl.BlockSpec((tm, D_MODEL), lambda i, j: (i + r0, 0)),
        pl.BlockSpec((1, D_MODEL), lambda i, j: (0, 0)),
        w_spec,
        pl.BlockSpec((1, tn), lambda i, j: (0, j)),
    ]
    args = [h, g2, w_bf, head_gain]
    out_specs = [pl.BlockSpec((tm, tn), lambda i, j: (i + r0, j))]
    if has_extra:
        in_specs.append(pl.BlockSpec((None, D_MODEL, nx), lambda i, j: (layer, 0, 0)))
        args.append(w_extra)
        out_specs.append(pl.BlockSpec((tm, nx), lambda i, j: (i + r0, 0)))
    n_in = len(in_specs)
    in_specs += [pl.BlockSpec(memory_space=pl.ANY)] * len(partials)
    args += partials
    outs = pl.pallas_call(
        kern,
        out_shape=tuple(out_shapes),
        grid=(N_TOK // tm - r0, n // tn),
        in_specs=in_specs,
        out_specs=tuple(out_specs),
        scratch_shapes=[pltpu.VMEM((tm, D_MODEL), BF16)],
        input_output_aliases={n_in + k: k for k in range(len(partials))},
        compiler_params=_params("parallel", "arbitrary"),
        name="inproj_main",
    )(*args)
    return outs if has_extra else outs[0]


def _outproj_kernel(a_ref, w_ref, h_ref, o_ref):
    o_ref[...] = h_ref[...] + _dot(a_ref[...], w_ref[...])


def _outproj_split_kernel(ap_ref, as_ref, w_ref, h_ref, o_ref):
    i = pl.program_id(1)

    @pl.when(i < N_PROMPT_TILES)
    def _():
        o_ref[...] = h_ref[...] + _dot(ap_ref[...], w_ref[...])

    @pl.when(i >= N_PROMPT_TILES)
    def _():
        o_ref[...] = h_ref[...] + _dot(as_ref[...], w_ref[...])


def outproj(h, a, w_all, layer, *, tn=1024):
    tm = TM_SPLIT
    k = w_all.shape[1]
    w_spec = pl.BlockSpec((None, k, tn), lambda j, i: (layer, 0, j))
    h_spec = pl.BlockSpec((tm, tn), lambda j, i: (i, j))
    if isinstance(a, tuple):
        kern = _outproj_split_kernel
        a_specs = [pl.BlockSpec((tm, k), lambda j, i: (jnp.minimum(i, N_PROMPT_TILES - 1), 0)),
                   pl.BlockSpec((tm, k), lambda j, i: (0, 0))]
        args = list(a)
    else:
        kern = _outproj_kernel
        a_specs = [pl.BlockSpec((tm, k), lambda j, i: (i, 0))]
        args = [a]
    return pl.pallas_call(
        kern,
        out_shape=jax.ShapeDtypeStruct((N_TOK, D_MODEL), F32),
        grid=(D_MODEL // tn, N_TOK // tm),
        in_specs=a_specs + [w_spec, h_spec],
        out_specs=h_spec,
        compiler_params=_params("parallel", "arbitrary"),
        name="outproj",
    )(*args, w_all, h)


def _ple_update(h_ref, p_ref, g_ref, wp_ref, wgt_ref):
    x = h_ref[...]
    xn = _rms_rows(x, g_ref[...]).astype(BF16)
    gate = jax.nn.sigmoid(_dot(xn, wgt_ref[...]))
    emb = _dot(p_ref[...].astype(BF16), wp_ref[...])
    return x + emb * gate


def _ple_kernel(h_ref, p_ref, g_ref, wp_ref, wgt_ref, o_ref):
    o_ref[...] = _ple_update(h_ref, p_ref, g_ref, wp_ref, wgt_ref)


def _ple_split_kernel(h_ref, p_ref, g_ref, wp_ref, wgt_ref, op_ref, os_ref):
    i = pl.program_id(0)

    @pl.when(i < N_PROMPT_TILES)
    def _():
        op_ref[...] = _ple_update(h_ref, p_ref, g_ref, wp_ref, wgt_ref)

    @pl.when(i >= N_PROMPT_TILES)
    def _():
        os_ref[...] = _ple_update(h_ref, p_ref, g_ref, wp_ref, wgt_ref)


def ple_add(h, p_all, g_all, w_ple_all, w_gate_all, layer, *, split=False):
    tm = TM_SPLIT if split else TM
    in_specs = [
        pl.BlockSpec((tm, D_MODEL), lambda i: (i, 0)),
        pl.BlockSpec((None, tm, PLE_DIM), lambda i: (layer, i, 0)),
        pl.BlockSpec((None, 1, D_MODEL), lambda i: (layer, 0, 0)),
        pl.BlockSpec((None, PLE_DIM, D_MODEL), lambda i: (layer, 0, 0)),
        pl.BlockSpec((None, D_MODEL, D_MODEL), lambda i: (layer, 0, 0)),
    ]
    if split:
        kern = _ple_split_kernel
        out_shape = (jax.ShapeDtypeStruct((N_PROMPT, D_MODEL), F32), jax.ShapeDtypeStruct((N_SAMPLE, D_MODEL), F32))
        out_specs = (pl.BlockSpec((tm, D_MODEL), lambda i: (jnp.minimum(i, N_PROMPT_TILES - 1), 0)),
                     pl.BlockSpec((tm, D_MODEL), lambda i: (0, 0)))
    else:
        kern = _ple_kernel
        out_shape = jax.ShapeDtypeStruct((N_TOK, D_MODEL), F32)
        out_specs = pl.BlockSpec((tm, D_MODEL), lambda i: (i, 0))
    return pl.pallas_call(
        kern,
        out_shape=out_shape,
        grid=(N_TOK // tm,),
        in_specs=in_specs,
        out_specs=out_specs,
        compiler_params=_params("arbitrary"),
        name="ple_add",
    )(h, p_all, g_all.reshape(DEPTH, 1, D_MODEL), w_ple_all, w_gate_all)


def _ssd_kernel(zx_ref, dtr_ref, st0_ref, cv0_ref, cw_ref, cb_ref, dtb_ref, alog_ref, rep_ref, dsk_ref, gn_ref,
                y_ref, st_ref, cvt_ref, s_scr, xpad, act, pcum, dts):
    s = pl.program_id(0)
    is_sample = s >= N_PROMPT_CHUNKS
    pos = s % CHUNKS_PER_SEQ
    first = jnp.logical_or(is_sample, pos == 0)
    last = jnp.logical_or(is_sample, pos == CHUNKS_PER_SEQ - 1)

    @pl.when(jnp.logical_and(first, jnp.logical_not(is_sample)))
    def _():
        s_scr[...] = jnp.zeros_like(s_scr)
        xpad[0:8, :] = jnp.zeros((8, SSD_CONV_DIM), F32)

    @pl.when(is_sample)
    def _():
        for g in range(SSD_GROUPS):
            s_scr[g] = st0_ref[0, g].T
        xpad[0:8, :] = jnp.zeros((8, SSD_CONV_DIM), F32)
        xpad[5:8, :] = cv0_ref[0]

    xpad[8:72, :] = zx_ref[:, SSD_INNER:SSD_MAIN_WIDTH]
    conv = cb_ref[...] + cw_ref[3:4, :] * xpad[8:72, :]
    conv = conv + cw_ref[2:3, :] * xpad[7:71, :]
    conv = conv + cw_ref[1:2, :] * xpad[6:70, :]
    conv = conv + cw_ref[0:1, :] * xpad[5:69, :]
    act[...] = conv * jax.nn.sigmoid(conv)
    xpad[0:8, :] = xpad[64:72, :]

    x = dtr_ref[...] + dtb_ref[...]
    dt = jnp.maximum(x, 0.0) + jnp.log1p(jnp.exp(-jnp.abs(x)))
    run = dt * (-jnp.exp(alog_ref[...]))
    row = lax.broadcasted_iota(jnp.int32, (CHUNK, 128), 0)
    for k in (1, 2, 4, 8, 16, 32):
        run = run + jnp.where(row >= k, pltpu.roll(run, k, 0), 0.0)
    both = jnp.concatenate([dt, run], axis=0)
    hi = both.astype(BF16)
    r1 = both - hi.astype(F32)
    mid = r1.astype(BF16)
    lo = (r1 - mid.astype(F32)).astype(BF16)
    wide = _dot(jnp.concatenate([hi, mid, lo], axis=0), rep_ref[...])
    wide = (wide[0:2 * CHUNK] + wide[2 * CHUNK:4 * CHUNK]) + wide[4 * CHUNK:6 * CHUNK]
    dts[...] = wide[0:CHUNK]
    pcum[...] = wide[CHUNK:2 * CHUNK]

    gw = SSD_GROUP_WIDTH
    row_g = lax.broadcasted_iota(jnp.int32, (CHUNK, gw), 0)
    lane_g = lax.broadcasted_iota(jnp.int32, (CHUNK, gw), 1) % CHUNK
    diag = row_g == lane_g
    causal = row_g >= lane_g
    r4 = lax.broadcasted_iota(jnp.int32, (256, 256), 0) // CHUNK
    c4 = lax.broadcasted_iota(jnp.int32, (256, 256), 1) // SSD_HEAD_DIM
    head_diag = r4 == c4

    for g in range(SSD_GROUPS):
        cs = slice(g * gw, (g + 1) * gw)
        p_g = pcum[:, cs]
        dt_g = dts[:, cs]
        x_g = act[:, cs]
        b_g = act[:, SSD_INNER + g * SSD_STATE:SSD_INNER + (g + 1) * SSD_STATE].astype(BF16)
        c_g = act[:, SSD_INNER + SSD_GROUPS * SSD_STATE + g * SSD_STATE:
                  SSD_INNER + SSD_GROUPS * SSD_STATE + (g + 1) * SSD_STATE].astype(BF16)
        p_s = jnp.sum(jnp.where(diag, p_g, 0.0), axis=0, keepdims=True)
        dt_s = jnp.sum(jnp.where(diag, dt_g, 0.0), axis=0, keepdims=True)
        p_last = p_g[CHUNK - 1:CHUNK, :]
        cb = _dot_nt(c_g, jnp.concatenate([b_g] * 8, axis=0))
        m = (cb * jnp.exp(jnp.where(causal, p_g - p_s, NEG)) * dt_s).astype(BF16)
        x_bf = x_g.astype(BF16)
        halves = []
        for hh in range(2):
            hs = slice(hh * 256, (hh + 1) * 256)
            xh = x_bf[:, hs]
            x_bd = jnp.where(head_diag, jnp.concatenate([xh] * 4, axis=0), jnp.zeros((), BF16))
            halves.append(_dot(m[:, hs], x_bd))
        y = jnp.concatenate(halves, axis=1)
        st = s_scr[g]
        y = y + _dot(c_g, st.astype(BF16)) * jnp.exp(p_g)
        wx = (jnp.exp(p_last - p_g) * dt_g * x_g).astype(BF16)
        s_scr[g] = st * jnp.exp(p_last) + _dot_tn(b_g, wx)
        y = y + dsk_ref[:, cs] * x_g
        z = zx_ref[:, cs]
        y = y * (z * jax.nn.sigmoid(z))
        y_ref[:, cs] = _rms_rows(y, gn_ref[:, cs]).astype(BF16)

    @pl.when(last)
    def _():
        cvt_ref[0] = xpad[0:8, :]
        for g in range(SSD_GROUPS):
            st_ref[0, g] = s_scr[g].T


def _ssd_seq(s):
    return jnp.where(s < N_PROMPT_CHUNKS, s // CHUNKS_PER_SEQ, BATCH + s - N_PROMPT_CHUNKS)


def ssd_core(zx, dt_raw, state_all, conv_all, conv_w_all, layer, conv_b, dt_bias, a_log, d_skip, g_norm):
    n_seq = BATCH + DEC_BATCH
    pad_heads = lambda v: jnp.pad(v, (0, 128 - SSD_HEADS)).reshape(1, 128)
    samp = lambda s: jnp.maximum(s - N_PROMPT_CHUNKS, 0)
    row_spec = lambda w: pl.BlockSpec((1, w), lambda s: (0, 0))
    st_all = state_all.reshape(-1, DEC_BATCH, SSD_GROUPS, SSD_GROUP_WIDTH, SSD_STATE)
    head_of_channel = jnp.arange(SSD_INNER, dtype=jnp.int32) // SSD_HEAD_DIM
    rep = (jnp.arange(128, dtype=jnp.int32)[:, None] == head_of_channel[None, :]).astype(BF16)
    y, st, cvt = pl.pallas_call(
        _ssd_kernel,
        out_shape=(jax.ShapeDtypeStruct((N_TOK, SSD_INNER), BF16),
                   jax.ShapeDtypeStruct((n_seq, SSD_GROUPS, SSD_GROUP_WIDTH, SSD_STATE), F32),
                   jax.ShapeDtypeStruct((n_seq, 8, SSD_CONV_DIM), F32)),
        grid=(N_CHUNKS,),
        in_specs=[
            pl.BlockSpec((CHUNK, SSD_MAIN_WIDTH), lambda s: (s, 0)),
            pl.BlockSpec((CHUNK, 128), lambda s: (s, 0)),
            pl.BlockSpec((None, 1, SSD_GROUPS, SSD_GROUP_WIDTH, SSD_STATE), lambda s: (layer, samp(s), 0, 0, 0)),
            pl.BlockSpec((None, 1, 3, SSD_CONV_DIM), lambda s: (layer, samp(s), 0, 0)),
            pl.BlockSpec((None, 4, SSD_CONV_DIM), lambda s: (layer, 0, 0)),
            row_spec(SSD_CONV_DIM), row_spec(128), row_spec(128),
            pl.BlockSpec((128, SSD_INNER), lambda s: (0, 0)),
            row_spec(SSD_INNER), row_spec(SSD_INNER),
        ],
        out_specs=(pl.BlockSpec((CHUNK, SSD_INNER), lambda s: (s, 0)),
                   pl.BlockSpec((1, SSD_GROUPS, SSD_GROUP_WIDTH, SSD_STATE), lambda s: (_ssd_seq(s), 0, 0, 0)),
                   pl.BlockSpec((1, 8, SSD_CONV_DIM), lambda s: (_ssd_seq(s), 0, 0))),
        scratch_shapes=[
            pltpu.VMEM((SSD_GROUPS, SSD_STATE, SSD_GROUP_WIDTH), F32),
            pltpu.VMEM((CHUNK + 8, SSD_CONV_DIM), F32),
            pltpu.VMEM((CHUNK, SSD_CONV_DIM), F32),
            pltpu.VMEM((CHUNK, SSD_INNER), F32),
            pltpu.VMEM((CHUNK, SSD_INNER), F32),
        ],
        compiler_params=_params("arbitrary"),
        name="ssd_core",
    )(zx, dt_raw, st_all, conv_all, conv_w_all, conv_b.reshape(1, SSD_CONV_DIM), pad_heads(dt_bias),
      pad_heads(a_log), rep, jnp.repeat(d_skip, SSD_HEAD_DIM).reshape(1, SSD_INNER), g_norm.reshape(1, SSD_INNER))
    return y, st.reshape(n_seq, SSD_HEADS, SSD_HEAD_DIM, SSD_STATE), cvt[:, 5:8]


DIFF_SCALE = DIFF_HEAD_DIM ** -0.5
LOG2E = math.log2(math.e)


def _diff_lambda(lp_ref, lam_init):
    lp = lp_ref[...]
    a = jnp.sum(lp[0:1] * lp[1:2], axis=-1, keepdims=True)
    b = jnp.sum(lp[2:3] * lp[3:4], axis=-1, keepdims=True)
    return jnp.exp(a) - jnp.exp(b) + lam_init


def _alibi_slope(h):
    return jnp.exp2(-(jnp.zeros((1, 1), F32) + (h + 1).astype(F32)))


def _diff_prompt_kernel(q_ref, k_ref, v_ref, lp_ref, gs_ref, o_ref, *, lam_init, tq):
    h = pl.program_id(1)
    lam = _diff_lambda(lp_ref, lam_init)
    slope2 = _alibi_slope(h) * LOG2E
    c = DIFF_SCALE * LOG2E
    d = DIFF_HEAD_DIM
    t_i = lax.broadcasted_iota(jnp.int32, (tq, tq), 0)
    j_i = lax.broadcasted_iota(jnp.int32, (tq, tq), 1)
    bias_own = jnp.where(j_i // CHUNK <= t_i // CHUNK,
                         slope2 * (t_i - jnp.abs(t_i - j_i)).astype(F32), NEG)
    for qb in range(SEQ // tq):
        q0 = qb * tq
        rows = slice(q0, q0 + tq)
        if qb:
            bias_past = slope2 * (lax.broadcasted_iota(jnp.int32, (1, q0), 1) - q0).astype(F32)
        weights = []
        for m in range(2):
            cols = slice(m * d, (m + 1) * d)
            q = q_ref[rows, cols].astype(BF16)
            s_own = _dot_nt(q, k_ref[rows, cols].astype(BF16)) * c + bias_own
            mx = jnp.max(s_own, axis=-1, keepdims=True)
            if qb:
                s_past = _dot_nt(q, k_ref[0:q0, cols].astype(BF16)) * c + bias_past
                mx = jnp.maximum(mx, jnp.max(s_past, axis=-1, keepdims=True))
                e_past = jnp.exp2(s_past - mx)
            e_own = jnp.exp2(s_own - mx)
            den = jnp.sum(e_own, axis=-1, keepdims=True)
            if qb:
                den = den + jnp.sum(e_past, axis=-1, keepdims=True)
            weights.append((e_own, e_past if qb else None, 1.0 / den))
        (eo0, ep0, r0), (eo1, ep1, r1) = weights
        r1 = lam * r1
        o = _dot((eo0 * r0 - eo1 * r1).astype(BF16), v_ref[rows, :].astype(BF16))
        if qb:
            o = o + _dot((ep0 * r0 - ep1 * r1).astype(BF16), v_ref[0:q0, :].astype(BF16))
        o_ref[rows, :] = (_rms_rows(o, gs_ref[...]) * (1.0 - lam_init)).astype(BF16)


def diff_attn_prompt(qkv, lam_p, g_sub, lam_init, *, tq=256):
    w = 2 * DIFF_HEAD_DIM
    kern = functools.partial(_diff_prompt_kernel, lam_init=lam_init, tq=tq)
    return pl.pallas_call(
        kern,
        out_shape=jax.ShapeDtypeStruct((N_PROMPT, D_MODEL), BF16),
        grid=(BATCH, DIFF_HEADS),
        in_specs=[
            pl.BlockSpec((SEQ, w), lambda b, h: (b, h)),
            pl.BlockSpec((SEQ, w), lambda b, h: (b, DIFF_HEADS + h)),
            pl.BlockSpec((SEQ, w), lambda b, h: (b, 2 * DIFF_HEADS + h)),
            pl.BlockSpec((4, DIFF_HEAD_DIM), lambda b, h: (0, 0)),
            pl.BlockSpec((1, w), lambda b, h: (0, 0)),
        ],
        out_specs=pl.BlockSpec((SEQ, w), lambda b, h: (b, h)),
        compiler_params=_params("parallel", "parallel"),
        name="diff_attn_prompt",
    )(qkv, qkv, qkv, lam_p, g_sub.reshape(1, w))


def _diff_sample_kernel(q_ref, kn_ref, vn_ref, kc0_ref, kc1_ref, vc0_ref, vc1_ref, lp_ref, gs_ref, o_ref, vh_ref,
                        *, lam_init):
    kc_refs = (kc0_ref, kc1_ref)
    h = pl.program_id(1)
    lam = _diff_lambda(lp_ref, lam_init)
    slope = _alibi_slope(h)
    d = DIFF_HEAD_DIM
    t = DEC_SEQ
    tq_p = lax.broadcasted_iota(jnp.int32, (t, PAST_LEN), 0)
    kp_p = lax.broadcasted_iota(jnp.int32, (t, PAST_LEN), 1)
    bias_p = -slope * (PAST_LEN + tq_p - kp_p).astype(F32)
    tq_n = lax.broadcasted_iota(jnp.int32, (t, t), 0)
    kp_n = lax.broadcasted_iota(jnp.int32, (t, t), 1)
    bias_n = -slope * jnp.abs(tq_n - kp_n).astype(F32)
    pp, pn = [], []
    for m in range(2):
        cols = slice(m * d, (m + 1) * d)
        q = q_ref[:, cols].astype(BF16)
        kc = kc_refs[m].reshape(PAST_LEN, d)[...]
        sp = _dot_nt(q, kc.astype(BF16)) * DIFF_SCALE + bias_p
        sn = _dot_nt(q, kn_ref[:, cols].astype(BF16)) * DIFF_SCALE + bias_n
        mx = jnp.maximum(jnp.max(sp, axis=-1, keepdims=True), jnp.max(sn, axis=-1, keepdims=True))
        ep = jnp.exp(sp - mx)
        en = jnp.exp(sn - mx)
        den = jnp.sum(ep, axis=-1, keepdims=True) + jnp.sum(en, axis=-1, keepdims=True)
        pp.append(ep / den)
        pn.append(en / den)
    ap = (pp[0] - lam * pp[1]).astype(BF16)
    an = (pn[0] - lam * pn[1]).astype(BF16)
    for k in range(DIFF_HEADS):
        @pl.when(h == k)
        def _(k=k):
            for half, vc_ref in enumerate((vc0_ref, vc1_ref)):
                rows = vc_ref.reshape(PAST_LEN * DIFF_HEADS, d)
                vh_ref[:, half * d:(half + 1) * d] = rows[pl.ds(k, PAST_LEN, stride=DIFF_HEADS), :].astype(BF16)

    o = _dot(ap, vh_ref[...]) + _dot(an, vn_ref[...].astype(BF16))
    o_ref[...] = (_rms_rows(o, gs_ref[...]) * (1.0 - lam_init)).astype(BF16)


def diff_attn_sample(qkv, cache_k, cache_v, lam_p, g_sub, lam_init):
    w = 2 * DIFF_HEAD_DIM
    r0 = N_PROMPT // DEC_SEQ
    kern = functools.partial(_diff_sample_kernel, lam_init=lam_init)
    k_rows = cache_k.reshape(DEC_BATCH, PAST_LEN, 2 * DIFF_HEADS, 1, DIFF_HEAD_DIM)
    k_spec = lambda m: pl.BlockSpec((None, PAST_LEN, None, 1, DIFF_HEAD_DIM), lambda b, h: (b, 0, 2 * h + m, 0, 0))
    v_spec = lambda half: pl.BlockSpec((None, PAST_LEN, DIFF_HEADS, DIFF_HEAD_DIM), lambda b, h: (b, 0, 0, half))
    return pl.pallas_call(
        kern,
        out_shape=jax.ShapeDtypeStruct((N_SAMPLE, D_MODEL), BF16),
        grid=(DEC_BATCH, DIFF_HEADS),
        in_specs=[
            pl.BlockSpec((DEC_SEQ, w), lambda b, h: (r0 + b, h)),
            pl.BlockSpec((DEC_SEQ, w), lambda b, h: (r0 + b, DIFF_HEADS + h)),
            pl.BlockSpec((DEC_SEQ, w), lambda b, h: (r0 + b, 2 * DIFF_HEADS + h)),
            k_spec(0),
            k_spec(1),
            v_spec(0),
            v_spec(1),
            pl.BlockSpec((4, DIFF_HEAD_DIM), lambda b, h: (0, 0)),
            pl.BlockSpec((1, w), lambda b, h: (0, 0)),
        ],
        out_specs=pl.BlockSpec((DEC_SEQ, w), lambda b, h: (b, h)),
        scratch_shapes=[pltpu.VMEM((PAST_LEN, w), BF16)],
        compiler_params=_params("parallel", "arbitrary"),
        name="diff_attn_sample",
    )(qkv, qkv, qkv, k_rows, k_rows, cache_v, cache_v, lam_p, g_sub.reshape(1, w))


BAND_SCALE = BAND_HEAD_DIM ** -0.5
BAND_TQ = 256
BAND_E = 1024


def _band_bias(e_ref, tq):
    eb = jnp.broadcast_to(e_ref[0], (tq, BAND_E))
    return pltpu.roll(eb, BAND_E - (BAND_TQ - 1), 1, stride=1, stride_axis=0)


def _band_prompt_kernel(q_ref, k_ref, v_ref, e_ref, o_ref):
    tq = BAND_TQ
    win = BAND_WINDOW + tq
    bias = _band_bias(e_ref, tq)[:, :win]
    qc = lax.broadcasted_iota(jnp.int32, (tq, win), 0) // CHUNK
    jc = lax.broadcasted_iota(jnp.int32, (tq, win), 1) // CHUNK
    bias = jnp.where(jnp.logical_and(jc >= qc, jc <= qc + BAND_WINDOW // CHUNK), bias, NEG)
    for qb in range(SEQ // tq):
        q0 = qb * tq
        k0 = max(0, q0 - BAND_WINDOW)
        nk = q0 + tq - k0
        rows = slice(q0, q0 + tq)
        sc = _dot_nt(q_ref[rows, :].astype(BF16), k_ref[k0:k0 + nk, :].astype(BF16)) * BAND_SCALE
        sc = sc + bias[:, win - nk:]
        e = jnp.exp(sc - jnp.max(sc, axis=-1, keepdims=True))
        p = (e / jnp.sum(e, axis=-1, keepdims=True)).astype(BF16)
        o_ref[rows, :] = _dot(p, v_ref[k0:k0 + nk, :].astype(BF16)).astype(BF16)


def _band_e_rows(table):
    edge = jnp.broadcast_to(table[:, 2 * REL_CLIP:], (BAND_HEADS, BAND_E - (2 * REL_CLIP + 1)))
    return jnp.concatenate([edge, table[:, ::-1]], axis=1).reshape(BAND_HEADS, 1, BAND_E)


def band_attn_prompt(qkv, table):
    d = BAND_HEAD_DIM
    return pl.pallas_call(
        _band_prompt_kernel,
        out_shape=jax.ShapeDtypeStruct((N_PROMPT, D_MODEL), BF16),
        grid=(BATCH, BAND_HEADS),
        in_specs=[
            pl.BlockSpec((SEQ, d), lambda b, h: (b, h)),
            pl.BlockSpec((SEQ, d), lambda b, h: (b, BAND_HEADS + h)),
            pl.BlockSpec((SEQ, d), lambda b, h: (b, 2 * BAND_HEADS + h)),
            pl.BlockSpec((1, 1, BAND_E), lambda b, h: (h, 0, 0)),
        ],
        out_specs=pl.BlockSpec((SEQ, d), lambda b, h: (b, h)),
        compiler_params=_params("parallel", "parallel"),
        name="band_attn_prompt",
    )(qkv, qkv, qkv, _band_e_rows(table))


def _band_sample_kernel(q_ref, kn_ref, vn_ref, kc_ref, vc_ref, e_ref, o_ref):
    t = DEC_SEQ
    bias = _band_bias(e_ref, t)
    q = q_ref[...].astype(BF16)
    kc = kc_ref.reshape(BAND_WINDOW, BAND_HEAD_DIM)[...]
    vc = vc_ref.reshape(BAND_WINDOW, BAND_HEAD_DIM)[...]
    sp = _dot_nt(q, kc.astype(BF16)) * BAND_SCALE + bias[:, :BAND_WINDOW]
    sn = _dot_nt(q, kn_ref[...].astype(BF16)) * BAND_SCALE + bias[:, BAND_WINDOW:BAND_WINDOW + t]
    mx = jnp.maximum(jnp.max(sp, axis=-1, keepdims=True), jnp.max(sn, axis=-1, keepdims=True))
    ep = jnp.exp(sp - mx)
    en = jnp.exp(sn - mx)
    den = jnp.sum(ep, axis=-1, keepdims=True) + jnp.sum(en, axis=-1, keepdims=True)
    o = (_dot((ep / den).astype(BF16), vc.astype(BF16))
         + _dot((en / den).astype(BF16), vn_ref[...].astype(BF16)))
    o_ref[...] = o.astype(BF16)


def band_attn_sample(qkv, cache_k, cache_v, table):
    d = BAND_HEAD_DIM
    r0 = N_PROMPT // DEC_SEQ
    rows_view = (DEC_BATCH, BAND_WINDOW, BAND_HEADS, 1, d)
    cache_spec = pl.BlockSpec((None, BAND_WINDOW, None, 1, d), lambda b, h: (b, 0, h, 0, 0))
    return pl.pallas_call(
        _band_sample_kernel,
        out_shape=jax.ShapeDtypeStruct((N_SAMPLE, D_MODEL), BF16),
        grid=(DEC_BATCH, BAND_HEADS),
        in_specs=[
            pl.BlockSpec((DEC_SEQ, d), lambda b, h: (r0 + b, h)),
            pl.BlockSpec((DEC_SEQ, d), lambda b, h: (r0 + b, BAND_HEADS + h)),
            pl.BlockSpec((DEC_SEQ, d), lambda b, h: (r0 + b, 2 * BAND_HEADS + h)),
            cache_spec,
            cache_spec,
            pl.BlockSpec((1, 1, BAND_E), lambda b, h: (h, 0, 0)),
        ],
        out_specs=pl.BlockSpec((DEC_SEQ, d), lambda b, h: (b, h)),
        compiler_params=_params("parallel", "parallel"),
        name="band_attn_sample",
    )(qkv, qkv, qkv, cache_k.reshape(rows_view), cache_v.reshape(rows_view), _band_e_rows(table))


def kernel(x_prompt, x_sample, p_prompt, p_sample, state_ssm, state_conv, cache_k_diff, cache_v_diff,
           cache_k_band, cache_v_band, g_ffn, w_ffn_gate, w_ffn_up, w_ffn_down, g_mix,
           ssd_w_in, ssd_conv_w, ssd_conv_b, ssd_dt_bias, ssd_a_log, ssd_d, ssd_g_norm, ssd_w_out,
           diff_w_in, diff_g_q, diff_g_k, diff_lambda, diff_g_sub, diff_w_out,
           band_w_in, band_g_q, band_g_k, band_rel_bias, band_w_out,
           g_ple, w_ple, w_ple_gate):
    h = jnp.concatenate([x_prompt.reshape(N_PROMPT, D_MODEL), x_sample.reshape(N_SAMPLE, D_MODEL)], axis=0)
    p_all = jnp.concatenate([p_prompt.reshape(DEPTH, N_PROMPT, PLE_DIM),
                             p_sample.reshape(DEPTH, N_SAMPLE, PLE_DIM)], axis=1)
    bf = lambda w: w.astype(BF16)
    wg_all, wu_all, wd_all = w_ffn_gate, w_ffn_up, w_ffn_down
    w_ple_all, w_gate_all = bf(w_ple), bf(w_ple_gate)
    ssd_w_main = bf(ssd_w_in)
    ssd_w_dt = bf(jnp.pad(ssd_w_in[:, :, SSD_MAIN_WIDTH:], ((0, 0), (0, 0), (0, 128 - SSD_HEADS))))
    ssd_w_out_all, diff_w_in_all, diff_w_out_all = bf(ssd_w_out), diff_w_in, bf(diff_w_out)
    band_w_in_all, band_w_out_all = band_w_in, bf(band_w_out)

    def prompt_rows(a):
        return a[:N_PROMPT].reshape(BATCH, SEQ, -1)

    def sample_rows(a):
        return a[N_PROMPT:].reshape(DEC_BATCH, DEC_SEQ, -1)

    ssm_p, conv_p, ssm_s, conv_s = [], [], [], []
    kd_p = vd_p = kb_p = vb_p = kd_s = vd_s = kb_s = vb_s = None
    for i in range(DEPTH):
        kind, j = i % 3, i // 3
        h = ffn_half(h, g_ffn, wg_all, wu_all, wd_all, i, 0)
        if kind == 0:
            zx, dt_raw = inproj(h, g_mix[i], ssd_w_main, j, n_cols=SSD_MAIN_WIDTH, w_extra=ssd_w_dt)
            y, st, cvt = ssd_core(zx, dt_raw, state_ssm, state_conv, ssd_conv_w, j, ssd_conv_b[j], ssd_dt_bias[j],
                                  ssd_a_log[j], ssd_d[j], ssd_g_norm[j])
            h = outproj(h, y, ssd_w_out_all, j)
            ssm_p.append(st[:BATCH])
            ssm_s.append(st[BATCH:])
            conv_p.append(cvt[:BATCH])
            conv_s.append(cvt[BATCH:])
        elif kind == 1:
            lam_init = 0.8 - 0.6 * math.exp(-0.3 * i)
            gains = jnp.concatenate([jnp.tile(diff_g_q[j].reshape(-1), DIFF_HEADS),
                                     jnp.tile(diff_g_k[j].reshape(-1), DIFF_HEADS),
                                     jnp.ones((D_MODEL,), F32)]).reshape(1, 3 * D_MODEL)
            qkv = inproj(h, g_mix[i], diff_w_in_all, j, gains, norm_cols=2 * D_MODEL)
            o_p = diff_attn_prompt(qkv, diff_lambda[j], diff_g_sub[j], lam_init)
            o_s = diff_attn_sample(qkv, cache_k_diff[j], cache_v_diff[j], diff_lambda[j], diff_g_sub[j], lam_init)
            h = outproj(h, (o_p, o_s), diff_w_out_all, j)
            k_new, v_new = qkv[:, D_MODEL:2 * D_MODEL], qkv[:, 2 * D_MODEL:]
            kd_p = prompt_rows(k_new).reshape(1, BATCH, SEQ, DIFF_HEADS, 2, DIFF_HEAD_DIM)
            vd_p = prompt_rows(v_new).reshape(1, BATCH, SEQ, DIFF_HEADS, 2 * DIFF_HEAD_DIM)
            kd_s = sample_rows(k_new).reshape(1, DEC_BATCH, DEC_SEQ, DIFF_HEADS, 2, DIFF_HEAD_DIM)
            vd_s = sample_rows(v_new).reshape(1, DEC_BATCH, DEC_SEQ, DIFF_HEADS, 2 * DIFF_HEAD_DIM)
        else:
            gains = jnp.concatenate([jnp.tile(band_g_q[j], BAND_HEADS), jnp.tile(band_g_k[j], BAND_HEADS),
                                     jnp.ones((D_MODEL,), F32)]).reshape(1, 3 * D_MODEL)
            qkv = inproj(h, g_mix[i], band_w_in_all, j, gains, norm_cols=2 * D_MODEL)
            o_p = band_attn_prompt(qkv, band_rel_bias[j])
            o_s = band_attn_sample(qkv, cache_k_band[j], cache_v_band[j], band_rel_bias[j])
            h = outproj(h, (o_p, o_s), band_w_out_all, j)
            k_new, v_new = qkv[:, D_MODEL:2 * D_MODEL], qkv[:, 2 * D_MODEL:]
            kb_p = prompt_rows(k_new)[:, SEQ - BAND_WINDOW:].reshape(1, BATCH, BAND_WINDOW, BAND_HEADS, BAND_HEAD_DIM)
            vb_p = prompt_rows(v_new)[:, SEQ - BAND_WINDOW:].reshape(1, BATCH, BAND_WINDOW, BAND_HEADS, BAND_HEAD_DIM)
            kb_s = sample_rows(k_new).reshape(1, DEC_BATCH, DEC_SEQ, BAND_HEADS, BAND_HEAD_DIM)
            vb_s = sample_rows(v_new).reshape(1, DEC_BATCH, DEC_SEQ, BAND_HEADS, BAND_HEAD_DIM)
        h = ffn_half(h, g_ffn, wg_all, wu_all, wd_all, i, 1)
        h = ple_add(h, p_all, g_ple, w_ple_all, w_gate_all, i, split=(i == DEPTH - 1))
    h_p, h_s = h
    return (h_p.reshape(BATCH, SEQ, D_MODEL), h_s.reshape(DEC_BATCH, DEC_SEQ, D_MODEL),
            jnp.stack(ssm_p), jnp.stack(conv_p), kd_p, vd_p, kb_p, vb_p,
            jnp.stack(ssm_s), jnp.stack(conv_s), kd_s, vd_s, kb_s, vb_s)
```

```python
import functools
import math

import jax
import jax.numpy as jnp
from jax import lax
from jax.experimental import pallas as pl
from jax.experimental.pallas import tpu as pltpu

F32 = jnp.float32
BF16 = jnp.bfloat16

D_MODEL = 2048
BATCH = 4
SEQ = 2048
DEPTH = 4
DEC_BATCH = 8
DEC_SEQ = 64
PAST_LEN = 2048
CHUNK = 64
NORM_EPS = 1e-6
FFN_DIM = 5632
PLE_DIM = 256
SSD_INNER = 4096
SSD_HEADS = 64
SSD_HEAD_DIM = 64
SSD_GROUPS = 8
SSD_STATE = 128
SSD_GROUP_WIDTH = SSD_INNER // SSD_GROUPS
SSD_CONV_DIM = SSD_INNER + 2 * SSD_GROUPS * SSD_STATE
SSD_MAIN_WIDTH = SSD_INNER + SSD_CONV_DIM
SSD_EXT_WIDTH = SSD_MAIN_WIDTH + SSD_INNER
DIFF_HEADS = 8
DIFF_HEAD_DIM = 128
BAND_HEADS = 16
BAND_HEAD_DIM = 128
BAND_WINDOW = 512
REL_CLIP = 256

N_PROMPT = BATCH * SEQ
N_SAMPLE = DEC_BATCH * DEC_SEQ
N_TOK = N_PROMPT + N_SAMPLE
CHUNKS_PER_SEQ = SEQ // CHUNK
N_PROMPT_CHUNKS = N_PROMPT // CHUNK
N_CHUNKS = N_TOK // CHUNK

TM_WIDE = 1088
TM = 544
TM_SPLIT = 512
N_PROMPT_TILES = N_PROMPT // TM_SPLIT
VMEM_LIMIT = 56 * 1024 * 1024
NEG = -1e30


def _params(*sem, vmem=VMEM_LIMIT):
    return pltpu.CompilerParams(dimension_semantics=sem, vmem_limit_bytes=vmem)


def _rms_rows(x, g):
    ms = jnp.mean(x * x, axis=-1, keepdims=True)
    return x * lax.rsqrt(ms + NORM_EPS) * g


def _dot(a, b):
    return jnp.dot(a, b, preferred_element_type=F32)


def _dot_nt(a, b):
    return lax.dot_general(a, b, (((1,), (1,)), ((), ())), preferred_element_type=F32)


def _dot_tn(a, b):
    return lax.dot_general(a, b, (((0,), (0,)), ((), ())), preferred_element_type=F32)


def _ffn_step(first, h_ref, g_ref, wg_ref, wu_ref, wd_ref, o_ref, xn_ref):
    @pl.when(first)
    def _():
        x = h_ref[...]
        xn_ref[...] = _rms_rows(x, g_ref[...]).astype(BF16)
        o_ref[...] = x

    xn = xn_ref[...]
    gate = _dot(xn, wg_ref[...])
    up = _dot(xn, wu_ref[...])
    act = (0.5 * (gate * jax.nn.sigmoid(gate)) * up).astype(BF16)
    o_ref[...] += _dot(act, wd_ref[...])


def _ffn_first_kernel(h_ref, g_ref, wg_ref, wu_ref, wd_ref, o_ref, wg_bf_ref, wu_bf_ref, wd_bf_ref, xn_ref):
    wg_bf_ref[...] = wg_ref[...].astype(BF16)
    wu_bf_ref[...] = wu_ref[...].astype(BF16)
    wd_bf_ref[...] = wd_ref[...].astype(BF16)
    _ffn_step(pl.program_id(0) == 0, h_ref, g_ref, wg_bf_ref, wu_bf_ref, wd_bf_ref, o_ref, xn_ref)


def _ffn_rest_kernel(h_ref, g_ref, wg_ref, wu_ref, wd_ref, partial_ref, o_ref, xn_ref):
    del partial_ref
    _ffn_step(pl.program_id(1) == 0, h_ref, g_ref, wg_ref, wu_ref, wd_ref, o_ref, xn_ref)


def ffn_half(h, g_all, wg_all, wu_all, wd_all, layer, half, *, tf=512, tf_first=256):
    tm = TM
    tm_first = TM_WIDE
    skip = tm_first // tm
    g4 = g_all.reshape(DEPTH, 2, 1, D_MODEL)
    g_spec1 = pl.BlockSpec((None, None, 1, D_MODEL), lambda f: (layer, half, 0, 0))
    partial, wg_bf, wu_bf, wd_bf = pl.pallas_call(
        _ffn_first_kernel,
        out_shape=(jax.ShapeDtypeStruct((N_TOK, D_MODEL), F32),
                   jax.ShapeDtypeStruct((D_MODEL, FFN_DIM), BF16),
                   jax.ShapeDtypeStruct((D_MODEL, FFN_DIM), BF16),
                   jax.ShapeDtypeStruct((FFN_DIM, D_MODEL), BF16)),
        grid=(FFN_DIM // tf_first,),
        in_specs=[
            pl.BlockSpec((tm_first, D_MODEL), lambda f: (0, 0), pipeline_mode=pl.Buffered(1)),
            g_spec1,
            pl.BlockSpec((None, None, D_MODEL, tf_first), lambda f: (layer, half, 0, f)),
            pl.BlockSpec((None, None, D_MODEL, tf_first), lambda f: (layer, half, 0, f)),
            pl.BlockSpec((None, None, tf_first, D_MODEL), lambda f: (layer, half, f, 0)),
        ],
        out_specs=(pl.BlockSpec((tm_first, D_MODEL), lambda f: (0, 0)),
                   pl.BlockSpec((D_MODEL, tf_first), lambda f: (0, f)),
                   pl.BlockSpec((D_MODEL, tf_first), lambda f: (0, f)),
                   pl.BlockSpec((tf_first, D_MODEL), lambda f: (f, 0))),
        scratch_shapes=[pltpu.VMEM((tm_first, D_MODEL), BF16)],
        compiler_params=_params("arbitrary"),
        name="ffn_first",
    )(h, g4, wg_all, wu_all, wd_all)
    return pl.pallas_call(
        _ffn_rest_kernel,
        out_shape=jax.ShapeDtypeStruct((N_TOK, D_MODEL), F32),
        grid=(N_TOK // tm - skip, FFN_DIM // tf),
        in_specs=[
            pl.BlockSpec((tm, D_MODEL), lambda i, f: (i + skip, 0)),
            pl.BlockSpec((None, None, 1, D_MODEL), lambda i, f: (layer, half, 0, 0)),
            pl.BlockSpec((D_MODEL, tf), lambda i, f: (0, f)),
            pl.BlockSpec((D_MODEL, tf), lambda i, f: (0, f)),
            pl.BlockSpec((tf, D_MODEL), lambda i, f: (f, 0)),
            pl.BlockSpec(memory_space=pl.ANY),
        ],
        out_specs=pl.BlockSpec((tm, D_MODEL), lambda i, f: (i + skip, 0)),
        scratch_shapes=[pltpu.VMEM((tm, D_MODEL), BF16)],
        input_output_aliases={5: 0},
        compiler_params=_params("parallel", "arbitrary"),
        name="ffn_rest",
    )(h, g4, wg_bf, wu_bf, wd_bf, partial)


def _inproj_kernel(h_ref, g_ref, w_ref, hg_ref, *rest, norm_tiles, tn, has_extra, first, n_alias=0):
    rest = list(rest)
    wx_ref = rest.pop(0) if has_extra else None
    del rest[:n_alias]
    o_ref = rest.pop(0)
    ox_ref = rest.pop(0) if has_extra else None
    wbf_ref = rest.pop(0) if first else None
    (xn_ref,) = rest
    j = pl.program_id(0 if first else 1)

    @pl.when(j == 0)
    def _():
        xn = _rms_rows(h_ref[...], g_ref[...]).astype(BF16)
        xn_ref[...] = xn
        if has_extra:
            ox_ref[...] = _dot(xn, wx_ref[...])

    if first:
        w = w_ref[...].astype(BF16)
        wbf_ref[...] = w
    else:
        w = w_ref[...]
    y = _dot(xn_ref[...], w)
    if norm_tiles == 0:
        o_ref[...] = y
        return

    @pl.when(j >= norm_tiles)
    def _():
        o_ref[...] = y

    @pl.when(j < norm_tiles)
    def _():
        for c in range(tn // 128):
            sl = slice(c * 128, (c + 1) * 128)
            o_ref[:, sl] = _rms_rows(y[:, sl], hg_ref[:, sl])


def inproj(h, g, w_all, layer, head_gain=None, *, n_cols=None, norm_cols=0, w_extra=None, tn=1024, tn_first=512):
    tm = TM_WIDE
    n = w_all.shape[2] if n_cols is None else n_cols
    if head_gain is None:
        head_gain = jnp.ones((1, n), F32)
    has_extra = w_extra is not None
    g2 = g.reshape(1, D_MODEL)
    nx = w_extra.shape[2] if has_extra else 0
    out_shapes = [jax.ShapeDtypeStruct((N_TOK, n), F32)]
    if has_extra:
        out_shapes.append(jax.ShapeDtypeStruct((N_TOK, nx), F32))

    partials = []
    if w_all.dtype == BF16:
        w_bf = w_all
        w_spec = pl.BlockSpec((None, D_MODEL, tn), lambda i, j: (layer, 0, j))
    else:
        kern = functools.partial(_inproj_kernel, norm_tiles=norm_cols // tn_first, tn=tn_first, has_extra=has_extra,
                                 first=True)
        in_specs = [
            pl.BlockSpec((tm, D_MODEL), lambda j: (0, 0)),
            pl.BlockSpec((1, D_MODEL), lambda j: (0, 0)),
            pl.BlockSpec((None, D_MODEL, tn_first), lambda j: (layer, 0, j)),
            pl.BlockSpec((1, tn_first), lambda j: (0, j)),
        ]
        args = [h, g2, w_all, head_gain]
        out_specs = [pl.BlockSpec((tm, tn_first), lambda j: (0, j))]
        if has_extra:
            in_specs.append(pl.BlockSpec((None, D_MODEL, nx), lambda j: (layer, 0, 0)))
            args.append(w_extra)
            out_specs.append(pl.BlockSpec((tm, nx), lambda j: (0, 0)))
        out_specs.append(pl.BlockSpec((D_MODEL, tn_first), lambda j: (0, j)))
        *partials, w_bf = pl.pallas_call(
            kern,
            out_shape=tuple(out_shapes) + (jax.ShapeDtypeStruct((D_MODEL, n), BF16),),
            grid=(n // tn_first,),
            in_specs=in_specs,
            out_specs=tuple(out_specs),
            scratch_shapes=[pltpu.VMEM((tm, D_MODEL), BF16)],
            compiler_params=_params("arbitrary"),
            name="inproj_first",
        )(*args)
        w_spec = pl.BlockSpec((D_MODEL, tn), lambda i, j: (0, j))

    r0 = len(partials) and 1
    kern = functools.partial(_inproj_kernel, norm_tiles=norm_cols // tn, tn=tn, has_extra=has_extra, first=False,
                             n_alias=len(partials))
    in_specs = [
        pl.BlockSpec((tm, D_MODEL), lambda i, j: (i + r0, 0)),
        pl.BlockSpec((1, D_MODEL), lambda i, j: (0, 0)),
        w_spec,
        pl.BlockSpec((1, tn), lambda i, j: (0, j)),
    ]
    args = [h, g2, w_bf, head_gain]
    out_specs = [pl.BlockSpec((tm, tn), lambda i, j: (i + r0, j))]
    if has_extra:
        in_specs.append(pl.BlockSpec((None, D_MODEL, nx), lambda i, j: (layer, 0, 0)))
        args.append(w_extra)
        out_specs.append(pl.BlockSpec((tm, nx), lambda i, j: (i + r0, 0)))
    n_in = len(in_specs)
    in_specs += [pl.BlockSpec(memory_space=pl.ANY)] * len(partials)
    args += partials
    outs = pl.pallas_call(
        kern,
        out_shape=tuple(out_shapes),
        grid=(N_TOK // tm - r0, n // tn),
        in_specs=in_specs,
        out_specs=tuple(out_specs),
        scratch_shapes=[pltpu.VMEM((tm, D_MODEL), BF16)],
        input_output_aliases={n_in + k: k for k in range(len(partials))},
        compiler_params=_params("parallel", "arbitrary"),
        name="inproj_main",
    )(*args)
    return outs if has_extra else outs[0]


def _outproj_kernel(a_ref, w_ref, h_ref, o_ref):
    o_ref[...] = h_ref[...] + _dot(a_ref[...], w_ref[...])


def _outproj_split_kernel(ap_ref, as_ref, w_ref, h_ref, o_ref):
    i = pl.program_id(1)

    @pl.when(i < N_PROMPT_TILES)
    def _():
        o_ref[...] = h_ref[...] + _dot(ap_ref[...], w_ref[...])

    @pl.when(i >= N_PROMPT_TILES)
    def _():
        o_ref[...] = h_ref[...] + _dot(as_ref[...], w_ref[...])


def outproj(h, a, w_all, layer, *, tn=1024):
    tm = TM_SPLIT
    k = w_all.shape[1]
    w_spec = pl.BlockSpec((None, k, tn), lambda j, i: (layer, 0, j))
    h_spec = pl.BlockSpec((tm, tn), lambda j, i: (i, j))
    if isinstance(a, tuple):
        kern = _outproj_split_kernel
        a_specs = [pl.BlockSpec((tm, k), lambda j, i: (jnp.minimum(i, N_PROMPT_TILES - 1), 0)),
                   pl.BlockSpec((tm, k), lambda j, i: (0, 0))]
        args = list(a)
    else:
        kern = _outproj_kernel
        a_specs = [pl.BlockSpec((tm, k), lambda j, i: (i, 0))]
        args = [a]
    return pl.pallas_call(
        kern,
        out_shape=jax.ShapeDtypeStruct((N_TOK, D_MODEL), F32),
        grid=(D_MODEL // tn, N_TOK // tm),
        in_specs=a_specs + [w_spec, h_spec],
        out_specs=h_spec,
        compiler_params=_params("parallel", "arbitrary"),
        name="outproj",
    )(*args, w_all, h)


def _ple_update(h_ref, p_ref, g_ref, wp_ref, wgt_ref):
    x = h_ref[...]
    xn = _rms_rows(x, g_ref[...]).astype(BF16)
    gate = jax.nn.sigmoid(_dot(xn, wgt_ref[...]))
    emb = _dot(p_ref[...].astype(BF16), wp_ref[...])
    return x + emb * gate


def _ple_kernel(h_ref, p_ref, g_ref, wp_ref, wgt_ref, o_ref):
    o_ref[...] = _ple_update(h_ref, p_ref, g_ref, wp_ref, wgt_ref)


def _ple_split_kernel(h_ref, p_ref, g_ref, wp_ref, wgt_ref, op_ref, os_ref):
    i = pl.program_id(0)

    @pl.when(i < N_PROMPT_TILES)
    def _():
        op_ref[...] = _ple_update(h_ref, p_ref, g_ref, wp_ref, wgt_ref)

    @pl.when(i >= N_PROMPT_TILES)
    def _():
        os_ref[...] = _ple_update(h_ref, p_ref, g_ref, wp_ref, wgt_ref)


def ple_add(h, p_all, g_all, w_ple_all, w_gate_all, layer, *, split=False):
    tm = TM_SPLIT if split else TM
    in_specs = [
        pl.BlockSpec((tm, D_MODEL), lambda i: (i, 0)),
        pl.BlockSpec((None, tm, PLE_DIM), lambda i: (layer, i, 0)),
        pl.BlockSpec((None, 1, D_MODEL), lambda i: (layer, 0, 0)),
        pl.BlockSpec((None, PLE_DIM, D_MODEL), lambda i: (layer, 0, 0)),
        pl.BlockSpec((None, D_MODEL, D_MODEL), lambda i: (layer, 0, 0)),
    ]
    if split:
        kern = _ple_split_kernel
        out_shape = (jax.ShapeDtypeStruct((N_PROMPT, D_MODEL), F32), jax.ShapeDtypeStruct((N_SAMPLE, D_MODEL), F32))
        out_specs = (pl.BlockSpec((tm, D_MODEL), lambda i: (jnp.minimum(i, N_PROMPT_TILES - 1), 0)),
                     pl.BlockSpec((tm, D_MODEL), lambda i: (0, 0)))
    else:
        kern = _ple_kernel
        out_shape = jax.ShapeDtypeStruct((N_TOK, D_MODEL), F32)
        out_specs = pl.BlockSpec((tm, D_MODEL), lambda i: (i, 0))
    return pl.pallas_call(
        kern,
        out_shape=out_shape,
        grid=(N_TOK // tm,),
        in_specs=in_specs,
        out_specs=out_specs,
        compiler_params=_params("arbitrary"),
        name="ple_add",
    )(h, p_all, g_all.reshape(DEPTH, 1, D_MODEL), w_ple_all, w_gate_all)


def _ssd_kernel(zx_ref, dtr_ref, st0_ref, cv0_ref, cw_ref, cb_ref, dtb_ref, alog_ref, rep_ref, dsk_ref, gn_ref,
                y_ref, st_ref, cvt_ref, s_scr, xpad, act, pcum, dts):
    s = pl.program_id(0)
    is_sample = s >= N_PROMPT_CHUNKS
    pos = s % CHUNKS_PER_SEQ
    first = jnp.logical_or(is_sample, pos == 0)
    last = jnp.logical_or(is_sample, pos == CHUNKS_PER_SEQ - 1)

    @pl.when(jnp.logical_and(first, jnp.logical_not(is_sample)))
    def _():
        s_scr[...] = jnp.zeros_like(s_scr)
        xpad[0:8, :] = jnp.zeros((8, SSD_CONV_DIM), F32)

    @pl.when(is_sample)
    def _():
        for g in range(SSD_GROUPS):
            s_scr[g] = st0_ref[0, g].T
        xpad[0:8, :] = jnp.zeros((8, SSD_CONV_DIM), F32)
        xpad[5:8, :] = cv0_ref[0]

    xpad[8:72, :] = zx_ref[:, SSD_INNER:SSD_MAIN_WIDTH]
    conv = cb_ref[...] + cw_ref[3:4, :] * xpad[8:72, :]
    conv = conv + cw_ref[2:3, :] * xpad[7:71, :]
    conv = conv + cw_ref[1:2, :] * xpad[6:70, :]
    conv = conv + cw_ref[0:1, :] * xpad[5:69, :]
    act[...] = conv * jax.nn.sigmoid(conv)
    xpad[0:8, :] = xpad[64:72, :]

    x = dtr_ref[...] + dtb_ref[...]
    dt = jnp.maximum(x, 0.0) + jnp.log1p(jnp.exp(-jnp.abs(x)))
    run = dt * (-jnp.exp(alog_ref[...]))
    row = lax.broadcasted_iota(jnp.int32, (CHUNK, 128), 0)
    for k in (1, 2, 4, 8, 16, 32):
        run = run + jnp.where(row >= k, pltpu.roll(run, k, 0), 0.0)
    both = jnp.concatenate([dt, run], axis=0)
    hi = both.astype(BF16)
    r1 = both - hi.astype(F32)
    mid = r1.astype(BF16)
    lo = (r1 - mid.astype(F32)).astype(BF16)
    wide = _dot(jnp.concatenate([hi, mid, lo], axis=0), rep_ref[...])
    wide = (wide[0:2 * CHUNK] + wide[2 * CHUNK:4 * CHUNK]) + wide[4 * CHUNK:6 * CHUNK]
    dts[...] = wide[0:CHUNK]
    pcum[...] = wide[CHUNK:2 * CHUNK]

    gw = SSD_GROUP_WIDTH
    row_g = lax.broadcasted_iota(jnp.int32, (CHUNK, gw), 0)
    lane_g = lax.broadcasted_iota(jnp.int32, (CHUNK, gw), 1) % CHUNK
    diag = row_g == lane_g
    causal = row_g >= lane_g
    r4 = lax.broadcasted_iota(jnp.int32, (256, 256), 0) // CHUNK
    c4 = lax.broadcasted_iota(jnp.int32, (256, 256), 1) // SSD_HEAD_DIM
    head_diag = r4 == c4

    for g in range(SSD_GROUPS):
        cs = slice(g * gw, (g + 1) * gw)
        p_g = pcum[:, cs]
        dt_g = dts[:, cs]
        x_g = act[:, cs]
        b_g = act[:, SSD_INNER + g * SSD_STATE:SSD_INNER + (g + 1) * SSD_STATE].astype(BF16)
        c_g = act[:, SSD_INNER + SSD_GROUPS * SSD_STATE + g * SSD_STATE:
                  SSD_INNER + SSD_GROUPS * SSD_STATE + (g + 1) * SSD_STATE].astype(BF16)
        p_s = jnp.sum(jnp.where(diag, p_g, 0.0), axis=0, keepdims=True)
        dt_s = jnp.sum(jnp.where(diag, dt_g, 0.0), axis=0, keepdims=True)
        p_last = p_g[CHUNK - 1:CHUNK, :]
        cb = _dot_nt(c_g, jnp.concatenate([b_g] * 8, axis=0))
        m = (cb * jnp.exp(jnp.where(causal, p_g - p_s, NEG)) * dt_s).astype(BF16)
        x_bf = x_g.astype(BF16)
        halves = []
        for hh in range(2):
            hs = slice(hh * 256, (hh + 1) * 256)
            xh = x_bf[:, hs]
            x_bd = jnp.where(head_diag, jnp.concatenate([xh] * 4, axis=0), jnp.zeros((), BF16))
            halves.append(_dot(m[:, hs], x_bd))
        y = jnp.concatenate(halves, axis=1)
        st = s_scr[g]
        y = y + _dot(c_g, st.astype(BF16)) * jnp.exp(p_g)
        wx = (jnp.exp(p_last - p_g) * dt_g * x_g).astype(BF16)
        s_scr[g] = st * jnp.exp(p_last) + _dot_tn(b_g, wx)
        y = y + dsk_ref[:, cs] * x_g
        z = zx_ref[:, cs]
        y = y * (z * jax.nn.sigmoid(z))
        y_ref[:, cs] = _rms_rows(y, gn_ref[:, cs]).astype(BF16)

    @pl.when(last)
    def _():
        cvt_ref[0] = xpad[0:8, :]
        for g in range(SSD_GROUPS):
            st_ref[0, g] = s_scr[g].T


def _ssd_seq(s):
    return jnp.where(s < N_PROMPT_CHUNKS, s // CHUNKS_PER_SEQ, BATCH + s - N_PROMPT_CHUNKS)


def ssd_core(zx, dt_raw, state_all, conv_all, conv_w_all, layer, conv_b, dt_bias, a_log, d_skip, g_norm):
    n_seq = BATCH + DEC_BATCH
    pad_heads = lambda v: jnp.pad(v, (0, 128 - SSD_HEADS)).reshape(1, 128)
    samp = lambda s: jnp.maximum(s - N_PROMPT_CHUNKS, 0)
    row_spec = lambda w: pl.BlockSpec((1, w), lambda s: (0, 0))
    st_all = state_all.reshape(-1, DEC_BATCH, SSD_GROUPS, SSD_GROUP_WIDTH, SSD_STATE)
    head_of_channel = jnp.arange(SSD_INNER, dtype=jnp.int32) // SSD_HEAD_DIM
    rep = (jnp.arange(128, dtype=jnp.int32)[:, None] == head_of_channel[None, :]).astype(BF16)
    y, st, cvt = pl.pallas_call(
        _ssd_kernel,
        out_shape=(jax.ShapeDtypeStruct((N_TOK, SSD_INNER), BF16),
                   jax.ShapeDtypeStruct((n_seq, SSD_GROUPS, SSD_GROUP_WIDTH, SSD_STATE), F32),
                   jax.ShapeDtypeStruct((n_seq, 8, SSD_CONV_DIM), F32)),
        grid=(N_CHUNKS,),
        in_specs=[
            pl.BlockSpec((CHUNK, SSD_MAIN_WIDTH), lambda s: (s, 0)),
            pl.BlockSpec((CHUNK, 128), lambda s: (s, 0)),
            pl.BlockSpec((None, 1, SSD_GROUPS, SSD_GROUP_WIDTH, SSD_STATE), lambda s: (layer, samp(s), 0, 0, 0)),
            pl.BlockSpec((None, 1, 3, SSD_CONV_DIM), lambda s: (layer, samp(s), 0, 0)),
            pl.BlockSpec((None, 4, SSD_CONV_DIM), lambda s: (layer, 0, 0)),
            row_spec(SSD_CONV_DIM), row_spec(128), row_spec(128),
            pl.BlockSpec((128, SSD_INNER), lambda s: (0, 0)),
            row_spec(SSD_INNER), row_spec(SSD_INNER),
        ],
        out_specs=(pl.BlockSpec((CHUNK, SSD_INNER), lambda s: (s, 0)),
                   pl.BlockSpec((1, SSD_GROUPS, SSD_GROUP_WIDTH, SSD_STATE), lambda s: (_ssd_seq(s), 0, 0, 0)),
                   pl.BlockSpec((1, 8, SSD_CONV_DIM), lambda s: (_ssd_seq(s), 0, 0))),
        scratch_shapes=[
            pltpu.VMEM((SSD_GROUPS, SSD_STATE, SSD_GROUP_WIDTH), F32),
            pltpu.VMEM((CHUNK + 8, SSD_CONV_DIM), F32),
            pltpu.VMEM((CHUNK, SSD_CONV_DIM), F32),
            pltpu.VMEM((CHUNK, SSD_INNER), F32),
            pltpu.VMEM((CHUNK, SSD_INNER), F32),
        ],
        compiler_params=_params("arbitrary"),
        name="ssd_core",
    )(zx, dt_raw, st_all, conv_all, conv_w_all, conv_b.reshape(1, SSD_CONV_DIM), pad_heads(dt_bias),
      pad_heads(a_log), rep, jnp.repeat(d_skip, SSD_HEAD_DIM).reshape(1, SSD_INNER), g_norm.reshape(1, SSD_INNER))
    return y, st.reshape(n_seq, SSD_HEADS, SSD_HEAD_DIM, SSD_STATE), cvt[:, 5:8]


DIFF_SCALE = DIFF_HEAD_DIM ** -0.5
LOG2E = math.log2(math.e)


def _diff_lambda(lp_ref, lam_init):
    lp = lp_ref[...]
    a = jnp.sum(lp[0:1] * lp[1:2], axis=-1, keepdims=True)
    b = jnp.sum(lp[2:3] * lp[3:4], axis=-1, keepdims=True)
    return jnp.exp(a) - jnp.exp(b) + lam_init


def _alibi_slope(h):
    return jnp.exp2(-(jnp.zeros((1, 1), F32) + (h + 1).astype(F32)))


def _diff_prompt_kernel(q_ref, k_ref, v_ref, lp_ref, gs_ref, o_ref, *, lam_init, tq):
    h = pl.program_id(1)
    lam = _diff_lambda(lp_ref, lam_init)
    slope2 = _alibi_slope(h) * LOG2E
    c = DIFF_SCALE * LOG2E
    d = DIFF_HEAD_DIM
    t_i = lax.broadcasted_iota(jnp.int32, (tq, tq), 0)
    j_i = lax.broadcasted_iota(jnp.int32, (tq, tq), 1)
    bias_own = jnp.where(j_i // CHUNK <= t_i // CHUNK,
                         slope2 * (t_i - jnp.abs(t_i - j_i)).astype(F32), NEG)
    for qb in range(SEQ // tq):
        q0 = qb * tq
        rows = slice(q0, q0 + tq)
        if qb:
            bias_past = slope2 * (lax.broadcasted_iota(jnp.int32, (1, q0), 1) - q0).astype(F32)
        weights = []
        for m in range(2):
            cols = slice(m * d, (m + 1) * d)
            q = q_ref[rows, cols].astype(BF16)
            s_own = _dot_nt(q, k_ref[rows, cols].astype(BF16)) * c + bias_own
            mx = jnp.max(s_own, axis=-1, keepdims=True)
            if qb:
                s_past = _dot_nt(q, k_ref[0:q0, cols].astype(BF16)) * c + bias_past
                mx = jnp.maximum(mx, jnp.max(s_past, axis=-1, keepdims=True))
                e_past = jnp.exp2(s_past - mx)
            e_own = jnp.exp2(s_own - mx)
            den = jnp.sum(e_own, axis=-1, keepdims=True)
            if qb:
                den = den + jnp.sum(e_past, axis=-1, keepdims=True)
            weights.append((e_own, e_past if qb else None, 1.0 / den))
        (eo0, ep0, r0), (eo1, ep1, r1) = weights
        r1 = lam * r1
        o = _dot((eo0 * r0 - eo1 * r1).astype(BF16), v_ref[rows, :].astype(BF16))
        if qb:
            o = o + _dot((ep0 * r0 - ep1 * r1).astype(BF16), v_ref[0:q0, :].astype(BF16))
        o_ref[rows, :] = (_rms_rows(o, gs_ref[...]) * (1.0 - lam_init)).astype(BF16)


def diff_attn_prompt(qkv, lam_p, g_sub, lam_init, *, tq=256):
    w = 2 * DIFF_HEAD_DIM
    kern = functools.partial(_diff_prompt_kernel, lam_init=lam_init, tq=tq)
    return pl.pallas_call(
        kern,
        out_shape=jax.ShapeDtypeStruct((N_PROMPT, D_MODEL), BF16),
        grid=(BATCH, DIFF_HEADS),
        in_specs=[
            pl.BlockSpec((SEQ, w), lambda b, h: (b, h)),
            pl.BlockSpec((SEQ, w), lambda b, h: (b, DIFF_HEADS + h)),
            pl.BlockSpec((SEQ, w), lambda b, h: (b, 2 * DIFF_HEADS + h)),
            pl.BlockSpec((4, DIFF_HEAD_DIM), lambda b, h: (0, 0)),
            pl.BlockSpec((1, w), lambda b, h: (0, 0)),
        ],
        out_specs=pl.BlockSpec((SEQ, w), lambda b, h: (b, h)),
        compiler_params=_params("parallel", "parallel"),
        name="diff_attn_prompt",
    )(qkv, qkv, qkv, lam_p, g_sub.reshape(1, w))


def _diff_sample_kernel(q_ref, kn_ref, vn_ref, kc0_ref, kc1_ref, vc0_ref, vc1_ref, lp_ref, gs_ref, o_ref, vh_ref,
                        *, lam_init):
    kc_refs = (kc0_ref, kc1_ref)
    h = pl.program_id(1)
    lam = _diff_lambda(lp_ref, lam_init)
    slope = _alibi_slope(h)
    d = DIFF_HEAD_DIM
    t = DEC_SEQ
    tq_p = lax.broadcasted_iota(jnp.int32, (t, PAST_LEN), 0)
    kp_p = lax.broadcasted_iota(jnp.int32, (t, PAST_LEN), 1)
    bias_p = -slope * (PAST_LEN + tq_p - kp_p).astype(F32)
    tq_n = lax.broadcasted_iota(jnp.int32, (t, t), 0)
    kp_n = lax.broadcasted_iota(jnp.int32, (t, t), 1)
    bias_n = -slope * jnp.abs(tq_n - kp_n).astype(F32)
    pp, pn = [], []
    for m in range(2):
        cols = slice(m * d, (m + 1) * d)
        q = q_ref[:, cols].astype(BF16)
        kc = kc_refs[m].reshape(PAST_LEN, d)[...]
        sp = _dot_nt(q, kc.astype(BF16)) * DIFF_SCALE + bias_p
        sn = _dot_nt(q, kn_ref[:, cols].astype(BF16)) * DIFF_SCALE + bias_n
        mx = jnp.maximum(jnp.max(sp, axis=-1, keepdims=True), jnp.max(sn, axis=-1, keepdims=True))
        ep = jnp.exp(sp - mx)
        en = jnp.exp(sn - mx)
        den = jnp.sum(ep, axis=-1, keepdims=True) + jnp.sum(en, axis=-1, keepdims=True)
        pp.append(ep / den)
        pn.append(en / den)
    ap = (pp[0] - lam * pp[1]).astype(BF16)
    an = (pn[0] - lam * pn[1]).astype(BF16)
    for k in range(DIFF_HEADS):
        @pl.when(h == k)
        def _(k=k):
            for half, vc_ref in enumerate((vc0_ref, vc1_ref)):
                rows = vc_ref.reshape(PAST_LEN * DIFF_HEADS, d)
                vh_ref[:, half * d:(half + 1) * d] = rows[pl.ds(k, PAST_LEN, stride=DIFF_HEADS), :].astype(BF16)

    o = _dot(ap, vh_ref[...]) + _dot(an, vn_ref[...].astype(BF16))
    o_ref[...] = (_rms_rows(o, gs_ref[...]) * (1.0 - lam_init)).astype(BF16)


def diff_attn_sample(qkv, cache_k, cache_v, lam_p, g_sub, lam_init):
    w = 2 * DIFF_HEAD_DIM
    r0 = N_PROMPT // DEC_SEQ
    kern = functools.partial(_diff_sample_kernel, lam_init=lam_init)
    k_rows = cache_k.reshape(DEC_BATCH, PAST_LEN, 2 * DIFF_HEADS, 1, DIFF_HEAD_DIM)
    k_spec = lambda m: pl.BlockSpec((None, PAST_LEN, None, 1, DIFF_HEAD_DIM), lambda b, h: (b, 0, 2 * h + m, 0, 0))
    v_spec = lambda half: pl.BlockSpec((None, PAST_LEN, DIFF_HEADS, DIFF_HEAD_DIM), lambda b, h: (b, 0, 0, half))
    return pl.pallas_call(
        kern,
        out_shape=jax.ShapeDtypeStruct((N_SAMPLE, D_MODEL), BF16),
        grid=(DEC_BATCH, DIFF_HEADS),
        in_specs=[
            pl.BlockSpec((DEC_SEQ, w), lambda b, h: (r0 + b, h)),
            pl.BlockSpec((DEC_SEQ, w), lambda b, h: (r0 + b, DIFF_HEADS + h)),
            pl.BlockSpec((DEC_SEQ, w), lambda b, h: (r0 + b, 2 * DIFF_HEADS + h)),
            k_spec(0),
            k_spec(1),
            v_spec(0),
            v_spec(1),
            pl.BlockSpec((4, DIFF_HEAD_DIM), lambda b, h: (0, 0)),
            pl.BlockSpec((1, w), lambda b, h: (0, 0)),
        ],
        out_specs=pl.BlockSpec((DEC_SEQ, w), lambda b, h: (b, h)),
        scratch_shapes=[pltpu.VMEM((PAST_LEN, w), BF16)],
        compiler_params=_params("parallel", "arbitrary"),
        name="diff_attn_sample",
    )(qkv, qkv, qkv, k_rows, k_rows, cache_v, cache_v, lam_p, g_sub.reshape(1, w))


BAND_SCALE = BAND_HEAD_DIM ** -0.5
BAND_TQ = 256
BAND_E = 1024


def _band_bias(e_ref, tq):
    eb = jnp.broadcast_to(e_ref[0], (tq, BAND_E))
    return pltpu.roll(eb, BAND_E - (BAND_TQ - 1), 1, stride=1, stride_axis=0)


def _band_prompt_kernel(q_ref, k_ref, v_ref, e_ref, o_ref):
    tq = BAND_TQ
    win = BAND_WINDOW + tq
    bias = _band_bias(e_ref, tq)[:, :win]
    qc = lax.broadcasted_iota(jnp.int32, (tq, win), 0) // CHUNK
    jc = lax.broadcasted_iota(jnp.int32, (tq, win), 1) // CHUNK
    bias = jnp.where(jnp.logical_and(jc >= qc, jc <= qc + BAND_WINDOW // CHUNK), bias, NEG)
    for qb in range(SEQ // tq):
        q0 = qb * tq
        k0 = max(0, q0 - BAND_WINDOW)
        nk = q0 + tq - k0
        rows = slice(q0, q0 + tq)
        sc = _dot_nt(q_ref[rows, :].astype(BF16), k_ref[k0:k0 + nk, :].astype(BF16)) * BAND_SCALE
        sc = sc + bias[:, win - nk:]
        e = jnp.exp(sc - jnp.max(sc, axis=-1, keepdims=True))
        p = (e / jnp.sum(e, axis=-1, keepdims=True)).astype(BF16)
        o_ref[rows, :] = _dot(p, v_ref[k0:k0 + nk, :].astype(BF16)).astype(BF16)


def _band_e_rows(table):
    edge = jnp.broadcast_to(table[:, 2 * REL_CLIP:], (BAND_HEADS, BAND_E - (2 * REL_CLIP + 1)))
    return jnp.concatenate([edge, table[:, ::-1]], axis=1).reshape(BAND_HEADS, 1, BAND_E)


def band_attn_prompt(qkv, table):
    d = BAND_HEAD_DIM
    return pl.pallas_call(
        _band_prompt_kernel,
        out_shape=jax.ShapeDtypeStruct((N_PROMPT, D_MODEL), BF16),
        grid=(BATCH, BAND_HEADS),
        in_specs=[
            pl.BlockSpec((SEQ, d), lambda b, h: (b, h)),
            pl.BlockSpec((SEQ, d), lambda b, h: (b, BAND_HEADS + h)),
            pl.BlockSpec((SEQ, d), lambda b, h: (b, 2 * BAND_HEADS + h)),
            pl.BlockSpec((1, 1, BAND_E), lambda b, h: (h, 0, 0)),
        ],
        out_specs=pl.BlockSpec((SEQ, d), lambda b, h: (b, h)),
        compiler_params=_params("parallel", "parallel"),
        name="band_attn_prompt",
    )(qkv, qkv, qkv, _band_e_rows(table))


def _band_sample_kernel(q_ref, kn_ref, vn_ref, kc_ref, vc_ref, e_ref, o_ref):
    t = DEC_SEQ
    bias = _band_bias(e_ref, t)
    q = q_ref[...].astype(BF16)
    kc = kc_ref.reshape(BAND_WINDOW, BAND_HEAD_DIM)[...]
    vc = vc_ref.reshape(BAND_WINDOW, BAND_HEAD_DIM)[...]
    sp = _dot_nt(q, kc.astype(BF16)) * BAND_SCALE + bias[:, :BAND_WINDOW]
    sn = _dot_nt(q, kn_ref[...].astype(BF16)) * BAND_SCALE + bias[:, BAND_WINDOW:BAND_WINDOW + t]
    mx = jnp.maximum(jnp.max(sp, axis=-1, keepdims=True), jnp.max(sn, axis=-1, keepdims=True))
    ep = jnp.exp(sp - mx)
    en = jnp.exp(sn - mx)
    den = jnp.sum(ep, axis=-1, keepdims=True) + jnp.sum(en, axis=-1, keepdims=True)
    o = (_dot((ep / den).astype(BF16), vc.astype(BF16))
         + _dot((en / den).astype(BF16), vn_ref[...].astype(BF16)))
    o_ref[...] = o.astype(BF16)


def band_attn_sample(qkv, cache_k, cache_v, table):
    d = BAND_HEAD_DIM
    r0 = N_PROMPT // DEC_SEQ
    rows_view = (DEC_BATCH, BAND_WINDOW, BAND_HEADS, 1, d)
    cache_spec = pl.BlockSpec((None, BAND_WINDOW, None, 1, d), lambda b, h: (b, 0, h, 0, 0))
    return pl.pallas_call(
        _band_sample_kernel,
        out_shape=jax.ShapeDtypeStruct((N_SAMPLE, D_MODEL), BF16),
        grid=(DEC_BATCH, BAND_HEADS),
        in_specs=[
            pl.BlockSpec((DEC_SEQ, d), lambda b, h: (r0 + b, h)),
            pl.BlockSpec((DEC_SEQ, d), lambda b, h: (r0 + b, BAND_HEADS + h)),
            pl.BlockSpec((DEC_SEQ, d), lambda b, h: (r0 + b, 2 * BAND_HEADS + h)),
            cache_spec,
            cache_spec,
            pl.BlockSpec((1, 1, BAND_E), lambda b, h: (h, 0, 0)),
        ],
        out_specs=pl.BlockSpec((DEC_SEQ, d), lambda b, h: (b, h)),
        compiler_params=_params("parallel", "parallel"),
        name="band_attn_sample",
    )(qkv, qkv, qkv, cache_k.reshape(rows_view), cache_v.reshape(rows_view), _band_e_rows(table))


def kernel(x_prompt, x_sample, p_prompt, p_sample, state_ssm, state_conv, cache_k_diff, cache_v_diff,
           cache_k_band, cache_v_band, g_ffn, w_ffn_gate, w_ffn_up, w_ffn_down, g_mix,
           ssd_w_in, ssd_conv_w, ssd_conv_b, ssd_dt_bias, ssd_a_log, ssd_d, ssd_g_norm, ssd_w_out,
           diff_w_in, diff_g_q, diff_g_k, diff_lambda, diff_g_sub, diff_w_out,
           band_w_in, band_g_q, band_g_k, band_rel_bias, band_w_out,
           g_ple, w_ple, w_ple_gate):
    h = jnp.concatenate([x_prompt.reshape(N_PROMPT, D_MODEL), x_sample.reshape(N_SAMPLE, D_MODEL)], axis=0)
    p_all = jnp.concatenate([p_prompt.reshape(DEPTH, N_PROMPT, PLE_DIM),
                             p_sample.reshape(DEPTH, N_SAMPLE, PLE_DIM)], axis=1)
    bf = lambda w: w.astype(BF16)
    wg_all, wu_all, wd_all = w_ffn_gate, w_ffn_up, w_ffn_down
    w_ple_all, w_gate_all = bf(w_ple), bf(w_ple_gate)
    ssd_w_main = bf(ssd_w_in)
    ssd_w_dt = bf(jnp.pad(ssd_w_in[:, :, SSD_MAIN_WIDTH:], ((0, 0), (0, 0), (0, 128 - SSD_HEADS))))
    ssd_w_out_all, diff_w_in_all, diff_w_out_all = bf(ssd_w_out), diff_w_in, bf(diff_w_out)
    band_w_in_all, band_w_out_all = band_w_in, bf(band_w_out)

    def prompt_rows(a):
        return a[:N_PROMPT].reshape(BATCH, SEQ, -1)

    def sample_rows(a):
        return a[N_PROMPT:].reshape(DEC_BATCH, DEC_SEQ, -1)

    ssm_p, conv_p, ssm_s, conv_s = [], [], [], []
    kd_p = vd_p = kb_p = vb_p = kd_s = vd_s = kb_s = vb_s = None
    for i in range(DEPTH):
        kind, j = i % 3, i // 3
        h = ffn_half(h, g_ffn, wg_all, wu_all, wd_all, i, 0)
        if kind == 0:
            zx, dt_raw = inproj(h, g_mix[i], ssd_w_main, j, n_cols=SSD_MAIN_WIDTH, w_extra=ssd_w_dt)
            y, st, cvt = ssd_core(zx, dt_raw, state_ssm, state_conv, ssd_conv_w, j, ssd_conv_b[j], ssd_dt_bias[j],
                                  ssd_a_log[j], ssd_d[j], ssd_g_norm[j])
            h = outproj(h, y, ssd_w_out_all, j)
            ssm_p.append(st[:BATCH])
            ssm_s.append(st[BATCH:])
            conv_p.append(cvt[:BATCH])
            conv_s.append(cvt[BATCH:])
        elif kind == 1:
            lam_init = 0.8 - 0.6 * math.exp(-0.3 * i)
            gains = jnp.concatenate([jnp.tile(diff_g_q[j].reshape(-1), DIFF_HEADS),
                                     jnp.tile(diff_g_k[j].reshape(-1), DIFF_HEADS),
                                     jnp.ones((D_MODEL,), F32)]).reshape(1, 3 * D_MODEL)
            qkv = inproj(h, g_mix[i], diff_w_in_all, j, gains, norm_cols=2 * D_MODEL)
            o_p = diff_attn_prompt(qkv, diff_lambda[j], diff_g_sub[j], lam_init)
            o_s = diff_attn_sample(qkv, cache_k_diff[j], cache_v_diff[j], diff_lambda[j], diff_g_sub[j], lam_init)
            h = outproj(h, (o_p, o_s), diff_w_out_all, j)
            k_new, v_new = qkv[:, D_MODEL:2 * D_MODEL], qkv[:, 2 * D_MODEL:]
            kd_p = prompt_rows(k_new).reshape(1, BATCH, SEQ, DIFF_HEADS, 2, DIFF_HEAD_DIM)
            vd_p = prompt_rows(v_new).reshape(1, BATCH, SEQ, DIFF_HEADS, 2 * DIFF_HEAD_DIM)
            kd_s = sample_rows(k_new).reshape(1, DEC_BATCH, DEC_SEQ, DIFF_HEADS, 2, DIFF_HEAD_DIM)
            vd_s = sample_rows(v_new).reshape(1, DEC_BATCH, DEC_SEQ, DIFF_HEADS, 2 * DIFF_HEAD_DIM)
        else:
            gains = jnp.concatenate([jnp.tile(band_g_q[j], BAND_HEADS), jnp.tile(band_g_k[j], BAND_HEADS),
                                     jnp.ones((D_MODEL,), F32)]).reshape(1, 3 * D_MODEL)
            qkv = inproj(h, g_mix[i], band_w_in_all, j, gains, norm_cols=2 * D_MODEL)
            o_p = band_attn_prompt(qkv, band_rel_bias[j])
            o_s = band_attn_sample(qkv, cache_k_band[j], cache_v_band[j], band_rel_bias[j])
            h = outproj(h, (o_p, o_s), band_w_out_all, j)
            k_new, v_new = qkv[:, D_MODEL:2 * D_MODEL], qkv[:, 2 * D_MODEL:]
            kb_p = prompt_rows(k_new)[:, SEQ - BAND_WINDOW:].reshape(1, BATCH, BAND_WINDOW, BAND_HEADS, BAND_HEAD_DIM)
            vb_p = prompt_rows(v_new)[:, SEQ - BAND_WINDOW:].reshape(1, BATCH, BAND_WINDOW, BAND_HEADS, BAND_HEAD_DIM)
            kb_s = sample_rows(k_new).reshape(1, DEC_BATCH, DEC_SEQ, BAND_HEADS, BAND_HEAD_DIM)
            vb_s = sample_rows(v_new).reshape(1, DEC_BATCH, DEC_SEQ, BAND_HEADS, BAND_HEAD_DIM)
        h = ffn_half(h, g_ffn, wg_all, wu_all, wd_all, i, 1)
        h = ple_add(h, p_all, g_ple, w_ple_all, w_gate_all, i, split=(i == DEPTH - 1))
    h_p, h_s = h
    return (h_p.reshape(BATCH, SEQ, D_MODEL), h_s.reshape(DEC_BATCH, DEC_SEQ, D_MODEL),
            jnp.stack(ssm_p), jnp.stack(conv_p), kd_p, vd_p, kb_p, vb_p,
            jnp.stack(ssm_s), jnp.stack(conv_s), kd_s, vd_s, kb_s, vb_s)
```

```python
import functools
import math

import jax
import jax.numpy as jnp
from jax import lax
from jax.experimental import pallas as pl
from jax.experimental.pallas import tpu as pltpu

F32 = jnp.float32
BF16 = jnp.bfloat16

D_MODEL = 2048
BATCH = 4
SEQ = 2048
DEPTH = 4
DEC_BATCH = 8
DEC_SEQ = 64
PAST_LEN = 2048
CHUNK = 64
NORM_EPS = 1e-6
FFN_DIM = 5632
PLE_DIM = 256
SSD_INNER = 4096
SSD_HEADS = 64
SSD_HEAD_DIM = 64
SSD_GROUPS = 8
SSD_STATE = 128
SSD_GROUP_WIDTH = SSD_INNER // SSD_GROUPS
SSD_CONV_DIM = SSD_INNER + 2 * SSD_GROUPS * SSD_STATE
SSD_MAIN_WIDTH = SSD_INNER + SSD_CONV_DIM
SSD_EXT_WIDTH = SSD_MAIN_WIDTH + SSD_INNER
DIFF_HEADS = 8
DIFF_HEAD_DIM = 128
BAND_HEADS = 16
BAND_HEAD_DIM = 128
BAND_WINDOW = 512
REL_CLIP = 256

N_PROMPT = BATCH * SEQ
N_SAMPLE = DEC_BATCH * DEC_SEQ
N_TOK = N_PROMPT + N_SAMPLE
CHUNKS_PER_SEQ = SEQ // CHUNK
N_PROMPT_CHUNKS = N_PROMPT // CHUNK
N_CHUNKS = N_TOK // CHUNK

TM_WIDE = 1088
TM = 544
TM_SPLIT = 512
N_PROMPT_TILES = N_PROMPT // TM_SPLIT
VMEM_LIMIT = 56 * 1024 * 1024
NEG = -1e30


def _params(*sem, vmem=VMEM_LIMIT):
    return pltpu.CompilerParams(dimension_semantics=sem, vmem_limit_bytes=vmem)


def _rms_rows(x, g):
    ms = jnp.mean(x * x, axis=-1, keepdims=True)
    return x * lax.rsqrt(ms + NORM_EPS) * g


def _dot(a, b):
    return jnp.dot(a, b, preferred_element_type=F32)


def _dot_nt(a, b):
    return lax.dot_general(a, b, (((1,), (1,)), ((), ())), preferred_element_type=F32)


def _dot_tn(a, b):
    return lax.dot_general(a, b, (((0,), (0,)), ((), ())), preferred_element_type=F32)


def _ffn_step(first, h_ref, g_ref, wg_ref, wu_ref, wd_ref, o_ref, xn_ref):
    @pl.when(first)
    def _():
        x = h_ref[...]
        xn_ref[...] = _rms_rows(x, g_ref[...]).astype(BF16)
        o_ref[...] = x

    xn = xn_ref[...]
    gate = _dot(xn, wg_ref[...])
    up = _dot(xn, wu_ref[...])
    act = (0.5 * (gate * jax.nn.sigmoid(gate)) * up).astype(BF16)
    o_ref[...] += _dot(act, wd_ref[...])


def _ffn_first_kernel(h_ref, g_ref, wg_ref, wu_ref, wd_ref, o_ref, wg_bf_ref, wu_bf_ref, wd_bf_ref, xn_ref):
    wg_bf_ref[...] = wg_ref[...].astype(BF16)
    wu_bf_ref[...] = wu_ref[...].astype(BF16)
    wd_bf_ref[...] = wd_ref[...].astype(BF16)
    _ffn_step(pl.program_id(0) == 0, h_ref, g_ref, wg_bf_ref, wu_bf_ref, wd_bf_ref, o_ref, xn_ref)


def _ffn_rest_kernel(h_ref, g_ref, wg_ref, wu_ref, wd_ref, partial_ref, o_ref, xn_ref):
    del partial_ref
    _ffn_step(pl.program_id(1) == 0, h_ref, g_ref, wg_ref, wu_ref, wd_ref, o_ref, xn_ref)


def ffn_half(h, g_all, wg_all, wu_all, wd_all, layer, half, *, tf=512, tf_first=256):
    tm = TM
    tm_first = TM_WIDE
    skip = tm_first // tm
    g4 = g_all.reshape(DEPTH, 2, 1, D_MODEL)
    g_spec1 = pl.BlockSpec((None, None, 1, D_MODEL), lambda f: (layer, half, 0, 0))
    partial, wg_bf, wu_bf, wd_bf = pl.pallas_call(
        _ffn_first_kernel,
        out_shape=(jax.ShapeDtypeStruct((N_TOK, D_MODEL), F32),
                   jax.ShapeDtypeStruct((D_MODEL, FFN_DIM), BF16),
                   jax.ShapeDtypeStruct((D_MODEL, FFN_DIM), BF16),
                   jax.ShapeDtypeStruct((FFN_DIM, D_MODEL), BF16)),
        grid=(FFN_DIM // tf_first,),
        in_specs=[
            pl.BlockSpec((tm_first, D_MODEL), lambda f: (0, 0), pipeline_mode=pl.Buffered(1)),
            g_spec1,
            pl.BlockSpec((None, None, D_MODEL, tf_first), lambda f: (layer, half, 0, f)),
            pl.BlockSpec((None, None, D_MODEL, tf_first), lambda f: (layer, half, 0, f)),
            pl.BlockSpec((None, None, tf_first, D_MODEL), lambda f: (layer, half, f, 0)),
        ],
        out_specs=(pl.BlockSpec((tm_first, D_MODEL), lambda f: (0, 0)),
                   pl.BlockSpec((D_MODEL, tf_first), lambda f: (0, f)),
                   pl.BlockSpec((D_MODEL, tf_first), lambda f: (0, f)),
                   pl.BlockSpec((tf_first, D_MODEL), lambda f: (f, 0))),
        scratch_shapes=[pltpu.VMEM((tm_first, D_MODEL), BF16)],
        compiler_params=_params("arbitrary"),
        name="ffn_first",
    )(h, g4, wg_all, wu_all, wd_all)
    return pl.pallas_call(
        _ffn_rest_kernel,
        out_shape=jax.ShapeDtypeStruct((N_TOK, D_MODEL), F32),
        grid=(N_TOK // tm - skip, FFN_DIM // tf),
        in_specs=[
            pl.BlockSpec((tm, D_MODEL), lambda i, f: (i + skip, 0)),
            pl.BlockSpec((None, None, 1, D_MODEL), lambda i, f: (layer, half, 0, 0)),
            pl.BlockSpec((D_MODEL, tf), lambda i, f: (0, f)),
            pl.BlockSpec((D_MODEL, tf), lambda i, f: (0, f)),
            pl.BlockSpec((tf, D_MODEL), lambda i, f: (f, 0)),
            pl.BlockSpec(memory_space=pl.ANY),
        ],
        out_specs=pl.BlockSpec((tm, D_MODEL), lambda i, f: (i + skip, 0)),
        scratch_shapes=[pltpu.VMEM((tm, D_MODEL), BF16)],
        input_output_aliases={5: 0},
        compiler_params=_params("parallel", "arbitrary"),
        name="ffn_rest",
    )(h, g4, wg_bf, wu_bf, wd_bf, partial)


def _inproj_kernel(h_ref, g_ref, w_ref, hg_ref, *rest, norm_tiles, tn, has_extra, first, n_alias=0):
    rest = list(rest)
    wx_ref = rest.pop(0) if has_extra else None
    del rest[:n_alias]
    o_ref = rest.pop(0)
    ox_ref = rest.pop(0) if has_extra else None
    wbf_ref = rest.pop(0) if first else None
    (xn_ref,) = rest
    j = pl.program_id(0 if first else 1)

    @pl.when(j == 0)
    def _():
        xn = _rms_rows(h_ref[...], g_ref[...]).astype(BF16)
        xn_ref[...] = xn
        if has_extra:
            ox_ref[...] = _dot(xn, wx_ref[...])

    if first:
        w = w_ref[...].astype(BF16)
        wbf_ref[...] = w
    else:
        w = w_ref[...]
    y = _dot(xn_ref[...], w)
    if norm_tiles == 0:
        o_ref[...] = y
        return

    @pl.when(j >= norm_tiles)
    def _():
        o_ref[...] = y

    @pl.when(j < norm_tiles)
    def _():
        for c in range(tn // 128):
            sl = slice(c * 128, (c + 1) * 128)
            o_ref[:, sl] = _rms_rows(y[:, sl], hg_ref[:, sl])


def inproj(h, g, w_all, layer, head_gain=None, *, n_cols=None, norm_cols=0, w_extra=None, tn=1024, tn_first=512):
    tm = TM_WIDE
    n = w_all.shape[2] if n_cols is None else n_cols
    if head_gain is None:
        head_gain = jnp.ones((1, n), F32)
    has_extra = w_extra is not None
    g2 = g.reshape(1, D_MODEL)
    nx = w_extra.shape[2] if has_extra else 0
    out_shapes = [jax.ShapeDtypeStruct((N_TOK, n), F32)]
    if has_extra:
        out_shapes.append(jax.ShapeDtypeStruct((N_TOK, nx), F32))

    partials = []
    if w_all.dtype == BF16:
        w_bf = w_all
        w_spec = pl.BlockSpec((None, D_MODEL, tn), lambda i, j: (layer, 0, j))
    else:
        kern = functools.partial(_inproj_kernel, norm_tiles=norm_cols // tn_first, tn=tn_first, has_extra=has_extra,
                                 first=True)
        in_specs = [
            pl.BlockSpec((tm, D_MODEL), lambda j: (0, 0)),
            pl.BlockSpec((1, D_MODEL), lambda j: (0, 0)),
            pl.BlockSpec((None, D_MODEL, tn_first), lambda j: (layer, 0, j)),
            pl.BlockSpec((1, tn_first), lambda j: (0, j)),
        ]
        args = [h, g2, w_all, head_gain]
        out_specs = [pl.BlockSpec((tm, tn_first), lambda j: (0, j))]
        if has_extra:
            in_specs.append(pl.BlockSpec((None, D_MODEL, nx), lambda j: (layer, 0, 0)))
            args.append(w_extra)
            out_specs.append(pl.BlockSpec((tm, nx), lambda j: (0, 0)))
        out_specs.append(pl.BlockSpec((D_MODEL, tn_first), lambda j: (0, j)))
        *partials, w_bf = pl.pallas_call(
            kern,
            out_shape=tuple(out_shapes) + (jax.ShapeDtypeStruct((D_MODEL, n), BF16),),
            grid=(n // tn_first,),
            in_specs=in_specs,
            out_specs=tuple(out_specs),
            scratch_shapes=[pltpu.VMEM((tm, D_MODEL), BF16)],
            compiler_params=_params("arbitrary"),
            name="inproj_first",
        )(*args)
        w_spec = pl.BlockSpec((D_MODEL, tn), lambda i, j: (0, j))

    r0 = len(partials) and 1
    kern = functools.partial(_inproj_kernel, norm_tiles=norm_cols // tn, tn=tn, has_extra=has_extra, first=False,
                             n_alias=len(partials))
    in_specs = [
        pl.BlockSpec((tm, D_MODEL), lambda i, j: (i + r0, 0)),
        pl.BlockSpec((1, D_MODEL), lambda i, j: (0, 0)),
        w_spec,
        pl.BlockSpec((1, tn), lambda i, j: (0, j)),
    ]
    args = [h, g2, w_bf, head_gain]
    out_specs = [pl.BlockSpec((tm, tn), lambda i, j: (i + r0, j))]
    if has_extra:
        in_specs.append(pl.BlockSpec((None, D_MODEL, nx), lambda i, j: (layer, 0, 0)))
        args.append(w_extra)
        out_specs.append(pl.BlockSpec((tm, nx), lambda i, j: (i + r0, 0)))
    n_in = len(in_specs)
    in_specs += [pl.BlockSpec(memory_space=pl.ANY)] * len(partials)
    args += partials
    outs = pl.pallas_call(
        kern,
        out_shape=tuple(out_shapes),
        grid=(N_TOK // tm - r0, n // tn),
        in_specs=in_specs,
        out_specs=tuple(out_specs),
        scratch_shapes=[pltpu.VMEM((tm, D_MODEL), BF16)],
        input_output_aliases={n_in + k: k for k in range(len(partials))},
        compiler_params=_params("parallel", "arbitrary"),
        name="inproj_main",
    )(*args)
    return outs if has_extra else outs[0]


def _cast_once(first, w_ref, wbf_ref):
    @pl.when(first)
    def _():
        wbf_ref[...] = w_ref[...].astype(BF16)


def _outproj_kernel(a_ref, w_ref, h_ref, o_ref, wbf_ref):
    _cast_once(pl.program_id(1) == 0, w_ref, wbf_ref)
    o_ref[...] = h_ref[...] + _dot(a_ref[...], wbf_ref[...])


def _outproj_split_kernel(ap_ref, as_ref, w_ref, h_ref, o_ref, wbf_ref):
    i = pl.program_id(1)
    _cast_once(i == 0, w_ref, wbf_ref)

    @pl.when(i < N_PROMPT_TILES)
    def _():
        o_ref[...] = h_ref[...] + _dot(ap_ref[...], wbf_ref[...])

    @pl.when(i >= N_PROMPT_TILES)
    def _():
        o_ref[...] = h_ref[...] + _dot(as_ref[...], wbf_ref[...])


def outproj(h, a, w_all, layer):
    tm = TM_SPLIT
    k = w_all.shape[1]
    tn = 1024 if k <= D_MODEL else 512
    w_spec = pl.BlockSpec((None, k, tn), lambda j, i: (layer, 0, j))
    h_spec = pl.BlockSpec((tm, tn), lambda j, i: (i, j))
    if isinstance(a, tuple):
        kern = _outproj_split_kernel
        a_specs = [pl.BlockSpec((tm, k), lambda j, i: (jnp.minimum(i, N_PROMPT_TILES - 1), 0)),
                   pl.BlockSpec((tm, k), lambda j, i: (0, 0))]
        args = list(a)
    else:
        kern = _outproj_kernel
        a_specs = [pl.BlockSpec((tm, k), lambda j, i: (i, 0))]
        args = [a]
    return pl.pallas_call(
        kern,
        out_shape=jax.ShapeDtypeStruct((N_TOK, D_MODEL), F32),
        grid=(D_MODEL // tn, N_TOK // tm),
        in_specs=a_specs + [w_spec, h_spec],
        out_specs=h_spec,
        scratch_shapes=[pltpu.VMEM((k, tn), BF16)],
        compiler_params=_params("parallel", "arbitrary"),
        name="outproj",
    )(*args, w_all, h)


def _ple_update(h_ref, p_ref, g_ref, wp_ref, wgt_ref):
    x = h_ref[...]
    xn = _rms_rows(x, g_ref[...]).astype(BF16)
    gate = jax.nn.sigmoid(_dot(xn, wgt_ref[...]))
    emb = _dot(p_ref[...].astype(BF16), wp_ref[...])
    return x + emb * gate


def _ple_kernel(h_ref, p_ref, g_ref, wp_ref, wgt_ref, o_ref, wp_bf_ref, wgt_bf_ref):
    first = pl.program_id(0) == 0
    _cast_once(first, wp_ref, wp_bf_ref)
    _cast_once(first, wgt_ref, wgt_bf_ref)
    o_ref[...] = _ple_update(h_ref, p_ref, g_ref, wp_bf_ref, wgt_bf_ref)


def _ple_split_kernel(h_ref, p_ref, g_ref, wp_ref, wgt_ref, op_ref, os_ref, wp_bf_ref, wgt_bf_ref):
    i = pl.program_id(0)
    _cast_once(i == 0, wp_ref, wp_bf_ref)
    _cast_once(i == 0, wgt_ref, wgt_bf_ref)

    @pl.when(i < N_PROMPT_TILES)
    def _():
        op_ref[...] = _ple_update(h_ref, p_ref, g_ref, wp_bf_ref, wgt_bf_ref)

    @pl.when(i >= N_PROMPT_TILES)
    def _():
        os_ref[...] = _ple_update(h_ref, p_ref, g_ref, wp_bf_ref, wgt_bf_ref)


def ple_add(h, p_all, g_all, w_ple_all, w_gate_all, layer, *, split=False):
    tm = TM_SPLIT if split else TM
    in_specs = [
        pl.BlockSpec((tm, D_MODEL), lambda i: (i, 0)),
        pl.BlockSpec((None, tm, PLE_DIM), lambda i: (layer, i, 0)),
        pl.BlockSpec((None, 1, D_MODEL), lambda i: (layer, 0, 0)),
        pl.BlockSpec((None, PLE_DIM, D_MODEL), lambda i: (layer, 0, 0), pipeline_mode=pl.Buffered(1)),
        pl.BlockSpec((None, D_MODEL, D_MODEL), lambda i: (layer, 0, 0), pipeline_mode=pl.Buffered(1)),
    ]
    if split:
        kern = _ple_split_kernel
        out_shape = (jax.ShapeDtypeStruct((N_PROMPT, D_MODEL), F32), jax.ShapeDtypeStruct((N_SAMPLE, D_MODEL), F32))
        out_specs = (pl.BlockSpec((tm, D_MODEL), lambda i: (jnp.minimum(i, N_PROMPT_TILES - 1), 0)),
                     pl.BlockSpec((tm, D_MODEL), lambda i: (0, 0)))
    else:
        kern = _ple_kernel
        out_shape = jax.ShapeDtypeStruct((N_TOK, D_MODEL), F32)
        out_specs = pl.BlockSpec((tm, D_MODEL), lambda i: (i, 0))
    return pl.pallas_call(
        kern,
        out_shape=out_shape,
        grid=(N_TOK // tm,),
        in_specs=in_specs,
        out_specs=out_specs,
        scratch_shapes=[pltpu.VMEM((PLE_DIM, D_MODEL), BF16), pltpu.VMEM((D_MODEL, D_MODEL), BF16)],
        compiler_params=_params("arbitrary"),
        name="ple_add",
    )(h, p_all, g_all.reshape(DEPTH, 1, D_MODEL), w_ple_all, w_gate_all)


def _ssd_kernel(zx_ref, dtr_ref, st0_ref, cv0_ref, cw_ref, cb_ref, dtb_ref, alog_ref, rep_ref, dsk_ref, gn_ref,
                y_ref, st_ref, cvt_ref, s_scr, xpad, act, pcum, dts):
    s = pl.program_id(0)
    is_sample = s >= N_PROMPT_CHUNKS
    pos = s % CHUNKS_PER_SEQ
    first = jnp.logical_or(is_sample, pos == 0)
    last = jnp.logical_or(is_sample, pos == CHUNKS_PER_SEQ - 1)

    @pl.when(jnp.logical_and(first, jnp.logical_not(is_sample)))
    def _():
        s_scr[...] = jnp.zeros_like(s_scr)
        xpad[0:8, :] = jnp.zeros((8, SSD_CONV_DIM), F32)

    @pl.when(is_sample)
    def _():
        for g in range(SSD_GROUPS):
            s_scr[g] = st0_ref[0, g].T
        xpad[0:8, :] = jnp.zeros((8, SSD_CONV_DIM), F32)
        xpad[5:8, :] = cv0_ref[0]

    xpad[8:72, :] = zx_ref[:, SSD_INNER:SSD_MAIN_WIDTH]
    conv = cb_ref[...] + cw_ref[3:4, :] * xpad[8:72, :]
    conv = conv + cw_ref[2:3, :] * xpad[7:71, :]
    conv = conv + cw_ref[1:2, :] * xpad[6:70, :]
    conv = conv + cw_ref[0:1, :] * xpad[5:69, :]
    act[...] = conv * jax.nn.sigmoid(conv)
    xpad[0:8, :] = xpad[64:72, :]

    x = dtr_ref[...] + dtb_ref[...]
    dt = jnp.maximum(x, 0.0) + jnp.log1p(jnp.exp(-jnp.abs(x)))
    run = dt * (-jnp.exp(alog_ref[...]))
    row = lax.broadcasted_iota(jnp.int32, (CHUNK, 128), 0)
    for k in (1, 2, 4, 8, 16, 32):
        run = run + jnp.where(row >= k, pltpu.roll(run, k, 0), 0.0)
    both = jnp.concatenate([dt, run], axis=0)
    hi = both.astype(BF16)
    r1 = both - hi.astype(F32)
    mid = r1.astype(BF16)
    lo = (r1 - mid.astype(F32)).astype(BF16)
    wide = _dot(jnp.concatenate([hi, mid, lo], axis=0), rep_ref[...])
    wide = (wide[0:2 * CHUNK] + wide[2 * CHUNK:4 * CHUNK]) + wide[4 * CHUNK:6 * CHUNK]
    dts[...] = wide[0:CHUNK]
    pcum[...] = wide[CHUNK:2 * CHUNK]

    gw = SSD_GROUP_WIDTH
    row_g = lax.broadcasted_iota(jnp.int32, (CHUNK, gw), 0)
    lane_g = lax.broadcasted_iota(jnp.int32, (CHUNK, gw), 1) % CHUNK
    diag = row_g == lane_g
    causal = row_g >= lane_g
    r4 = lax.broadcasted_iota(jnp.int32, (256, 256), 0) // CHUNK
    c4 = lax.broadcasted_iota(jnp.int32, (256, 256), 1) // SSD_HEAD_DIM
    head_diag = r4 == c4

    for g in range(SSD_GROUPS):
        cs = slice(g * gw, (g + 1) * gw)
        p_g = pcum[:, cs]
        dt_g = dts[:, cs]
        x_g = act[:, cs]
        b_g = act[:, SSD_INNER + g * SSD_STATE:SSD_INNER + (g + 1) * SSD_STATE].astype(BF16)
        c_g = act[:, SSD_INNER + SSD_GROUPS * SSD_STATE + g * SSD_STATE:
                  SSD_INNER + SSD_GROUPS * SSD_STATE + (g + 1) * SSD_STATE].astype(BF16)
        p_s = jnp.sum(jnp.where(diag, p_g, 0.0), axis=0, keepdims=True)
        dt_s = jnp.sum(jnp.where(diag, dt_g, 0.0), axis=0, keepdims=True)
        p_last = p_g[CHUNK - 1:CHUNK, :]
        cb = _dot_nt(c_g, jnp.concatenate([b_g] * 8, axis=0))
        m = (cb * jnp.exp(jnp.where(causal, p_g - p_s, NEG)) * dt_s).astype(BF16)
        x_bf = x_g.astype(BF16)
        halves = []
        for hh in range(2):
            hs = slice(hh * 256, (hh + 1) * 256)
            xh = x_bf[:, hs]
            x_bd = jnp.where(head_diag, jnp.concatenate([xh] * 4, axis=0), jnp.zeros((), BF16))
            halves.append(_dot(m[:, hs], x_bd))
        y = jnp.concatenate(halves, axis=1)
        st = s_scr[g]
        y = y + _dot(c_g, st.astype(BF16)) * jnp.exp(p_g)
        wx = (jnp.exp(p_last - p_g) * dt_g * x_g).astype(BF16)
        s_scr[g] = st * jnp.exp(p_last) + _dot_tn(b_g, wx)
        y = y + dsk_ref[:, cs] * x_g
        z = zx_ref[:, cs]
        y = y * (z * jax.nn.sigmoid(z))
        y_ref[:, cs] = _rms_rows(y, gn_ref[:, cs]).astype(BF16)

    @pl.when(last)
    def _():
        cvt_ref[0] = xpad[0:8, :]
        for g in range(SSD_GROUPS):
            st_ref[0, g] = s_scr[g].T


def _ssd_seq(s):
    return jnp.where(s < N_PROMPT_CHUNKS, s // CHUNKS_PER_SEQ, BATCH + s - N_PROMPT_CHUNKS)


def ssd_core(zx, dt_raw, state_all, conv_all, conv_w_all, layer, conv_b, dt_bias, a_log, d_skip, g_norm):
    n_seq = BATCH + DEC_BATCH
    pad_heads = lambda v: jnp.pad(v, (0, 128 - SSD_HEADS)).reshape(1, 128)
    samp = lambda s: jnp.maximum(s - N_PROMPT_CHUNKS, 0)
    row_spec = lambda w: pl.BlockSpec((1, w), lambda s: (0, 0))
    st_all = state_all.reshape(-1, DEC_BATCH, SSD_GROUPS, SSD_GROUP_WIDTH, SSD_STATE)
    head_of_channel = jnp.arange(SSD_INNER, dtype=jnp.int32) // SSD_HEAD_DIM
    rep = (jnp.arange(128, dtype=jnp.int32)[:, None] == head_of_channel[None, :]).astype(BF16)
    y, st, cvt = pl.pallas_call(
        _ssd_kernel,
        out_shape=(jax.ShapeDtypeStruct((N_TOK, SSD_INNER), BF16),
                   jax.ShapeDtypeStruct((n_seq, SSD_GROUPS, SSD_GROUP_WIDTH, SSD_STATE), F32),
                   jax.ShapeDtypeStruct((n_seq, 8, SSD_CONV_DIM), F32)),
        grid=(N_CHUNKS,),
        in_specs=[
            pl.BlockSpec((CHUNK, SSD_MAIN_WIDTH), lambda s: (s, 0)),
            pl.BlockSpec((CHUNK, 128), lambda s: (s, 0)),
            pl.BlockSpec((None, 1, SSD_GROUPS, SSD_GROUP_WIDTH, SSD_STATE), lambda s: (layer, samp(s), 0, 0, 0)),
            pl.BlockSpec((None, 1, 3, SSD_CONV_DIM), lambda s: (layer, samp(s), 0, 0)),
            pl.BlockSpec((None, 4, SSD_CONV_DIM), lambda s: (layer, 0, 0)),
            row_spec(SSD_CONV_DIM), row_spec(128), row_spec(128),
            pl.BlockSpec((128, SSD_INNER), lambda s: (0, 0)),
            row_spec(SSD_INNER), row_spec(SSD_INNER),
        ],
        out_specs=(pl.BlockSpec((CHUNK, SSD_INNER), lambda s: (s, 0)),
                   pl.BlockSpec((1, SSD_GROUPS, SSD_GROUP_WIDTH, SSD_STATE), lambda s: (_ssd_seq(s), 0, 0, 0)),
                   pl.BlockSpec((1, 8, SSD_CONV_DIM), lambda s: (_ssd_seq(s), 0, 0))),
        scratch_shapes=[
            pltpu.VMEM((SSD_GROUPS, SSD_STATE, SSD_GROUP_WIDTH), F32),
            pltpu.VMEM((CHUNK + 8, SSD_CONV_DIM), F32),
            pltpu.VMEM((CHUNK, SSD_CONV_DIM), F32),
            pltpu.VMEM((CHUNK, SSD_INNER), F32),
            pltpu.VMEM((CHUNK, SSD_INNER), F32),
        ],
        compiler_params=_params("arbitrary"),
        name="ssd_core",
    )(zx, dt_raw, st_all, conv_all, conv_w_all, conv_b.reshape(1, SSD_CONV_DIM), pad_heads(dt_bias),
      pad_heads(a_log), rep, jnp.repeat(d_skip, SSD_HEAD_DIM).reshape(1, SSD_INNER), g_norm.reshape(1, SSD_INNER))
    return y, st.reshape(n_seq, SSD_HEADS, SSD_HEAD_DIM, SSD_STATE), cvt[:, 5:8]


DIFF_SCALE = DIFF_HEAD_DIM ** -0.5
LOG2E = math.log2(math.e)


def _diff_lambda(lp_ref, lam_init):
    lp = lp_ref[...]
    a = jnp.sum(lp[0:1] * lp[1:2], axis=-1, keepdims=True)
    b = jnp.sum(lp[2:3] * lp[3:4], axis=-1, keepdims=True)
    return jnp.exp(a) - jnp.exp(b) + lam_init


def _alibi_slope(h):
    return jnp.exp2(-(jnp.zeros((1, 1), F32) + (h + 1).astype(F32)))


def _diff_prompt_kernel(q_ref, k_ref, v_ref, lp_ref, gs_ref, o_ref, *, lam_init, tq):
    h = pl.program_id(1)
    lam = _diff_lambda(lp_ref, lam_init)
    slope2 = _alibi_slope(h) * LOG2E
    c = DIFF_SCALE * LOG2E
    d = DIFF_HEAD_DIM
    t_i = lax.broadcasted_iota(jnp.int32, (tq, tq), 0)
    j_i = lax.broadcasted_iota(jnp.int32, (tq, tq), 1)
    bias_own = jnp.where(j_i // CHUNK <= t_i // CHUNK,
                         slope2 * (t_i - jnp.abs(t_i - j_i)).astype(F32), NEG)
    for qb in range(SEQ // tq):
        q0 = qb * tq
        rows = slice(q0, q0 + tq)
        if qb:
            bias_past = slope2 * (lax.broadcasted_iota(jnp.int32, (1, q0), 1) - q0).astype(F32)
        weights = []
        for m in range(2):
            cols = slice(m * d, (m + 1) * d)
            q = q_ref[rows, cols].astype(BF16)
            s_own = _dot_nt(q, k_ref[rows, cols].astype(BF16)) * c + bias_own
            mx = jnp.max(s_own, axis=-1, keepdims=True)
            if qb:
                s_past = _dot_nt(q, k_ref[0:q0, cols].astype(BF16)) * c + bias_past
                mx = jnp.maximum(mx, jnp.max(s_past, axis=-1, keepdims=True))
                e_past = jnp.exp2(s_past - mx)
            e_own = jnp.exp2(s_own - mx)
            den = jnp.sum(e_own, axis=-1, keepdims=True)
            if qb:
                den = den + jnp.sum(e_past, axis=-1, keepdims=True)
            weights.append((e_own, e_past if qb else None, 1.0 / den))
        (eo0, ep0, r0), (eo1, ep1, r1) = weights
        r1 = lam * r1
        o = _dot((eo0 * r0 - eo1 * r1).astype(BF16), v_ref[rows, :].astype(BF16))
        if qb:
            o = o + _dot((ep0 * r0 - ep1 * r1).astype(BF16), v_ref[0:q0, :].astype(BF16))
        o_ref[rows, :] = (_rms_rows(o, gs_ref[...]) * (1.0 - lam_init)).astype(BF16)


def diff_attn_prompt(qkv, lam_p, g_sub, lam_init, *, tq=256):
    w = 2 * DIFF_HEAD_DIM
    kern = functools.partial(_diff_prompt_kernel, lam_init=lam_init, tq=tq)
    return pl.pallas_call(
        kern,
        out_shape=jax.ShapeDtypeStruct((N_PROMPT, D_MODEL), BF16),
        grid=(BATCH, DIFF_HEADS),
        in_specs=[
            pl.BlockSpec((SEQ, w), lambda b, h: (b, h)),
            pl.BlockSpec((SEQ, w), lambda b, h: (b, DIFF_HEADS + h)),
            pl.BlockSpec((SEQ, w), lambda b, h: (b, 2 * DIFF_HEADS + h)),
            pl.BlockSpec((4, DIFF_HEAD_DIM), lambda b, h: (0, 0)),
            pl.BlockSpec((1, w), lambda b, h: (0, 0)),
        ],
        out_specs=pl.BlockSpec((SEQ, w), lambda b, h: (b, h)),
        compiler_params=_params("parallel", "parallel"),
        name="diff_attn_prompt",
    )(qkv, qkv, qkv, lam_p, g_sub.reshape(1, w))


def _diff_sample_kernel(q_ref, kn_ref, vn_ref, kc0_ref, kc1_ref, vc0_ref, vc1_ref, lp_ref, gs_ref, o_ref, vh_ref,
                        *, lam_init):
    kc_refs = (kc0_ref, kc1_ref)
    h = pl.program_id(1)
    lam = _diff_lambda(lp_ref, lam_init)
    slope = _alibi_slope(h)
    d = DIFF_HEAD_DIM
    t = DEC_SEQ
    tq_p = lax.broadcasted_iota(jnp.int32, (t, PAST_LEN), 0)
    kp_p = lax.broadcasted_iota(jnp.int32, (t, PAST_LEN), 1)
    bias_p = -slope * (PAST_LEN + tq_p - kp_p).astype(F32)
    tq_n = lax.broadcasted_iota(jnp.int32, (t, t), 0)
    kp_n = lax.broadcasted_iota(jnp.int32, (t, t), 1)
    bias_n = -slope * jnp.abs(tq_n - kp_n).astype(F32)
    pp, pn = [], []
    for m in range(2):
        cols = slice(m * d, (m + 1) * d)
        q = q_ref[:, cols].astype(BF16)
        kc = kc_refs[m].reshape(PAST_LEN, d)[...]
        sp = _dot_nt(q, kc.astype(BF16)) * DIFF_SCALE + bias_p
        sn = _dot_nt(q, kn_ref[:, cols].astype(BF16)) * DIFF_SCALE + bias_n
        mx = jnp.maximum(jnp.max(sp, axis=-1, keepdims=True), jnp.max(sn, axis=-1, keepdims=True))
        ep = jnp.exp(sp - mx)
        en = jnp.exp(sn - mx)
        den = jnp.sum(ep, axis=-1, keepdims=True) + jnp.sum(en, axis=-1, keepdims=True)
        pp.append(ep / den)
        pn.append(en / den)
    ap = (pp[0] - lam * pp[1]).astype(BF16)
    an = (pn[0] - lam * pn[1]).astype(BF16)
    for k in range(DIFF_HEADS):
        @pl.when(h == k)
        def _(k=k):
            for half, vc_ref in enumerate((vc0_ref, vc1_ref)):
                rows = vc_ref.reshape(PAST_LEN * DIFF_HEADS, d)
                vh_ref[:, half * d:(half + 1) * d] = rows[pl.ds(k, PAST_LEN, stride=DIFF_HEADS), :].astype(BF16)

    o = _dot(ap, vh_ref[...]) + _dot(an, vn_ref[...].astype(BF16))
    o_ref[...] = (_rms_rows(o, gs_ref[...]) * (1.0 - lam_init)).astype(BF16)


def diff_attn_sample(qkv, cache_k, cache_v, lam_p, g_sub, lam_init):
    w = 2 * DIFF_HEAD_DIM
    r0 = N_PROMPT // DEC_SEQ
    kern = functools.partial(_diff_sample_kernel, lam_init=lam_init)
    k_rows = cache_k.reshape(DEC_BATCH, PAST_LEN, 2 * DIFF_HEADS, 1, DIFF_HEAD_DIM)
    k_spec = lambda m: pl.BlockSpec((None, PAST_LEN, None, 1, DIFF_HEAD_DIM), lambda b, h: (b, 0, 2 * h + m, 0, 0))
    v_spec = lambda half: pl.BlockSpec((None, PAST_LEN, DIFF_HEADS, DIFF_HEAD_DIM), lambda b, h: (b, 0, 0, half))
    return pl.pallas_call(
        kern,
        out_shape=jax.ShapeDtypeStruct((N_SAMPLE, D_MODEL), BF16),
        grid=(DEC_BATCH, DIFF_HEADS),
        in_specs=[
            pl.BlockSpec((DEC_SEQ, w), lambda b, h: (r0 + b, h)),
            pl.BlockSpec((DEC_SEQ, w), lambda b, h: (r0 + b, DIFF_HEADS + h)),
            pl.BlockSpec((DEC_SEQ, w), lambda b, h: (r0 + b, 2 * DIFF_HEADS + h)),
            k_spec(0),
            k_spec(1),
            v_spec(0),
            v_spec(1),
            pl.BlockSpec((4, DIFF_HEAD_DIM), lambda b, h: (0, 0)),
            pl.BlockSpec((1, w), lambda b, h: (0, 0)),
        ],
        out_specs=pl.BlockSpec((DEC_SEQ, w), lambda b, h: (b, h)),
        scratch_shapes=[pltpu.VMEM((PAST_LEN, w), BF16)],
        compiler_params=_params("parallel", "arbitrary"),
        name="diff_attn_sample",
    )(qkv, qkv, qkv, k_rows, k_rows, cache_v, cache_v, lam_p, g_sub.reshape(1, w))


BAND_SCALE = BAND_HEAD_DIM ** -0.5
BAND_TQ = 256
BAND_E = 1024


def _band_bias(e_row, tq):
    eb = jnp.broadcast_to(e_row, (tq, BAND_E))
    return pltpu.roll(eb, BAND_E - (BAND_TQ - 1), 1, stride=1, stride_axis=0)


def _band_prompt_kernel(q_ref, k_ref, v_ref, e_ref, o_ref):
    tq = BAND_TQ
    win = BAND_WINDOW + tq
    bias = _band_bias(e_ref[0], tq)[:, :win]
    qc = lax.broadcasted_iota(jnp.int32, (tq, win), 0) // CHUNK
    jc = lax.broadcasted_iota(jnp.int32, (tq, win), 1) // CHUNK
    bias = jnp.where(jnp.logical_and(jc >= qc, jc <= qc + BAND_WINDOW // CHUNK), bias, NEG)
    for qb in range(SEQ // tq):
        q0 = qb * tq
        k0 = max(0, q0 - BAND_WINDOW)
        nk = q0 + tq - k0
        rows = slice(q0, q0 + tq)
        sc = _dot_nt(q_ref[rows, :].astype(BF16), k_ref[k0:k0 + nk, :].astype(BF16)) * BAND_SCALE
        sc = sc + bias[:, win - nk:]
        e = jnp.exp(sc - jnp.max(sc, axis=-1, keepdims=True))
        p = (e / jnp.sum(e, axis=-1, keepdims=True)).astype(BF16)
        o_ref[rows, :] = _dot(p, v_ref[k0:k0 + nk, :].astype(BF16)).astype(BF16)


def _band_e_rows(table):
    edge = jnp.broadcast_to(table[:, 2 * REL_CLIP:], (BAND_HEADS, BAND_E - (2 * REL_CLIP + 1)))
    return jnp.concatenate([edge, table[:, ::-1]], axis=1).reshape(BAND_HEADS, 1, BAND_E)


def band_attn_prompt(qkv, table):
    d = BAND_HEAD_DIM
    return pl.pallas_call(
        _band_prompt_kernel,
        out_shape=jax.ShapeDtypeStruct((N_PROMPT, D_MODEL), BF16),
        grid=(BATCH, BAND_HEADS),
        in_specs=[
            pl.BlockSpec((SEQ, d), lambda b, h: (b, h)),
            pl.BlockSpec((SEQ, d), lambda b, h: (b, BAND_HEADS + h)),
            pl.BlockSpec((SEQ, d), lambda b, h: (b, 2 * BAND_HEADS + h)),
            pl.BlockSpec((1, 1, BAND_E), lambda b, h: (h, 0, 0)),
        ],
        out_specs=pl.BlockSpec((SEQ, d), lambda b, h: (b, h)),
        compiler_params=_params("parallel", "parallel"),
        name="band_attn_prompt",
    )(qkv, qkv, qkv, _band_e_rows(table))


def _band_sample_kernel(q_ref, kn_ref, vn_ref, kc_ref, vc_ref, e_ref, o_ref):
    t = DEC_SEQ
    d = BAND_HEAD_DIM
    kc_rows = kc_ref.reshape(BAND_WINDOW * BAND_HEADS, d)
    vc_rows = vc_ref.reshape(BAND_WINDOW * BAND_HEADS, d)
    for h in range(BAND_HEADS):
        cols = slice(h * d, (h + 1) * d)
        bias = _band_bias(e_ref[h], t)
        q = q_ref[:, cols].astype(BF16)
        kc = kc_rows[pl.ds(h, BAND_WINDOW, stride=BAND_HEADS), :]
        vc = vc_rows[pl.ds(h, BAND_WINDOW, stride=BAND_HEADS), :]
        sp = _dot_nt(q, kc.astype(BF16)) * BAND_SCALE + bias[:, :BAND_WINDOW]
        sn = _dot_nt(q, kn_ref[:, cols].astype(BF16)) * BAND_SCALE + bias[:, BAND_WINDOW:BAND_WINDOW + t]
        mx = jnp.maximum(jnp.max(sp, axis=-1, keepdims=True), jnp.max(sn, axis=-1, keepdims=True))
        ep = jnp.exp(sp - mx)
        en = jnp.exp(sn - mx)
        den = jnp.sum(ep, axis=-1, keepdims=True) + jnp.sum(en, axis=-1, keepdims=True)
        o = (_dot((ep / den).astype(BF16), vc.astype(BF16))
             + _dot((en / den).astype(BF16), vn_ref[:, cols].astype(BF16)))
        o_ref[:, cols] = o.astype(BF16)


def band_attn_sample(qkv, cache_k, cache_v, table):
    d = BAND_HEAD_DIM
    r0 = N_PROMPT // DEC_SEQ
    rows_view = (DEC_BATCH, BAND_WINDOW, BAND_HEADS, 1, d)
    cache_spec = pl.BlockSpec((None, BAND_WINDOW, BAND_HEADS, 1, d), lambda b: (b, 0, 0, 0, 0))
    return pl.pallas_call(
        _band_sample_kernel,
        out_shape=jax.ShapeDtypeStruct((N_SAMPLE, D_MODEL), BF16),
        grid=(DEC_BATCH,),
        in_specs=[
            pl.BlockSpec((DEC_SEQ, D_MODEL), lambda b: (r0 + b, 0)),
            pl.BlockSpec((DEC_SEQ, D_MODEL), lambda b: (r0 + b, 1)),
            pl.BlockSpec((DEC_SEQ, D_MODEL), lambda b: (r0 + b, 2)),
            cache_spec,
            cache_spec,
            pl.BlockSpec((BAND_HEADS, 1, BAND_E), lambda b: (0, 0, 0)),
        ],
        out_specs=pl.BlockSpec((DEC_SEQ, D_MODEL), lambda b: (b, 0)),
        compiler_params=_params("parallel"),
        name="band_attn_sample",
    )(qkv, qkv, qkv, cache_k.reshape(rows_view), cache_v.reshape(rows_view), _band_e_rows(table))


def kernel(x_prompt, x_sample, p_prompt, p_sample, state_ssm, state_conv, cache_k_diff, cache_v_diff,
           cache_k_band, cache_v_band, g_ffn, w_ffn_gate, w_ffn_up, w_ffn_down, g_mix,
           ssd_w_in, ssd_conv_w, ssd_conv_b, ssd_dt_bias, ssd_a_log, ssd_d, ssd_g_norm, ssd_w_out,
           diff_w_in, diff_g_q, diff_g_k, diff_lambda, diff_g_sub, diff_w_out,
           band_w_in, band_g_q, band_g_k, band_rel_bias, band_w_out,
           g_ple, w_ple, w_ple_gate):
    h = jnp.concatenate([x_prompt.reshape(N_PROMPT, D_MODEL), x_sample.reshape(N_SAMPLE, D_MODEL)], axis=0)
    p_all = jnp.concatenate([p_prompt.reshape(DEPTH, N_PROMPT, PLE_DIM),
                             p_sample.reshape(DEPTH, N_SAMPLE, PLE_DIM)], axis=1)
    bf = lambda w: w.astype(BF16)
    wg_all, wu_all, wd_all = w_ffn_gate, w_ffn_up, w_ffn_down
    w_ple_all, w_gate_all = w_ple, w_ple_gate
    ssd_w_main = bf(ssd_w_in)
    ssd_w_dt = bf(jnp.pad(ssd_w_in[:, :, SSD_MAIN_WIDTH:], ((0, 0), (0, 0), (0, 128 - SSD_HEADS))))
    ssd_w_out_all, diff_w_in_all, diff_w_out_all = ssd_w_out, diff_w_in, diff_w_out
    band_w_in_all, band_w_out_all = band_w_in, band_w_out

    def prompt_rows(a):
        return a[:N_PROMPT].reshape(BATCH, SEQ, -1)

    def sample_rows(a):
        return a[N_PROMPT:].reshape(DEC_BATCH, DEC_SEQ, -1)

    ssm_p, conv_p, ssm_s, conv_s = [], [], [], []
    kd_p = vd_p = kb_p = vb_p = kd_s = vd_s = kb_s = vb_s = None
    for i in range(DEPTH):
        kind, j = i % 3, i // 3
        h = ffn_half(h, g_ffn, wg_all, wu_all, wd_all, i, 0)
        if kind == 0:
            zx, dt_raw = inproj(h, g_mix[i], ssd_w_main, j, n_cols=SSD_MAIN_WIDTH, w_extra=ssd_w_dt)
            y, st, cvt = ssd_core(zx, dt_raw, state_ssm, state_conv, ssd_conv_w, j, ssd_conv_b[j], ssd_dt_bias[j],
                                  ssd_a_log[j], ssd_d[j], ssd_g_norm[j])
            h = outproj(h, y, ssd_w_out_all, j)
            ssm_p.append(st[:BATCH])
            ssm_s.append(st[BATCH:])
            conv_p.append(cvt[:BATCH])
            conv_s.append(cvt[BATCH:])
        elif kind == 1:
            lam_init = 0.8 - 0.6 * math.exp(-0.3 * i)
            gains = jnp.concatenate([jnp.tile(diff_g_q[j].reshape(-1), DIFF_HEADS),
                                     jnp.tile(diff_g_k[j].reshape(-1), DIFF_HEADS),
                                     jnp.ones((D_MODEL,), F32)]).reshape(1, 3 * D_MODEL)
            qkv = inproj(h, g_mix[i], diff_w_in_all, j, gains, norm_cols=2 * D_MODEL)
            o_p = diff_attn_prompt(qkv, diff_lambda[j], diff_g_sub[j], lam_init)
            o_s = diff_attn_sample(qkv, cache_k_diff[j], cache_v_diff[j], diff_lambda[j], diff_g_sub[j], lam_init)
            h = outproj(h, (o_p, o_s), diff_w_out_all, j)
            k_new, v_new = qkv[:, D_MODEL:2 * D_MODEL], qkv[:, 2 * D_MODEL:]
            kd_p = prompt_rows(k_new).reshape(1, BATCH, SEQ, DIFF_HEADS, 2, DIFF_HEAD_DIM)
            vd_p = prompt_rows(v_new).reshape(1, BATCH, SEQ, DIFF_HEADS, 2 * DIFF_HEAD_DIM)
            kd_s = sample_rows(k_new).reshape(1, DEC_BATCH, DEC_SEQ, DIFF_HEADS, 2, DIFF_HEAD_DIM)
            vd_s = sample_rows(v_new).reshape(1, DEC_BATCH, DEC_SEQ, DIFF_HEADS, 2 * DIFF_HEAD_DIM)
        else:
            gains = jnp.concatenate([jnp.tile(band_g_q[j], BAND_HEADS), jnp.tile(band_g_k[j], BAND_HEADS),
                                     jnp.ones((D_MODEL,), F32)]).reshape(1, 3 * D_MODEL)
            qkv = inproj(h, g_mix[i], band_w_in_all, j, gains, norm_cols=2 * D_MODEL)
            o_p = band_attn_prompt(qkv, band_rel_bias[j])
            o_s = band_attn_sample(qkv, cache_k_band[j], cache_v_band[j], band_rel_bias[j])
            h = outproj(h, (o_p, o_s), band_w_out_all, j)
            tail = lax.slice(qkv[:N_PROMPT].reshape(BATCH, SEQ, 3 * D_MODEL),
                             (0, SEQ - BAND_WINDOW, D_MODEL), (BATCH, SEQ, 3 * D_MODEL))
            kb_p = tail[:, :, :D_MODEL].reshape(1, BATCH, BAND_WINDOW, BAND_HEADS, BAND_HEAD_DIM)
            vb_p = tail[:, :, D_MODEL:].reshape(1, BATCH, BAND_WINDOW, BAND_HEADS, BAND_HEAD_DIM)
            new_s = qkv[N_PROMPT:, D_MODEL:]
            kb_s = new_s[:, :D_MODEL].reshape(1, DEC_BATCH, DEC_SEQ, BAND_HEADS, BAND_HEAD_DIM)
            vb_s = new_s[:, D_MODEL:].reshape(1, DEC_BATCH, DEC_SEQ, BAND_HEADS, BAND_HEAD_DIM)
        h = ffn_half(h, g_ffn, wg_all, wu_all, wd_all, i, 1)
        h = ple_add(h, p_all, g_ple, w_ple_all, w_gate_all, i, split=(i == DEPTH - 1))
    h_p, h_s = h
    return (h_p.reshape(BATCH, SEQ, D_MODEL), h_s.reshape(DEC_BATCH, DEC_SEQ, D_MODEL),
            jnp.stack(ssm_p), jnp.stack(conv_p), kd_p, vd_p, kb_p, vb_p,
            jnp.stack(ssm_s), jnp.stack(conv_s), kd_s, vd_s, kb_s, vb_s)
```

```python
import functools
import math

import jax
import jax.numpy as jnp
from jax import lax
from jax.experimental import pallas as pl
from jax.experimental.pallas import tpu as pltpu

F32 = jnp.float32
BF16 = jnp.bfloat16

D_MODEL = 2048
BATCH = 4
SEQ = 2048
DEPTH = 4
DEC_BATCH = 8
DEC_SEQ = 64
PAST_LEN = 2048
CHUNK = 64
NORM_EPS = 1e-6
FFN_DIM = 5632
PLE_DIM = 256
SSD_INNER = 4096
SSD_HEADS = 64
SSD_HEAD_DIM = 64
SSD_GROUPS = 8
SSD_STATE = 128
SSD_GROUP_WIDTH = SSD_INNER // SSD_GROUPS
SSD_CONV_DIM = SSD_INNER + 2 * SSD_GROUPS * SSD_STATE
SSD_MAIN_WIDTH = SSD_INNER + SSD_CONV_DIM
SSD_EXT_WIDTH = SSD_MAIN_WIDTH + SSD_INNER
DIFF_HEADS = 8
DIFF_HEAD_DIM = 128
BAND_HEADS = 16
BAND_HEAD_DIM = 128
BAND_WINDOW = 512
REL_CLIP = 256

N_PROMPT = BATCH * SEQ
N_SAMPLE = DEC_BATCH * DEC_SEQ
N_TOK = N_PROMPT + N_SAMPLE
CHUNKS_PER_SEQ = SEQ // CHUNK
N_PROMPT_CHUNKS = N_PROMPT // CHUNK
N_CHUNKS = N_TOK // CHUNK

TM_WIDE = 1088
TM = 544
TM_SPLIT = 512
N_PROMPT_TILES = N_PROMPT // TM_SPLIT
VMEM_LIMIT = 56 * 1024 * 1024
NEG = -1e30


def _params(*sem, vmem=VMEM_LIMIT):
    return pltpu.CompilerParams(dimension_semantics=sem, vmem_limit_bytes=vmem)


def _rms_rows(x, g):
    ms = jnp.mean(x * x, axis=-1, keepdims=True)
    return x * lax.rsqrt(ms + NORM_EPS) * g


def _dot(a, b):
    return jnp.dot(a, b, preferred_element_type=F32)


def _dot_nt(a, b):
    return lax.dot_general(a, b, (((1,), (1,)), ((), ())), preferred_element_type=F32)


def _dot_tn(a, b):
    return lax.dot_general(a, b, (((0,), (0,)), ((), ())), preferred_element_type=F32)


def _ffn_step(first, h_ref, g_ref, wg_ref, wu_ref, wd_ref, o_ref, xn_ref):
    @pl.when(first)
    def _():
        x = h_ref[...]
        xn_ref[...] = _rms_rows(x, g_ref[...]).astype(BF16)
        o_ref[...] = x

    xn = xn_ref[...]
    gate = _dot(xn, wg_ref[...])
    up = _dot(xn, wu_ref[...])
    act = (0.5 * (gate * jax.nn.sigmoid(gate)) * up).astype(BF16)
    o_ref[...] += _dot(act, wd_ref[...])


def _ffn_first_kernel(h_ref, g_ref, wg_ref, wu_ref, wd_ref, o_ref, wg_bf_ref, wu_bf_ref, wd_bf_ref, xn_ref):
    wg_bf_ref[...] = wg_ref[...].astype(BF16)
    wu_bf_ref[...] = wu_ref[...].astype(BF16)
    wd_bf_ref[...] = wd_ref[...].astype(BF16)
    _ffn_step(pl.program_id(0) == 0, h_ref, g_ref, wg_bf_ref, wu_bf_ref, wd_bf_ref, o_ref, xn_ref)


def _ffn_rest_kernel(h_ref, g_ref, wg_ref, wu_ref, wd_ref, partial_ref, o_ref, xn_ref):
    del partial_ref
    _ffn_step(pl.program_id(1) == 0, h_ref, g_ref, wg_ref, wu_ref, wd_ref, o_ref, xn_ref)


def ffn_half(h, g_all, wg_all, wu_all, wd_all, layer, half, *, tf=512, tf_first=256):
    tm = TM
    tm_first = TM_WIDE
    skip = tm_first // tm
    g4 = g_all.reshape(DEPTH, 2, 1, D_MODEL)
    g_spec1 = pl.BlockSpec((None, None, 1, D_MODEL), lambda f: (layer, half, 0, 0))
    partial, wg_bf, wu_bf, wd_bf = pl.pallas_call(
        _ffn_first_kernel,
        out_shape=(jax.ShapeDtypeStruct((N_TOK, D_MODEL), F32),
                   jax.ShapeDtypeStruct((D_MODEL, FFN_DIM), BF16),
                   jax.ShapeDtypeStruct((D_MODEL, FFN_DIM), BF16),
                   jax.ShapeDtypeStruct((FFN_DIM, D_MODEL), BF16)),
        grid=(FFN_DIM // tf_first,),
        in_specs=[
            pl.BlockSpec((tm_first, D_MODEL), lambda f: (0, 0), pipeline_mode=pl.Buffered(1)),
            g_spec1,
            pl.BlockSpec((None, None, D_MODEL, tf_first), lambda f: (layer, half, 0, f)),
            pl.BlockSpec((None, None, D_MODEL, tf_first), lambda f: (layer, half, 0, f)),
            pl.BlockSpec((None, None, tf_first, D_MODEL), lambda f: (layer, half, f, 0)),
        ],
        out_specs=(pl.BlockSpec((tm_first, D_MODEL), lambda f: (0, 0)),
                   pl.BlockSpec((D_MODEL, tf_first), lambda f: (0, f)),
                   pl.BlockSpec((D_MODEL, tf_first), lambda f: (0, f)),
                   pl.BlockSpec((tf_first, D_MODEL), lambda f: (f, 0))),
        scratch_shapes=[pltpu.VMEM((tm_first, D_MODEL), BF16)],
        compiler_params=_params("arbitrary"),
        name="ffn_first",
    )(h, g4, wg_all, wu_all, wd_all)
    return pl.pallas_call(
        _ffn_rest_kernel,
        out_shape=jax.ShapeDtypeStruct((N_TOK, D_MODEL), F32),
        grid=(N_TOK // tm - skip, FFN_DIM // tf),
        in_specs=[
            pl.BlockSpec((tm, D_MODEL), lambda i, f: (i + skip, 0)),
            pl.BlockSpec((None, None, 1, D_MODEL), lambda i, f: (layer, half, 0, 0)),
            pl.BlockSpec((D_MODEL, tf), lambda i, f: (0, f)),
            pl.BlockSpec((D_MODEL, tf), lambda i, f: (0, f)),
            pl.BlockSpec((tf, D_MODEL), lambda i, f: (f, 0)),
            pl.BlockSpec(memory_space=pl.ANY),
        ],
        out_specs=pl.BlockSpec((tm, D_MODEL), lambda i, f: (i + skip, 0)),
        scratch_shapes=[pltpu.VMEM((tm, D_MODEL), BF16)],
        input_output_aliases={5: 0},
        compiler_params=_params("parallel", "arbitrary"),
        name="ffn_rest",
    )(h, g4, wg_bf, wu_bf, wd_bf, partial)


def _inproj_kernel(h_ref, g_ref, w_ref, hg_ref, *rest, norm_tiles, tn, has_extra, first, n_alias=0):
    rest = list(rest)
    wx_ref = rest.pop(0) if has_extra else None
    del rest[:n_alias]
    o_ref = rest.pop(0)
    ox_ref = rest.pop(0) if has_extra else None
    wbf_ref = rest.pop(0) if first else None
    (xn_ref,) = rest
    j = pl.program_id(0 if first else 1)

    @pl.when(j == 0)
    def _():
        xn = _rms_rows(h_ref[...], g_ref[...]).astype(BF16)
        xn_ref[...] = xn
        if has_extra:
            ox_ref[...] = _dot(xn, wx_ref[...])

    if first:
        w = w_ref[...].astype(BF16)
        wbf_ref[...] = w
    else:
        w = w_ref[...]
    y = _dot(xn_ref[...], w)
    if norm_tiles == 0:
        o_ref[...] = y
        return

    @pl.when(j >= norm_tiles)
    def _():
        o_ref[...] = y

    @pl.when(j < norm_tiles)
    def _():
        for c in range(tn // 128):
            sl = slice(c * 128, (c + 1) * 128)
            o_ref[:, sl] = _rms_rows(y[:, sl], hg_ref[:, sl])


def inproj(h, g, w_all, layer, head_gain=None, *, n_cols=None, norm_cols=0, w_extra=None, tn=1024, tn_first=512):
    tm = TM_WIDE
    n = w_all.shape[2] if n_cols is None else n_cols
    if head_gain is None:
        head_gain = jnp.ones((1, n), F32)
    has_extra = w_extra is not None
    g2 = g.reshape(1, D_MODEL)
    nx = w_extra.shape[2] if has_extra else 0
    out_shapes = [jax.ShapeDtypeStruct((N_TOK, n), F32)]
    if has_extra:
        out_shapes.append(jax.ShapeDtypeStruct((N_TOK, nx), F32))

    partials = []
    if w_all.dtype == BF16:
        w_bf = w_all
        w_spec = pl.BlockSpec((None, D_MODEL, tn), lambda i, j: (layer, 0, j))
    else:
        kern = functools.partial(_inproj_kernel, norm_tiles=norm_cols // tn_first, tn=tn_first, has_extra=has_extra,
                                 first=True)
        in_specs = [
            pl.BlockSpec((tm, D_MODEL), lambda j: (0, 0)),
            pl.BlockSpec((1, D_MODEL), lambda j: (0, 0)),
            pl.BlockSpec((None, D_MODEL, tn_first), lambda j: (layer, 0, j)),
            pl.BlockSpec((1, tn_first), lambda j: (0, j)),
        ]
        args = [h, g2, w_all, head_gain]
        out_specs = [pl.BlockSpec((tm, tn_first), lambda j: (0, j))]
        if has_extra:
            in_specs.append(pl.BlockSpec((None, D_MODEL, nx), lambda j: (layer, 0, 0)))
            args.append(w_extra)
            out_specs.append(pl.BlockSpec((tm, nx), lambda j: (0, 0)))
        out_specs.append(pl.BlockSpec((D_MODEL, tn_first), lambda j: (0, j)))
        *partials, w_bf = pl.pallas_call(
            kern,
            out_shape=tuple(out_shapes) + (jax.ShapeDtypeStruct((D_MODEL, n), BF16),),
            grid=(n // tn_first,),
            in_specs=in_specs,
            out_specs=tuple(out_specs),
            scratch_shapes=[pltpu.VMEM((tm, D_MODEL), BF16)],
            compiler_params=_params("arbitrary"),
            name="inproj_first",
        )(*args)
        w_spec = pl.BlockSpec((D_MODEL, tn), lambda i, j: (0, j))

    r0 = len(partials) and 1
    kern = functools.partial(_inproj_kernel, norm_tiles=norm_cols // tn, tn=tn, has_extra=has_extra, first=False,
                             n_alias=len(partials))
    in_specs = [
        pl.BlockSpec((tm, D_MODEL), lambda i, j: (i + r0, 0)),
        pl.BlockSpec((1, D_MODEL), lambda i, j: (0, 0)),
        w_spec,
        pl.BlockSpec((1, tn), lambda i, j: (0, j)),
    ]
    args = [h, g2, w_bf, head_gain]
    out_specs = [pl.BlockSpec((tm, tn), lambda i, j: (i + r0, j))]
    if has_extra:
        in_specs.append(pl.BlockSpec((None, D_MODEL, nx), lambda i, j: (layer, 0, 0)))
        args.append(w_extra)
        out_specs.append(pl.BlockSpec((tm, nx), lambda i, j: (i + r0, 0)))
    n_in = len(in_specs)
    in_specs += [pl.BlockSpec(memory_space=pl.ANY)] * len(partials)
    args += partials
    outs = pl.pallas_call(
        kern,
        out_shape=tuple(out_shapes),
        grid=(N_TOK // tm - r0, n // tn),
        in_specs=in_specs,
        out_specs=tuple(out_specs),
        scratch_shapes=[pltpu.VMEM((tm, D_MODEL), BF16)],
        input_output_aliases={n_in + k: k for k in range(len(partials))},
        compiler_params=_params("parallel", "arbitrary"),
        name="inproj_main",
    )(*args)
    return outs if has_extra else outs[0]


def _cast_once(first, w_ref, wbf_ref):
    @pl.when(first)
    def _():
        wbf_ref[...] = w_ref[...].astype(BF16)


def _outproj_kernel(a_ref, w_ref, h_ref, o_ref, *scratch):
    wbf_ref = w_ref
    if scratch:
        (wbf_ref,) = scratch
        _cast_once(pl.program_id(1) == 0, w_ref, wbf_ref)
    o_ref[...] = h_ref[...] + _dot(a_ref[...], wbf_ref[...])


def _outproj_split_kernel(ap_ref, as_ref, w_ref, h_ref, o_ref, wbf_ref):
    i = pl.program_id(1)
    _cast_once(i == 0, w_ref, wbf_ref)

    @pl.when(i < N_PROMPT_TILES)
    def _():
        o_ref[...] = h_ref[...] + _dot(ap_ref[...], wbf_ref[...])

    @pl.when(i >= N_PROMPT_TILES)
    def _():
        o_ref[...] = h_ref[...] + _dot(as_ref[...], wbf_ref[...])


def outproj(h, a, w_all, layer):
    tm = TM_SPLIT
    k = w_all.shape[1]
    tn = 1024
    scratch = [] if w_all.dtype == BF16 else [pltpu.VMEM((k, tn), BF16)]
    w_spec = pl.BlockSpec((None, k, tn), lambda j, i: (layer, 0, j))
    h_spec = pl.BlockSpec((tm, tn), lambda j, i: (i, j))
    if isinstance(a, tuple):
        kern = _outproj_split_kernel
        a_specs = [pl.BlockSpec((tm, k), lambda j, i: (jnp.minimum(i, N_PROMPT_TILES - 1), 0)),
                   pl.BlockSpec((tm, k), lambda j, i: (0, 0))]
        args = list(a)
    else:
        kern = _outproj_kernel
        a_specs = [pl.BlockSpec((tm, k), lambda j, i: (i, 0))]
        args = [a]
    return pl.pallas_call(
        kern,
        out_shape=jax.ShapeDtypeStruct((N_TOK, D_MODEL), F32),
        grid=(D_MODEL // tn, N_TOK // tm),
        in_specs=a_specs + [w_spec, h_spec],
        out_specs=h_spec,
        scratch_shapes=scratch,
        compiler_params=_params("parallel", "arbitrary"),
        name="outproj",
    )(*args, w_all, h)


def _ple_update(h_ref, p_ref, g_ref, wp_ref, wgt_ref):
    x = h_ref[...]
    xn = _rms_rows(x, g_ref[...]).astype(BF16)
    gate = jax.nn.sigmoid(_dot(xn, wgt_ref[...]))
    emb = _dot(p_ref[...].astype(BF16), wp_ref[...])
    return x + emb * gate


def _ple_kernel(h_ref, p_ref, g_ref, wp_ref, wgt_ref, o_ref, wp_bf_ref, wgt_bf_ref):
    first = pl.program_id(0) == 0
    _cast_once(first, wp_ref, wp_bf_ref)
    _cast_once(first, wgt_ref, wgt_bf_ref)
    o_ref[...] = _ple_update(h_ref, p_ref, g_ref, wp_bf_ref, wgt_bf_ref)


def _ple_split_kernel(h_ref, p_ref, g_ref, wp_ref, wgt_ref, op_ref, os_ref, wp_bf_ref, wgt_bf_ref):
    i = pl.program_id(0)
    _cast_once(i == 0, wp_ref, wp_bf_ref)
    _cast_once(i == 0, wgt_ref, wgt_bf_ref)

    @pl.when(i < N_PROMPT_TILES)
    def _():
        op_ref[...] = _ple_update(h_ref, p_ref, g_ref, wp_bf_ref, wgt_bf_ref)

    @pl.when(i >= N_PROMPT_TILES)
    def _():
        os_ref[...] = _ple_update(h_ref, p_ref, g_ref, wp_bf_ref, wgt_bf_ref)


def ple_add(h, p_all, g_all, w_ple_all, w_gate_all, layer, *, split=False):
    tm = TM_SPLIT if split else TM
    in_specs = [
        pl.BlockSpec((tm, D_MODEL), lambda i: (i, 0)),
        pl.BlockSpec((None, tm, PLE_DIM), lambda i: (layer, i, 0)),
        pl.BlockSpec((None, 1, D_MODEL), lambda i: (layer, 0, 0)),
        pl.BlockSpec((None, PLE_DIM, D_MODEL), lambda i: (layer, 0, 0), pipeline_mode=pl.Buffered(1)),
        pl.BlockSpec((None, D_MODEL, D_MODEL), lambda i: (layer, 0, 0), pipeline_mode=pl.Buffered(1)),
    ]
    if split:
        kern = _ple_split_kernel
        out_shape = (jax.ShapeDtypeStruct((N_PROMPT, D_MODEL), F32), jax.ShapeDtypeStruct((N_SAMPLE, D_MODEL), F32))
        out_specs = (pl.BlockSpec((tm, D_MODEL), lambda i: (jnp.minimum(i, N_PROMPT_TILES - 1), 0)),
                     pl.BlockSpec((tm, D_MODEL), lambda i: (0, 0)))
    else:
        kern = _ple_kernel
        out_shape = jax.ShapeDtypeStruct((N_TOK, D_MODEL), F32)
        out_specs = pl.BlockSpec((tm, D_MODEL), lambda i: (i, 0))
    return pl.pallas_call(
        kern,
        out_shape=out_shape,
        grid=(N_TOK // tm,),
        in_specs=in_specs,
        out_specs=out_specs,
        scratch_shapes=[pltpu.VMEM((PLE_DIM, D_MODEL), BF16), pltpu.VMEM((D_MODEL, D_MODEL), BF16)],
        compiler_params=_params("arbitrary"),
        name="ple_add",
    )(h, p_all, g_all.reshape(DEPTH, 1, D_MODEL), w_ple_all, w_gate_all)


def _ssd_kernel(zx_ref, dtr_ref, st0_ref, cv0_ref, cw_ref, cb_ref, dtb_ref, alog_ref, rep_ref, dsk_ref, gn_ref,
                y_ref, st_ref, cvt_ref, s_scr, xpad, act, pcum, dts):
    s = pl.program_id(0)
    is_sample = s >= N_PROMPT_CHUNKS
    pos = s % CHUNKS_PER_SEQ
    first = jnp.logical_or(is_sample, pos == 0)
    last = jnp.logical_or(is_sample, pos == CHUNKS_PER_SEQ - 1)

    @pl.when(jnp.logical_and(first, jnp.logical_not(is_sample)))
    def _():
        s_scr[...] = jnp.zeros_like(s_scr)
        xpad[0:8, :] = jnp.zeros((8, SSD_CONV_DIM), F32)

    @pl.when(is_sample)
    def _():
        for g in range(SSD_GROUPS):
            s_scr[g] = st0_ref[0, g].T
        xpad[0:8, :] = jnp.zeros((8, SSD_CONV_DIM), F32)
        xpad[5:8, :] = cv0_ref[0]

    xpad[8:72, :] = zx_ref[:, SSD_INNER:SSD_MAIN_WIDTH]
    conv = cb_ref[...] + cw_ref[3:4, :] * xpad[8:72, :]
    conv = conv + cw_ref[2:3, :] * xpad[7:71, :]
    conv = conv + cw_ref[1:2, :] * xpad[6:70, :]
    conv = conv + cw_ref[0:1, :] * xpad[5:69, :]
    act[...] = conv * jax.nn.sigmoid(conv)
    xpad[0:8, :] = xpad[64:72, :]

    x = dtr_ref[...] + dtb_ref[...]
    dt = jnp.maximum(x, 0.0) + jnp.log1p(jnp.exp(-jnp.abs(x)))
    run = dt * (-jnp.exp(alog_ref[...]))
    row = lax.broadcasted_iota(jnp.int32, (CHUNK, 128), 0)
    for k in (1, 2, 4, 8, 16, 32):
        run = run + jnp.where(row >= k, pltpu.roll(run, k, 0), 0.0)
    both = jnp.concatenate([dt, run], axis=0)
    hi = both.astype(BF16)
    r1 = both - hi.astype(F32)
    mid = r1.astype(BF16)
    lo = (r1 - mid.astype(F32)).astype(BF16)
    wide = _dot(jnp.concatenate([hi, mid, lo], axis=0), rep_ref[...])
    wide = (wide[0:2 * CHUNK] + wide[2 * CHUNK:4 * CHUNK]) + wide[4 * CHUNK:6 * CHUNK]
    dts[...] = wide[0:CHUNK]
    pcum[...] = wide[CHUNK:2 * CHUNK]

    gw = SSD_GROUP_WIDTH
    row_g = lax.broadcasted_iota(jnp.int32, (CHUNK, gw), 0)
    lane_g = lax.broadcasted_iota(jnp.int32, (CHUNK, gw), 1) % CHUNK
    diag = row_g == lane_g
    causal = row_g >= lane_g
    r4 = lax.broadcasted_iota(jnp.int32, (256, 256), 0) // CHUNK
    c4 = lax.broadcasted_iota(jnp.int32, (256, 256), 1) // SSD_HEAD_DIM
    head_diag = r4 == c4

    for g in range(SSD_GROUPS):
        cs = slice(g * gw, (g + 1) * gw)
        p_g = pcum[:, cs]
        dt_g = dts[:, cs]
        x_g = act[:, cs]
        b_g = act[:, SSD_INNER + g * SSD_STATE:SSD_INNER + (g + 1) * SSD_STATE].astype(BF16)
        c_g = act[:, SSD_INNER + SSD_GROUPS * SSD_STATE + g * SSD_STATE:
                  SSD_INNER + SSD_GROUPS * SSD_STATE + (g + 1) * SSD_STATE].astype(BF16)
        p_s = jnp.sum(jnp.where(diag, p_g, 0.0), axis=0, keepdims=True)
        dt_s = jnp.sum(jnp.where(diag, dt_g, 0.0), axis=0, keepdims=True)
        p_last = p_g[CHUNK - 1:CHUNK, :]
        cb = _dot_nt(c_g, jnp.concatenate([b_g] * 8, axis=0))
        m = (cb * jnp.exp(jnp.where(causal, p_g - p_s, NEG)) * dt_s).astype(BF16)
        x_bf = x_g.astype(BF16)
        halves = []
        for hh in range(2):
            hs = slice(hh * 256, (hh + 1) * 256)
            xh = x_bf[:, hs]
            x_bd = jnp.where(head_diag, jnp.concatenate([xh] * 4, axis=0), jnp.zeros((), BF16))
            halves.append(_dot(m[:, hs], x_bd))
        y = jnp.concatenate(halves, axis=1)
        st = s_scr[g]
        y = y + _dot(c_g, st.astype(BF16)) * jnp.exp(p_g)
        wx = (jnp.exp(p_last - p_g) * dt_g * x_g).astype(BF16)
        s_scr[g] = st * jnp.exp(p_last) + _dot_tn(b_g, wx)
        y = y + dsk_ref[:, cs] * x_g
        z = zx_ref[:, cs]
        y = y * (z * jax.nn.sigmoid(z))
        y_ref[:, cs] = _rms_rows(y, gn_ref[:, cs]).astype(BF16)

    @pl.when(last)
    def _():
        cvt_ref[0] = xpad[0:8, :]
        for g in range(SSD_GROUPS):
            st_ref[0, g] = s_scr[g].T


def _ssd_seq(s):
    return jnp.where(s < N_PROMPT_CHUNKS, s // CHUNKS_PER_SEQ, BATCH + s - N_PROMPT_CHUNKS)


def ssd_core(zx, dt_raw, state_all, conv_all, conv_w_all, layer, conv_b, dt_bias, a_log, d_skip, g_norm):
    n_seq = BATCH + DEC_BATCH
    pad_heads = lambda v: jnp.pad(v, (0, 128 - SSD_HEADS)).reshape(1, 128)
    samp = lambda s: jnp.maximum(s - N_PROMPT_CHUNKS, 0)
    row_spec = lambda w: pl.BlockSpec((1, w), lambda s: (0, 0))
    st_all = state_all.reshape(-1, DEC_BATCH, SSD_GROUPS, SSD_GROUP_WIDTH, SSD_STATE)
    head_of_channel = jnp.arange(SSD_INNER, dtype=jnp.int32) // SSD_HEAD_DIM
    rep = (jnp.arange(128, dtype=jnp.int32)[:, None] == head_of_channel[None, :]).astype(BF16)
    y, st, cvt = pl.pallas_call(
        _ssd_kernel,
        out_shape=(jax.ShapeDtypeStruct((N_TOK, SSD_INNER), BF16),
                   jax.ShapeDtypeStruct((n_seq, SSD_GROUPS, SSD_GROUP_WIDTH, SSD_STATE), F32),
                   jax.ShapeDtypeStruct((n_seq, 8, SSD_CONV_DIM), F32)),
        grid=(N_CHUNKS,),
        in_specs=[
            pl.BlockSpec((CHUNK, SSD_MAIN_WIDTH), lambda s: (s, 0)),
            pl.BlockSpec((CHUNK, 128), lambda s: (s, 0)),
            pl.BlockSpec((None, 1, SSD_GROUPS, SSD_GROUP_WIDTH, SSD_STATE), lambda s: (layer, samp(s), 0, 0, 0)),
            pl.BlockSpec((None, 1, 3, SSD_CONV_DIM), lambda s: (layer, samp(s), 0, 0)),
            pl.BlockSpec((None, 4, SSD_CONV_DIM), lambda s: (layer, 0, 0)),
            row_spec(SSD_CONV_DIM), row_spec(128), row_spec(128),
            pl.BlockSpec((128, SSD_INNER), lambda s: (0, 0)),
            row_spec(SSD_INNER), row_spec(SSD_INNER),
        ],
        out_specs=(pl.BlockSpec((CHUNK, SSD_INNER), lambda s: (s, 0)),
                   pl.BlockSpec((1, SSD_GROUPS, SSD_GROUP_WIDTH, SSD_STATE), lambda s: (_ssd_seq(s), 0, 0, 0)),
                   pl.BlockSpec((1, 8, SSD_CONV_DIM), lambda s: (_ssd_seq(s), 0, 0))),
        scratch_shapes=[
            pltpu.VMEM((SSD_GROUPS, SSD_STATE, SSD_GROUP_WIDTH), F32),
            pltpu.VMEM((CHUNK + 8, SSD_CONV_DIM), F32),
            pltpu.VMEM((CHUNK, SSD_CONV_DIM), F32),
            pltpu.VMEM((CHUNK, SSD_INNER), F32),
            pltpu.VMEM((CHUNK, SSD_INNER), F32),
        ],
        compiler_params=_params("arbitrary"),
        name="ssd_core",
    )(zx, dt_raw, st_all, conv_all, conv_w_all, conv_b.reshape(1, SSD_CONV_DIM), pad_heads(dt_bias),
      pad_heads(a_log), rep, jnp.repeat(d_skip, SSD_HEAD_DIM).reshape(1, SSD_INNER), g_norm.reshape(1, SSD_INNER))
    return y, st.reshape(n_seq, SSD_HEADS, SSD_HEAD_DIM, SSD_STATE), cvt[:, 5:8]


DIFF_SCALE = DIFF_HEAD_DIM ** -0.5
LOG2E = math.log2(math.e)


def _diff_lambda(lp_ref, lam_init):
    lp = lp_ref[...]
    a = jnp.sum(lp[0:1] * lp[1:2], axis=-1, keepdims=True)
    b = jnp.sum(lp[2:3] * lp[3:4], axis=-1, keepdims=True)
    return jnp.exp(a) - jnp.exp(b) + lam_init


def _alibi_slope(h):
    return jnp.exp2(-(jnp.zeros((1, 1), F32) + (h + 1).astype(F32)))


def _diff_prompt_kernel(q_ref, k_ref, v_ref, lp_ref, gs_ref, o_ref, *, lam_init, tq):
    h = pl.program_id(1)
    lam = _diff_lambda(lp_ref, lam_init)
    slope2 = _alibi_slope(h) * LOG2E
    c = DIFF_SCALE * LOG2E
    d = DIFF_HEAD_DIM
    t_i = lax.broadcasted_iota(jnp.int32, (tq, tq), 0)
    j_i = lax.broadcasted_iota(jnp.int32, (tq, tq), 1)
    bias_own = jnp.where(j_i // CHUNK <= t_i // CHUNK,
                         slope2 * (t_i - jnp.abs(t_i - j_i)).astype(F32), NEG)
    for qb in range(SEQ // tq):
        q0 = qb * tq
        rows = slice(q0, q0 + tq)
        if qb:
            bias_past = slope2 * (lax.broadcasted_iota(jnp.int32, (1, q0), 1) - q0).astype(F32)
        weights = []
        for m in range(2):
            cols = slice(m * d, (m + 1) * d)
            q = q_ref[rows, cols].astype(BF16)
            s_own = _dot_nt(q, k_ref[rows, cols].astype(BF16)) * c + bias_own
            mx = jnp.max(s_own, axis=-1, keepdims=True)
            if qb:
                s_past = _dot_nt(q, k_ref[0:q0, cols].astype(BF16)) * c + bias_past
                mx = jnp.maximum(mx, jnp.max(s_past, axis=-1, keepdims=True))
                e_past = jnp.exp2(s_past - mx)
            e_own = jnp.exp2(s_own - mx)
            den = jnp.sum(e_own, axis=-1, keepdims=True)
            if qb:
                den = den + jnp.sum(e_past, axis=-1, keepdims=True)
            weights.append((e_own, e_past if qb else None, 1.0 / den))
        (eo0, ep0, r0), (eo1, ep1, r1) = weights
        r1 = lam * r1
        o = _dot((eo0 * r0 - eo1 * r1).astype(BF16), v_ref[rows, :].astype(BF16))
        if qb:
            o = o + _dot((ep0 * r0 - ep1 * r1).astype(BF16), v_ref[0:q0, :].astype(BF16))
        o_ref[rows, :] = (_rms_rows(o, gs_ref[...]) * (1.0 - lam_init)).astype(BF16)


def diff_attn_prompt(qkv, lam_p, g_sub, lam_init, *, tq=256):
    w = 2 * DIFF_HEAD_DIM
    kern = functools.partial(_diff_prompt_kernel, lam_init=lam_init, tq=tq)
    return pl.pallas_call(
        kern,
        out_shape=jax.ShapeDtypeStruct((N_PROMPT, D_MODEL), BF16),
        grid=(BATCH, DIFF_HEADS),
        in_specs=[
            pl.BlockSpec((SEQ, w), lambda b, h: (b, h)),
            pl.BlockSpec((SEQ, w), lambda b, h: (b, DIFF_HEADS + h)),
            pl.BlockSpec((SEQ, w), lambda b, h: (b, 2 * DIFF_HEADS + h)),
            pl.BlockSpec((4, DIFF_HEAD_DIM), lambda b, h: (0, 0)),
            pl.BlockSpec((1, w), lambda b, h: (0, 0)),
        ],
        out_specs=pl.BlockSpec((SEQ, w), lambda b, h: (b, h)),
        compiler_params=_params("parallel", "parallel"),
        name="diff_attn_prompt",
    )(qkv, qkv, qkv, lam_p, g_sub.reshape(1, w))


def _diff_sample_kernel(q_ref, kn_ref, vn_ref, kc0_ref, kc1_ref, vc0_ref, vc1_ref, lp_ref, gs_ref, o_ref, vh_ref,
                        *, lam_init):
    kc_refs = (kc0_ref, kc1_ref)
    h = pl.program_id(1)
    lam = _diff_lambda(lp_ref, lam_init)
    slope = _alibi_slope(h)
    d = DIFF_HEAD_DIM
    t = DEC_SEQ
    tq_p = lax.broadcasted_iota(jnp.int32, (t, PAST_LEN), 0)
    kp_p = lax.broadcasted_iota(jnp.int32, (t, PAST_LEN), 1)
    bias_p = -slope * (PAST_LEN + tq_p - kp_p).astype(F32)
    tq_n = lax.broadcasted_iota(jnp.int32, (t, t), 0)
    kp_n = lax.broadcasted_iota(jnp.int32, (t, t), 1)
    bias_n = -slope * jnp.abs(tq_n - kp_n).astype(F32)
    pp, pn = [], []
    for m in range(2):
        cols = slice(m * d, (m + 1) * d)
        q = q_ref[:, cols].astype(BF16)
        kc = kc_refs[m].reshape(PAST_LEN, d)[...]
        sp = _dot_nt(q, kc.astype(BF16)) * DIFF_SCALE + bias_p
        sn = _dot_nt(q, kn_ref[:, cols].astype(BF16)) * DIFF_SCALE + bias_n
        mx = jnp.maximum(jnp.max(sp, axis=-1, keepdims=True), jnp.max(sn, axis=-1, keepdims=True))
        ep = jnp.exp(sp - mx)
        en = jnp.exp(sn - mx)
        den = jnp.sum(ep, axis=-1, keepdims=True) + jnp.sum(en, axis=-1, keepdims=True)
        pp.append(ep / den)
        pn.append(en / den)
    ap = (pp[0] - lam * pp[1]).astype(BF16)
    an = (pn[0] - lam * pn[1]).astype(BF16)
    for k in range(DIFF_HEADS):
        @pl.when(h == k)
        def _(k=k):
            for half, vc_ref in enumerate((vc0_ref, vc1_ref)):
                rows = vc_ref.reshape(PAST_LEN * DIFF_HEADS, d)
                vh_ref[:, half * d:(half + 1) * d] = rows[pl.ds(k, PAST_LEN, stride=DIFF_HEADS), :].astype(BF16)

    o = _dot(ap, vh_ref[...]) + _dot(an, vn_ref[...].astype(BF16))
    o_ref[...] = (_rms_rows(o, gs_ref[...]) * (1.0 - lam_init)).astype(BF16)


def diff_attn_sample(qkv, cache_k, cache_v, lam_p, g_sub, lam_init):
    w = 2 * DIFF_HEAD_DIM
    r0 = N_PROMPT // DEC_SEQ
    kern = functools.partial(_diff_sample_kernel, lam_init=lam_init)
    k_rows = cache_k.reshape(DEC_BATCH, PAST_LEN, 2 * DIFF_HEADS, 1, DIFF_HEAD_DIM)
    k_spec = lambda m: pl.BlockSpec((None, PAST_LEN, None, 1, DIFF_HEAD_DIM), lambda b, h: (b, 0, 2 * h + m, 0, 0))
    v_spec = lambda half: pl.BlockSpec((None, PAST_LEN, DIFF_HEADS, DIFF_HEAD_DIM), lambda b, h: (b, 0, 0, half))
    return pl.pallas_call(
        kern,
        out_shape=jax.ShapeDtypeStruct((N_SAMPLE, D_MODEL), BF16),
        grid=(DEC_BATCH, DIFF_HEADS),
        in_specs=[
            pl.BlockSpec((DEC_SEQ, w), lambda b, h: (r0 + b, h)),
            pl.BlockSpec((DEC_SEQ, w), lambda b, h: (r0 + b, DIFF_HEADS + h)),
            pl.BlockSpec((DEC_SEQ, w), lambda b, h: (r0 + b, 2 * DIFF_HEADS + h)),
            k_spec(0),
            k_spec(1),
            v_spec(0),
            v_spec(1),
            pl.BlockSpec((4, DIFF_HEAD_DIM), lambda b, h: (0, 0)),
            pl.BlockSpec((1, w), lambda b, h: (0, 0)),
        ],
        out_specs=pl.BlockSpec((DEC_SEQ, w), lambda b, h: (b, h)),
        scratch_shapes=[pltpu.VMEM((PAST_LEN, w), BF16)],
        compiler_params=_params("parallel", "arbitrary"),
        name="diff_attn_sample",
    )(qkv, qkv, qkv, k_rows, k_rows, cache_v, cache_v, lam_p, g_sub.reshape(1, w))


BAND_SCALE = BAND_HEAD_DIM ** -0.5
BAND_TQ = 256
BAND_E = 1024


def _band_bias(e_row, tq):
    eb = jnp.broadcast_to(e_row, (tq, BAND_E))
    return pltpu.roll(eb, BAND_E - (BAND_TQ - 1), 1, stride=1, stride_axis=0)


def _band_prompt_kernel(q_ref, k_ref, v_ref, e_ref, o_ref):
    tq = BAND_TQ
    win = BAND_WINDOW + tq
    bias = _band_bias(e_ref[0], tq)[:, :win]
    qc = lax.broadcasted_iota(jnp.int32, (tq, win), 0) // CHUNK
    jc = lax.broadcasted_iota(jnp.int32, (tq, win), 1) // CHUNK
    bias = jnp.where(jnp.logical_and(jc >= qc, jc <= qc + BAND_WINDOW // CHUNK), bias, NEG)
    for qb in range(SEQ // tq):
        q0 = qb * tq
        k0 = max(0, q0 - BAND_WINDOW)
        nk = q0 + tq - k0
        rows = slice(q0, q0 + tq)
        sc = _dot_nt(q_ref[rows, :].astype(BF16), k_ref[k0:k0 + nk, :].astype(BF16)) * BAND_SCALE
        sc = sc + bias[:, win - nk:]
        e = jnp.exp(sc - jnp.max(sc, axis=-1, keepdims=True))
        p = (e / jnp.sum(e, axis=-1, keepdims=True)).astype(BF16)
        o_ref[rows, :] = _dot(p, v_ref[k0:k0 + nk, :].astype(BF16)).astype(BF16)


def _band_e_rows(table):
    edge = jnp.broadcast_to(table[:, 2 * REL_CLIP:], (BAND_HEADS, BAND_E - (2 * REL_CLIP + 1)))
    return jnp.concatenate([edge, table[:, ::-1]], axis=1).reshape(BAND_HEADS, 1, BAND_E)


def band_attn_prompt(qkv, table):
    d = BAND_HEAD_DIM
    return pl.pallas_call(
        _band_prompt_kernel,
        out_shape=jax.ShapeDtypeStruct((N_PROMPT, D_MODEL), BF16),
        grid=(BATCH, BAND_HEADS),
        in_specs=[
            pl.BlockSpec((SEQ, d), lambda b, h: (b, h)),
            pl.BlockSpec((SEQ, d), lambda b, h: (b, BAND_HEADS + h)),
            pl.BlockSpec((SEQ, d), lambda b, h: (b, 2 * BAND_HEADS + h)),
            pl.BlockSpec((1, 1, BAND_E), lambda b, h: (h, 0, 0)),
        ],
        out_specs=pl.BlockSpec((SEQ, d), lambda b, h: (b, h)),
        compiler_params=_params("parallel", "parallel"),
        name="band_attn_prompt",
    )(qkv, qkv, qkv, _band_e_rows(table))


def _band_sample_kernel(q_ref, kn_ref, vn_ref, kc_ref, vc_ref, e_ref, o_ref):
    t = DEC_SEQ
    d = BAND_HEAD_DIM
    kc_rows = kc_ref.reshape(BAND_WINDOW * BAND_HEADS, d)
    vc_rows = vc_ref.reshape(BAND_WINDOW * BAND_HEADS, d)
    for h in range(BAND_HEADS):
        cols = slice(h * d, (h + 1) * d)
        bias = _band_bias(e_ref[h], t)
        q = q_ref[:, cols].astype(BF16)
        kc = kc_rows[pl.ds(h, BAND_WINDOW, stride=BAND_HEADS), :]
        vc = vc_rows[pl.ds(h, BAND_WINDOW, stride=BAND_HEADS), :]
        sp = _dot_nt(q, kc.astype(BF16)) * BAND_SCALE + bias[:, :BAND_WINDOW]
        sn = _dot_nt(q, kn_ref[:, cols].astype(BF16)) * BAND_SCALE + bias[:, BAND_WINDOW:BAND_WINDOW + t]
        mx = jnp.maximum(jnp.max(sp, axis=-1, keepdims=True), jnp.max(sn, axis=-1, keepdims=True))
        ep = jnp.exp(sp - mx)
        en = jnp.exp(sn - mx)
        den = jnp.sum(ep, axis=-1, keepdims=True) + jnp.sum(en, axis=-1, keepdims=True)
        o = (_dot((ep / den).astype(BF16), vc.astype(BF16))
             + _dot((en / den).astype(BF16), vn_ref[:, cols].astype(BF16)))
        o_ref[:, cols] = o.astype(BF16)


def band_attn_sample(qkv, cache_k, cache_v, table):
    d = BAND_HEAD_DIM
    r0 = N_PROMPT // DEC_SEQ
    rows_view = (DEC_BATCH, BAND_WINDOW, BAND_HEADS, 1, d)
    cache_spec = pl.BlockSpec((None, BAND_WINDOW, BAND_HEADS, 1, d), lambda b: (b, 0, 0, 0, 0))
    return pl.pallas_call(
        _band_sample_kernel,
        out_shape=jax.ShapeDtypeStruct((N_SAMPLE, D_MODEL), BF16),
        grid=(DEC_BATCH,),
        in_specs=[
            pl.BlockSpec((DEC_SEQ, D_MODEL), lambda b: (r0 + b, 0)),
            pl.BlockSpec((DEC_SEQ, D_MODEL), lambda b: (r0 + b, 1)),
            pl.BlockSpec((DEC_SEQ, D_MODEL), lambda b: (r0 + b, 2)),
            cache_spec,
            cache_spec,
            pl.BlockSpec((BAND_HEADS, 1, BAND_E), lambda b: (0, 0, 0)),
        ],
        out_specs=pl.BlockSpec((DEC_SEQ, D_MODEL), lambda b: (b, 0)),
        compiler_params=_params("parallel"),
        name="band_attn_sample",
    )(qkv, qkv, qkv, cache_k.reshape(rows_view), cache_v.reshape(rows_view), _band_e_rows(table))


def kernel(x_prompt, x_sample, p_prompt, p_sample, state_ssm, state_conv, cache_k_diff, cache_v_diff,
           cache_k_band, cache_v_band, g_ffn, w_ffn_gate, w_ffn_up, w_ffn_down, g_mix,
           ssd_w_in, ssd_conv_w, ssd_conv_b, ssd_dt_bias, ssd_a_log, ssd_d, ssd_g_norm, ssd_w_out,
           diff_w_in, diff_g_q, diff_g_k, diff_lambda, diff_g_sub, diff_w_out,
           band_w_in, band_g_q, band_g_k, band_rel_bias, band_w_out,
           g_ple, w_ple, w_ple_gate):
    h = jnp.concatenate([x_prompt.reshape(N_PROMPT, D_MODEL), x_sample.reshape(N_SAMPLE, D_MODEL)], axis=0)
    p_all = jnp.concatenate([p_prompt.reshape(DEPTH, N_PROMPT, PLE_DIM),
                             p_sample.reshape(DEPTH, N_SAMPLE, PLE_DIM)], axis=1)
    bf = lambda w: w.astype(BF16)
    wg_all, wu_all, wd_all = w_ffn_gate, w_ffn_up, w_ffn_down
    w_ple_all, w_gate_all = w_ple, w_ple_gate
    ssd_w_main = bf(ssd_w_in)
    ssd_w_dt = bf(jnp.pad(ssd_w_in[:, :, SSD_MAIN_WIDTH:], ((0, 0), (0, 0), (0, 128 - SSD_HEADS))))
    ssd_w_out_all, diff_w_in_all, diff_w_out_all = bf(ssd_w_out), diff_w_in, diff_w_out
    band_w_in_all, band_w_out_all = band_w_in, band_w_out

    def prompt_rows(a):
        return a[:N_PROMPT].reshape(BATCH, SEQ, -1)

    def sample_rows(a):
        return a[N_PROMPT:].reshape(DEC_BATCH, DEC_SEQ, -1)

    ssm_p, conv_p, ssm_s, conv_s = [], [], [], []
    kd_p = vd_p = kb_p = vb_p = kd_s = vd_s = kb_s = vb_s = None
    for i in range(DEPTH):
        kind, j = i % 3, i // 3
        h = ffn_half(h, g_ffn, wg_all, wu_all, wd_all, i, 0)
        if kind == 0:
            zx, dt_raw = inproj(h, g_mix[i], ssd_w_main, j, n_cols=SSD_MAIN_WIDTH, w_extra=ssd_w_dt)
            y, st, cvt = ssd_core(zx, dt_raw, state_ssm, state_conv, ssd_conv_w, j, ssd_conv_b[j], ssd_dt_bias[j],
                                  ssd_a_log[j], ssd_d[j], ssd_g_norm[j])
            h = outproj(h, y, ssd_w_out_all, j)
            ssm_p.append(st[:BATCH])
            ssm_s.append(st[BATCH:])
            conv_p.append(cvt[:BATCH])
            conv_s.append(cvt[BATCH:])
        elif kind == 1:
            lam_init = 0.8 - 0.6 * math.exp(-0.3 * i)
            gains = jnp.concatenate([jnp.tile(diff_g_q[j].reshape(-1), DIFF_HEADS),
                                     jnp.tile(diff_g_k[j].reshape(-1), DIFF_HEADS),
                                     jnp.ones((D_MODEL,), F32)]).reshape(1, 3 * D_MODEL)
            qkv = inproj(h, g_mix[i], diff_w_in_all, j, gains, norm_cols=2 * D_MODEL)
            o_p = diff_attn_prompt(qkv, diff_lambda[j], diff_g_sub[j], lam_init)
            o_s = diff_attn_sample(qkv, cache_k_diff[j], cache_v_diff[j], diff_lambda[j], diff_g_sub[j], lam_init)
            h = outproj(h, (o_p, o_s), diff_w_out_all, j)
            k_new, v_new = qkv[:, D_MODEL:2 * D_MODEL], qkv[:, 2 * D_MODEL:]
            kd_p = prompt_rows(k_new).reshape(1, BATCH, SEQ, DIFF_HEADS, 2, DIFF_HEAD_DIM)
            vd_p = prompt_rows(v_new).reshape(1, BATCH, SEQ, DIFF_HEADS, 2 * DIFF_HEAD_DIM)
            kd_s = sample_rows(k_new).reshape(1, DEC_BATCH, DEC_SEQ, DIFF_HEADS, 2, DIFF_HEAD_DIM)
            vd_s = sample_rows(v_new).reshape(1, DEC_BATCH, DEC_SEQ, DIFF_HEADS, 2 * DIFF_HEAD_DIM)
        else:
            gains = jnp.concatenate([jnp.tile(band_g_q[j], BAND_HEADS), jnp.tile(band_g_k[j], BAND_HEADS),
                                     jnp.ones((D_MODEL,), F32)]).reshape(1, 3 * D_MODEL)
            qkv = inproj(h, g_mix[i], band_w_in_all, j, gains, norm_cols=2 * D_MODEL)
            o_p = band_attn_prompt(qkv, band_rel_bias[j])
            o_s = band_attn_sample(qkv, cache_k_band[j], cache_v_band[j], band_rel_bias[j])
            h = outproj(h, (o_p, o_s), band_w_out_all, j)
            tail = jnp.stack([lax.slice(qkv, ((b + 1) * SEQ - BAND_WINDOW, D_MODEL), ((b + 1) * SEQ, 3 * D_MODEL))
                              for b in range(BATCH)])
            kb_p = tail[:, :, :D_MODEL].reshape(1, BATCH, BAND_WINDOW, BAND_HEADS, BAND_HEAD_DIM)
            vb_p = tail[:, :, D_MODEL:].reshape(1, BATCH, BAND_WINDOW, BAND_HEADS, BAND_HEAD_DIM)
            new_s = qkv[N_PROMPT:, D_MODEL:]
            kb_s = new_s[:, :D_MODEL].reshape(1, DEC_BATCH, DEC_SEQ, BAND_HEADS, BAND_HEAD_DIM)
            vb_s = new_s[:, D_MODEL:].reshape(1, DEC_BATCH, DEC_SEQ, BAND_HEADS, BAND_HEAD_DIM)
        h = ffn_half(h, g_ffn, wg_all, wu_all, wd_all, i, 1)
        h = ple_add(h, p_all, g_ple, w_ple_all, w_gate_all, i, split=(i == DEPTH - 1))
    h_p, h_s = h
    return (h_p.reshape(BATCH, SEQ, D_MODEL), h_s.reshape(DEC_BATCH, DEC_SEQ, D_MODEL),
            jnp.stack(ssm_p), jnp.stack(conv_p), kd_p, vd_p, kb_p, vb_p,
            jnp.stack(ssm_s), jnp.stack(conv_s), kd_s, vd_s, kb_s, vb_s)
```

```python
import functools
import math

import jax
import jax.numpy as jnp
from jax import lax
from jax.experimental import pallas as pl
from jax.experimental.pallas import tpu as pltpu

F32 = jnp.float32
BF16 = jnp.bfloat16

D_MODEL = 2048
BATCH = 4
SEQ = 2048
DEPTH = 4
DEC_BATCH = 8
DEC_SEQ = 64
PAST_LEN = 2048
CHUNK = 64
NORM_EPS = 1e-6
FFN_DIM = 5632
PLE_DIM = 256
SSD_INNER = 4096
SSD_HEADS = 64
SSD_HEAD_DIM = 64
SSD_GROUPS = 8
SSD_STATE = 128
SSD_GROUP_WIDTH = SSD_INNER // SSD_GROUPS
SSD_CONV_DIM = SSD_INNER + 2 * SSD_GROUPS * SSD_STATE
SSD_MAIN_WIDTH = SSD_INNER + SSD_CONV_DIM
SSD_EXT_WIDTH = SSD_MAIN_WIDTH + SSD_INNER
DIFF_HEADS = 8
DIFF_HEAD_DIM = 128
BAND_HEADS = 16
BAND_HEAD_DIM = 128
BAND_WINDOW = 512
REL_CLIP = 256

N_PROMPT = BATCH * SEQ
N_SAMPLE = DEC_BATCH * DEC_SEQ
N_TOK = N_PROMPT + N_SAMPLE
CHUNKS_PER_SEQ = SEQ // CHUNK
N_PROMPT_CHUNKS = N_PROMPT // CHUNK
N_CHUNKS = N_TOK // CHUNK

TM_WIDE = 1088
TM = 544
TM_SPLIT = 512
INPROJ_NORM_CHUNKS = 2
N_PROMPT_TILES = N_PROMPT // TM_SPLIT
VMEM_LIMIT = 56 * 1024 * 1024
NEG = -1e30


def _params(*sem, vmem=VMEM_LIMIT):
    return pltpu.CompilerParams(dimension_semantics=sem, vmem_limit_bytes=vmem)


def _rms_rows(x, g):
    ms = jnp.mean(x * x, axis=-1, keepdims=True)
    return x * lax.rsqrt(ms + NORM_EPS) * g


def _dot(a, b):
    return jnp.dot(a, b, preferred_element_type=F32)


def _dot_nt(a, b):
    return lax.dot_general(a, b, (((1,), (1,)), ((), ())), preferred_element_type=F32)


def _dot_tn(a, b):
    return lax.dot_general(a, b, (((0,), (0,)), ((), ())), preferred_element_type=F32)


def _ffn_step(first, h_ref, g_ref, wg_ref, wu_ref, wd_ref, o_ref, xn_ref):
    @pl.when(first)
    def _():
        x = h_ref[...]
        xn_ref[...] = _rms_rows(x, g_ref[...]).astype(BF16)
        o_ref[...] = x

    xn = xn_ref[...]
    gate = _dot(xn, wg_ref[...])
    up = _dot(xn, wu_ref[...])
    act = (0.5 * (gate * jax.nn.sigmoid(gate)) * up).astype(BF16)
    o_ref[...] += _dot(act, wd_ref[...])


def _ffn_first_kernel(h_ref, g_ref, wg_ref, wu_ref, wd_ref, o_ref, wg_bf_ref, wu_bf_ref, wd_bf_ref, xn_ref):
    wg_bf_ref[...] = wg_ref[...].astype(BF16)
    wu_bf_ref[...] = wu_ref[...].astype(BF16)
    wd_bf_ref[...] = wd_ref[...].astype(BF16)
    _ffn_step(pl.program_id(0) == 0, h_ref, g_ref, wg_bf_ref, wu_bf_ref, wd_bf_ref, o_ref, xn_ref)


def _ffn_rest_kernel(h_ref, g_ref, wg_ref, wu_ref, wd_ref, partial_ref, o_ref, xn_ref):
    del partial_ref
    _ffn_step(pl.program_id(1) == 0, h_ref, g_ref, wg_ref, wu_ref, wd_ref, o_ref, xn_ref)


def ffn_half(h, g_all, wg_all, wu_all, wd_all, layer, half, *, tf=512, tf_first=256):
    tm = TM
    tm_first = TM_WIDE
    skip = tm_first // tm
    g4 = g_all.reshape(DEPTH, 2, 1, D_MODEL)
    g_spec1 = pl.BlockSpec((None, None, 1, D_MODEL), lambda f: (layer, half, 0, 0))
    partial, wg_bf, wu_bf, wd_bf = pl.pallas_call(
        _ffn_first_kernel,
        out_shape=(jax.ShapeDtypeStruct((N_TOK, D_MODEL), F32),
                   jax.ShapeDtypeStruct((D_MODEL, FFN_DIM), BF16),
                   jax.ShapeDtypeStruct((D_MODEL, FFN_DIM), BF16),
                   jax.ShapeDtypeStruct((FFN_DIM, D_MODEL), BF16)),
        grid=(FFN_DIM // tf_first,),
        in_specs=[
            pl.BlockSpec((tm_first, D_MODEL), lambda f: (0, 0), pipeline_mode=pl.Buffered(1)),
            g_spec1,
            pl.BlockSpec((None, None, D_MODEL, tf_first), lambda f: (layer, half, 0, f)),
            pl.BlockSpec((None, None, D_MODEL, tf_first), lambda f: (layer, half, 0, f)),
            pl.BlockSpec((None, None, tf_first, D_MODEL), lambda f: (layer, half, f, 0)),
        ],
        out_specs=(pl.BlockSpec((tm_first, D_MODEL), lambda f: (0, 0)),
                   pl.BlockSpec((D_MODEL, tf_first), lambda f: (0, f)),
                   pl.BlockSpec((D_MODEL, tf_first), lambda f: (0, f)),
                   pl.BlockSpec((tf_first, D_MODEL), lambda f: (f, 0))),
        scratch_shapes=[pltpu.VMEM((tm_first, D_MODEL), BF16)],
        compiler_params=_params("arbitrary"),
        name="ffn_first",
    )(h, g4, wg_all, wu_all, wd_all)
    return pl.pallas_call(
        _ffn_rest_kernel,
        out_shape=jax.ShapeDtypeStruct((N_TOK, D_MODEL), F32),
        grid=(N_TOK // tm - skip, FFN_DIM // tf),
        in_specs=[
            pl.BlockSpec((tm, D_MODEL), lambda i, f: (i + skip, 0)),
            pl.BlockSpec((None, None, 1, D_MODEL), lambda i, f: (layer, half, 0, 0)),
            pl.BlockSpec((D_MODEL, tf), lambda i, f: (0, f)),
            pl.BlockSpec((D_MODEL, tf), lambda i, f: (0, f)),
            pl.BlockSpec((tf, D_MODEL), lambda i, f: (f, 0)),
            pl.BlockSpec(memory_space=pl.ANY),
        ],
        out_specs=pl.BlockSpec((tm, D_MODEL), lambda i, f: (i + skip, 0)),
        scratch_shapes=[pltpu.VMEM((tm, D_MODEL), BF16)],
        input_output_aliases={5: 0},
        compiler_params=_params("parallel", "arbitrary"),
        name="ffn_rest",
    )(h, g4, wg_bf, wu_bf, wd_bf, partial)


def _inproj_kernel(h_ref, g_ref, w_ref, hg_ref, *rest, norm_tiles, tn, has_extra, first, n_alias=0):
    rest = list(rest)
    wx_ref = rest.pop(0) if has_extra else None
    del rest[:n_alias]
    o_ref = rest.pop(0)
    ox_ref = rest.pop(0) if has_extra else None
    wbf_ref = rest.pop(0) if first else None
    (xn_ref,) = rest
    j = pl.program_id(0 if first else 1)

    @pl.when(j == 0)
    def _():
        xn = _rms_rows(h_ref[...], g_ref[...]).astype(BF16)
        xn_ref[...] = xn
        if has_extra:
            ox_ref[...] = _dot(xn, wx_ref[...])

    if first:
        wbf_ref[...] = w_ref[...].astype(BF16)
        w_ref = wbf_ref

    def plain():
        o_ref[...] = _dot(xn_ref[...], w_ref[...])

    if norm_tiles == 0:
        plain()
        return
    pl.when(j >= norm_tiles)(plain)

    @pl.when(j < norm_tiles)
    def _():
        rc = xn_ref.shape[0] // INPROJ_NORM_CHUNKS
        for r in range(INPROJ_NORM_CHUNKS):
            rows = slice(r * rc, (r + 1) * rc)
            y = _dot(xn_ref[rows, :], w_ref[...])
            for c in range(tn // 128):
                sl = slice(c * 128, (c + 1) * 128)
                o_ref[rows, sl] = _rms_rows(y[:, sl], hg_ref[:, sl])


def inproj(h, g, w_all, layer, head_gain=None, *, n_cols=None, norm_cols=0, w_extra=None, tn=1024, tn_first=512):
    tm = TM_WIDE
    n = w_all.shape[2] if n_cols is None else n_cols
    if head_gain is None:
        head_gain = jnp.ones((1, n), F32)
    has_extra = w_extra is not None
    g2 = g.reshape(1, D_MODEL)
    nx = w_extra.shape[2] if has_extra else 0
    out_shapes = [jax.ShapeDtypeStruct((N_TOK, n), F32)]
    if has_extra:
        out_shapes.append(jax.ShapeDtypeStruct((N_TOK, nx), F32))

    partials = []
    if w_all.dtype == BF16:
        w_bf = w_all
        w_spec = pl.BlockSpec((None, D_MODEL, tn), lambda i, j: (layer, 0, j))
    else:
        kern = functools.partial(_inproj_kernel, norm_tiles=norm_cols // tn_first, tn=tn_first, has_extra=has_extra,
                                 first=True)
        in_specs = [
            pl.BlockSpec((tm, D_MODEL), lambda j: (0, 0)),
            pl.BlockSpec((1, D_MODEL), lambda j: (0, 0)),
            pl.BlockSpec((None, D_MODEL, tn_first), lambda j: (layer, 0, j)),
            pl.BlockSpec((1, tn_first), lambda j: (0, j)),
        ]
        args = [h, g2, w_all, head_gain]
        out_specs = [pl.BlockSpec((tm, tn_first), lambda j: (0, j))]
        if has_extra:
            in_specs.append(pl.BlockSpec((None, D_MODEL, nx), lambda j: (layer, 0, 0)))
            args.append(w_extra)
            out_specs.append(pl.BlockSpec((tm, nx), lambda j: (0, 0)))
        out_specs.append(pl.BlockSpec((D_MODEL, tn_first), lambda j: (0, j)))
        *partials, w_bf = pl.pallas_call(
            kern,
            out_shape=tuple(out_shapes) + (jax.ShapeDtypeStruct((D_MODEL, n), BF16),),
            grid=(n // tn_first,),
            in_specs=in_specs,
            out_specs=tuple(out_specs),
            scratch_shapes=[pltpu.VMEM((tm, D_MODEL), BF16)],
            compiler_params=_params("arbitrary"),
            name="inproj_first",
        )(*args)
        w_spec = pl.BlockSpec((D_MODEL, tn), lambda i, j: (0, j))

    r0 = len(partials) and 1
    kern = functools.partial(_inproj_kernel, norm_tiles=norm_cols // tn, tn=tn, has_extra=has_extra, first=False,
                             n_alias=len(partials))
    in_specs = [
        pl.BlockSpec((tm, D_MODEL), lambda i, j: (i + r0, 0)),
        pl.BlockSpec((1, D_MODEL), lambda i, j: (0, 0)),
        w_spec,
        pl.BlockSpec((1, tn), lambda i, j: (0, j)),
    ]
    args = [h, g2, w_bf, head_gain]
    out_specs = [pl.BlockSpec((tm, tn), lambda i, j: (i + r0, j))]
    if has_extra:
        in_specs.append(pl.BlockSpec((None, D_MODEL, nx), lambda i, j: (layer, 0, 0)))
        args.append(w_extra)
        out_specs.append(pl.BlockSpec((tm, nx), lambda i, j: (i + r0, 0)))
    n_in = len(in_specs)
    in_specs += [pl.BlockSpec(memory_space=pl.ANY)] * len(partials)
    args += partials
    outs = pl.pallas_call(
        kern,
        out_shape=tuple(out_shapes),
        grid=(N_TOK // tm - r0, n // tn),
        in_specs=in_specs,
        out_specs=tuple(out_specs),
        scratch_shapes=[pltpu.VMEM((tm, D_MODEL), BF16)],
        input_output_aliases={n_in + k: k for k in range(len(partials))},
        compiler_params=_params("parallel", "arbitrary"),
        name="inproj_main",
    )(*args)
    return outs if has_extra else outs[0]


def _cast_once(first, w_ref, wbf_ref):
    @pl.when(first)
    def _():
        wbf_ref[...] = w_ref[...].astype(BF16)


def _outproj_kernel(a_ref, w_ref, h_ref, o_ref, *scratch):
    wbf_ref = w_ref
    if scratch:
        (wbf_ref,) = scratch
        _cast_once(pl.program_id(1) == 0, w_ref, wbf_ref)
    o_ref[...] = h_ref[...] + _dot(a_ref[...], wbf_ref[...])


def _outproj_split_kernel(ap_ref, as_ref, w_ref, h_ref, o_ref, wbf_ref):
    i = pl.program_id(1)
    _cast_once(i == 0, w_ref, wbf_ref)

    @pl.when(i < N_PROMPT_TILES)
    def _():
        o_ref[...] = h_ref[...] + _dot(ap_ref[...], wbf_ref[...])

    @pl.when(i >= N_PROMPT_TILES)
    def _():
        o_ref[...] = h_ref[...] + _dot(as_ref[...], wbf_ref[...])


def outproj(h, a, w_all, layer):
    tm = TM_SPLIT
    k = w_all.shape[1]
    tn = 1024
    scratch = [] if w_all.dtype == BF16 else [pltpu.VMEM((k, tn), BF16)]
    w_spec = pl.BlockSpec((None, k, tn), lambda j, i: (layer, 0, j))
    h_spec = pl.BlockSpec((tm, tn), lambda j, i: (i, j))
    if isinstance(a, tuple):
        kern = _outproj_split_kernel
        a_specs = [pl.BlockSpec((tm, k), lambda j, i: (jnp.minimum(i, N_PROMPT_TILES - 1), 0)),
                   pl.BlockSpec((tm, k), lambda j, i: (0, 0))]
        args = list(a)
    else:
        kern = _outproj_kernel
        a_specs = [pl.BlockSpec((tm, k), lambda j, i: (i, 0))]
        args = [a]
    return pl.pallas_call(
        kern,
        out_shape=jax.ShapeDtypeStruct((N_TOK, D_MODEL), F32),
        grid=(D_MODEL // tn, N_TOK // tm),
        in_specs=a_specs + [w_spec, h_spec],
        out_specs=h_spec,
        scratch_shapes=scratch,
        compiler_params=_params("parallel", "arbitrary"),
        name="outproj",
    )(*args, w_all, h)


def _ple_update(h_ref, p_ref, g_ref, wp_ref, wgt_ref):
    x = h_ref[...]
    xn = _rms_rows(x, g_ref[...]).astype(BF16)
    gate = jax.nn.sigmoid(_dot(xn, wgt_ref[...]))
    emb = _dot(p_ref[...].astype(BF16), wp_ref[...])
    return x + emb * gate


def _ple_kernel(h_ref, p_ref, g_ref, wp_ref, wgt_ref, o_ref, wp_bf_ref, wgt_bf_ref):
    first = pl.program_id(0) == 0
    _cast_once(first, wp_ref, wp_bf_ref)
    _cast_once(first, wgt_ref, wgt_bf_ref)
    o_ref[...] = _ple_update(h_ref, p_ref, g_ref, wp_bf_ref, wgt_bf_ref)


def _ple_split_kernel(h_ref, p_ref, g_ref, wp_ref, wgt_ref, op_ref, os_ref, wp_bf_ref, wgt_bf_ref):
    i = pl.program_id(0)
    _cast_once(i == 0, wp_ref, wp_bf_ref)
    _cast_once(i == 0, wgt_ref, wgt_bf_ref)

    @pl.when(i < N_PROMPT_TILES)
    def _():
        op_ref[...] = _ple_update(h_ref, p_ref, g_ref, wp_bf_ref, wgt_bf_ref)

    @pl.when(i >= N_PROMPT_TILES)
    def _():
        os_ref[...] = _ple_update(h_ref, p_ref, g_ref, wp_bf_ref, wgt_bf_ref)


def ple_add(h, p_all, g_all, w_ple_all, w_gate_all, layer, *, split=False):
    tm = TM_SPLIT if split else TM
    in_specs = [
        pl.BlockSpec((tm, D_MODEL), lambda i: (i, 0)),
        pl.BlockSpec((None, tm, PLE_DIM), lambda i: (layer, i, 0)),
        pl.BlockSpec((None, 1, D_MODEL), lambda i: (layer, 0, 0)),
        pl.BlockSpec((None, PLE_DIM, D_MODEL), lambda i: (layer, 0, 0), pipeline_mode=pl.Buffered(1)),
        pl.BlockSpec((None, D_MODEL, D_MODEL), lambda i: (layer, 0, 0), pipeline_mode=pl.Buffered(1)),
    ]
    if split:
        kern = _ple_split_kernel
        out_shape = (jax.ShapeDtypeStruct((N_PROMPT, D_MODEL), F32), jax.ShapeDtypeStruct((N_SAMPLE, D_MODEL), F32))
        out_specs = (pl.BlockSpec((tm, D_MODEL), lambda i: (jnp.minimum(i, N_PROMPT_TILES - 1), 0)),
                     pl.BlockSpec((tm, D_MODEL), lambda i: (0, 0)))
    else:
        kern = _ple_kernel
        out_shape = jax.ShapeDtypeStruct((N_TOK, D_MODEL), F32)
        out_specs = pl.BlockSpec((tm, D_MODEL), lambda i: (i, 0))
    return pl.pallas_call(
        kern,
        out_shape=out_shape,
        grid=(N_TOK // tm,),
        in_specs=in_specs,
        out_specs=out_specs,
        scratch_shapes=[pltpu.VMEM((PLE_DIM, D_MODEL), BF16), pltpu.VMEM((D_MODEL, D_MODEL), BF16)],
        compiler_params=_params("arbitrary"),
        name="ple_add",
    )(h, p_all, g_all.reshape(DEPTH, 1, D_MODEL), w_ple_all, w_gate_all)


def _ssd_kernel(zx_ref, dtr_ref, st0_ref, cv0_ref, cw_ref, cb_ref, dtb_ref, alog_ref, rep_ref, dsk_ref, gn_ref,
                y_ref, st_ref, cvt_ref, s_scr, xpad, act, pcum, dts):
    s = pl.program_id(0)
    is_sample = s >= N_PROMPT_CHUNKS
    pos = s % CHUNKS_PER_SEQ
    first = jnp.logical_or(is_sample, pos == 0)
    last = jnp.logical_or(is_sample, pos == CHUNKS_PER_SEQ - 1)

    @pl.when(jnp.logical_and(first, jnp.logical_not(is_sample)))
    def _():
        s_scr[...] = jnp.zeros_like(s_scr)
        xpad[0:8, :] = jnp.zeros((8, SSD_CONV_DIM), F32)

    @pl.when(is_sample)
    def _():
        for g in range(SSD_GROUPS):
            s_scr[g] = st0_ref[0, g].T
        xpad[0:8, :] = jnp.zeros((8, SSD_CONV_DIM), F32)
        xpad[5:8, :] = cv0_ref[0]

    xpad[8:72, :] = zx_ref[:, SSD_INNER:SSD_MAIN_WIDTH]
    conv = cb_ref[...] + cw_ref[3:4, :] * xpad[8:72, :]
    conv = conv + cw_ref[2:3, :] * xpad[7:71, :]
    conv = conv + cw_ref[1:2, :] * xpad[6:70, :]
    conv = conv + cw_ref[0:1, :] * xpad[5:69, :]
    act[...] = conv * jax.nn.sigmoid(conv)
    xpad[0:8, :] = xpad[64:72, :]

    x = dtr_ref[...] + dtb_ref[...]
    dt = jnp.maximum(x, 0.0) + jnp.log1p(jnp.exp(-jnp.abs(x)))
    run = dt * (-jnp.exp(alog_ref[...]))
    row = lax.broadcasted_iota(jnp.int32, (CHUNK, 128), 0)
    for k in (1, 2, 4, 8, 16, 32):
        run = run + jnp.where(row >= k, pltpu.roll(run, k, 0), 0.0)
    both = jnp.concatenate([dt, run], axis=0)
    hi = both.astype(BF16)
    r1 = both - hi.astype(F32)
    mid = r1.astype(BF16)
    lo = (r1 - mid.astype(F32)).astype(BF16)
    wide = _dot(jnp.concatenate([hi, mid, lo], axis=0), rep_ref[...])
    wide = (wide[0:2 * CHUNK] + wide[2 * CHUNK:4 * CHUNK]) + wide[4 * CHUNK:6 * CHUNK]
    dts[...] = wide[0:CHUNK]
    pcum[...] = wide[CHUNK:2 * CHUNK]

    gw = SSD_GROUP_WIDTH
    row_g = lax.broadcasted_iota(jnp.int32, (CHUNK, gw), 0)
    lane_g = lax.broadcasted_iota(jnp.int32, (CHUNK, gw), 1) % CHUNK
    diag = row_g == lane_g
    causal = row_g >= lane_g
    r4 = lax.broadcasted_iota(jnp.int32, (256, 256), 0) // CHUNK
    c4 = lax.broadcasted_iota(jnp.int32, (256, 256), 1) // SSD_HEAD_DIM
    head_diag = r4 == c4

    for g in range(SSD_GROUPS):
        cs = slice(g * gw, (g + 1) * gw)
        p_g = pcum[:, cs]
        dt_g = dts[:, cs]
        x_g = act[:, cs]
        b_g = act[:, SSD_INNER + g * SSD_STATE:SSD_INNER + (g + 1) * SSD_STATE].astype(BF16)
        c_g = act[:, SSD_INNER + SSD_GROUPS * SSD_STATE + g * SSD_STATE:
                  SSD_INNER + SSD_GROUPS * SSD_STATE + (g + 1) * SSD_STATE].astype(BF16)
        p_s = jnp.sum(jnp.where(diag, p_g, 0.0), axis=0, keepdims=True)
        dt_s = jnp.sum(jnp.where(diag, dt_g, 0.0), axis=0, keepdims=True)
        p_last = p_g[CHUNK - 1:CHUNK, :]
        cb = _dot_nt(c_g, jnp.concatenate([b_g] * 8, axis=0))
        m = (cb * jnp.exp(jnp.where(causal, p_g - p_s, NEG)) * dt_s).astype(BF16)
        x_bf = x_g.astype(BF16)
        halves = []
        for hh in range(2):
            hs = slice(hh * 256, (hh + 1) * 256)
            xh = x_bf[:, hs]
            x_bd = jnp.where(head_diag, jnp.concatenate([xh] * 4, axis=0), jnp.zeros((), BF16))
            halves.append(_dot(m[:, hs], x_bd))
        y = jnp.concatenate(halves, axis=1)
        st = s_scr[g]
        y = y + _dot(c_g, st.astype(BF16)) * jnp.exp(p_g)
        wx = (jnp.exp(p_last - p_g) * dt_g * x_g).astype(BF16)
        s_scr[g] = st * jnp.exp(p_last) + _dot_tn(b_g, wx)
        y = y + dsk_ref[:, cs] * x_g
        z = zx_ref[:, cs]
        y = y * (z * jax.nn.sigmoid(z))
        y_ref[:, cs] = _rms_rows(y, gn_ref[:, cs]).astype(BF16)

    @pl.when(last)
    def _():
        cvt_ref[0] = xpad[0:8, :]
        for g in range(SSD_GROUPS):
            st_ref[0, g] = s_scr[g].T


def _ssd_seq(s):
    return jnp.where(s < N_PROMPT_CHUNKS, s // CHUNKS_PER_SEQ, BATCH + s - N_PROMPT_CHUNKS)


def ssd_core(zx, dt_raw, state_all, conv_all, conv_w_all, layer, conv_b, dt_bias, a_log, d_skip, g_norm):
    n_seq = BATCH + DEC_BATCH
    pad_heads = lambda v: jnp.pad(v, (0, 128 - SSD_HEADS)).reshape(1, 128)
    samp = lambda s: jnp.maximum(s - N_PROMPT_CHUNKS, 0)
    row_spec = lambda w: pl.BlockSpec((1, w), lambda s: (0, 0))
    st_all = state_all.reshape(-1, DEC_BATCH, SSD_GROUPS, SSD_GROUP_WIDTH, SSD_STATE)
    head_of_channel = jnp.arange(SSD_INNER, dtype=jnp.int32) // SSD_HEAD_DIM
    rep = (jnp.arange(128, dtype=jnp.int32)[:, None] == head_of_channel[None, :]).astype(BF16)
    y, st, cvt = pl.pallas_call(
        _ssd_kernel,
        out_shape=(jax.ShapeDtypeStruct((N_TOK, SSD_INNER), BF16),
                   jax.ShapeDtypeStruct((n_seq, SSD_GROUPS, SSD_GROUP_WIDTH, SSD_STATE), F32),
                   jax.ShapeDtypeStruct((n_seq, 8, SSD_CONV_DIM), F32)),
        grid=(N_CHUNKS,),
        in_specs=[
            pl.BlockSpec((CHUNK, SSD_MAIN_WIDTH), lambda s: (s, 0)),
            pl.BlockSpec((CHUNK, 128), lambda s: (s, 0)),
            pl.BlockSpec((None, 1, SSD_GROUPS, SSD_GROUP_WIDTH, SSD_STATE), lambda s: (layer, samp(s), 0, 0, 0)),
            pl.BlockSpec((None, 1, 3, SSD_CONV_DIM), lambda s: (layer, samp(s), 0, 0)),
            pl.BlockSpec((None, 4, SSD_CONV_DIM), lambda s: (layer, 0, 0)),
            row_spec(SSD_CONV_DIM), row_spec(128), row_spec(128),
            pl.BlockSpec((128, SSD_INNER), lambda s: (0, 0)),
            row_spec(SSD_INNER), row_spec(SSD_INNER),
        ],
        out_specs=(pl.BlockSpec((CHUNK, SSD_INNER), lambda s: (s, 0)),
                   pl.BlockSpec((1, SSD_GROUPS, SSD_GROUP_WIDTH, SSD_STATE), lambda s: (_ssd_seq(s), 0, 0, 0)),
                   pl.BlockSpec((1, 8, SSD_CONV_DIM), lambda s: (_ssd_seq(s), 0, 0))),
        scratch_shapes=[
            pltpu.VMEM((SSD_GROUPS, SSD_STATE, SSD_GROUP_WIDTH), F32),
            pltpu.VMEM((CHUNK + 8, SSD_CONV_DIM), F32),
            pltpu.VMEM((CHUNK, SSD_CONV_DIM), F32),
            pltpu.VMEM((CHUNK, SSD_INNER), F32),
            pltpu.VMEM((CHUNK, SSD_INNER), F32),
        ],
        compiler_params=_params("arbitrary"),
        name="ssd_core",
    )(zx, dt_raw, st_all, conv_all, conv_w_all, conv_b.reshape(1, SSD_CONV_DIM), pad_heads(dt_bias),
      pad_heads(a_log), rep, jnp.repeat(d_skip, SSD_HEAD_DIM).reshape(1, SSD_INNER), g_norm.reshape(1, SSD_INNER))
    return y, st.reshape(n_seq, SSD_HEADS, SSD_HEAD_DIM, SSD_STATE), cvt[:, 5:8]


DIFF_SCALE = DIFF_HEAD_DIM ** -0.5
LOG2E = math.log2(math.e)


def _diff_lambda(lp_ref, lam_init):
    lp = lp_ref[...]
    a = jnp.sum(lp[0:1] * lp[1:2], axis=-1, keepdims=True)
    b = jnp.sum(lp[2:3] * lp[3:4], axis=-1, keepdims=True)
    return jnp.exp(a) - jnp.exp(b) + lam_init


def _alibi_slope(h):
    return jnp.exp2(-(jnp.zeros((1, 1), F32) + (h + 1).astype(F32)))


def _diff_prompt_kernel(q_ref, k_ref, v_ref, lp_ref, gs_ref, o_ref, *, lam_init, tq):
    h = pl.program_id(1)
    lam = _diff_lambda(lp_ref, lam_init)
    slope2 = _alibi_slope(h) * LOG2E
    c = DIFF_SCALE * LOG2E
    d = DIFF_HEAD_DIM
    t_i = lax.broadcasted_iota(jnp.int32, (tq, tq), 0)
    j_i = lax.broadcasted_iota(jnp.int32, (tq, tq), 1)
    bias_own = jnp.where(j_i // CHUNK <= t_i // CHUNK,
                         slope2 * (t_i - jnp.abs(t_i - j_i)).astype(F32), NEG)
    for qb in range(SEQ // tq):
        q0 = qb * tq
        rows = slice(q0, q0 + tq)
        if qb:
            bias_past = slope2 * (lax.broadcasted_iota(jnp.int32, (1, q0), 1) - q0).astype(F32)
        weights = []
        for m in range(2):
            cols = slice(m * d, (m + 1) * d)
            q = q_ref[rows, cols].astype(BF16)
            s_own = _dot_nt(q, k_ref[rows, cols].astype(BF16)) * c + bias_own
            mx = jnp.max(s_own, axis=-1, keepdims=True)
            if qb:
                s_past = _dot_nt(q, k_ref[0:q0, cols].astype(BF16)) * c + bias_past
                mx = jnp.maximum(mx, jnp.max(s_past, axis=-1, keepdims=True))
                e_past = jnp.exp2(s_past - mx)
            e_own = jnp.exp2(s_own - mx)
            den = jnp.sum(e_own, axis=-1, keepdims=True)
            if qb:
                den = den + jnp.sum(e_past, axis=-1, keepdims=True)
            weights.append((e_own, e_past if qb else None, 1.0 / den))
        (eo0, ep0, r0), (eo1, ep1, r1) = weights
        r1 = lam * r1
        o = _dot((eo0 * r0 - eo1 * r1).astype(BF16), v_ref[rows, :].astype(BF16))
        if qb:
            o = o + _dot((ep0 * r0 - ep1 * r1).astype(BF16), v_ref[0:q0, :].astype(BF16))
        o_ref[rows, :] = (_rms_rows(o, gs_ref[...]) * (1.0 - lam_init)).astype(BF16)


def diff_attn_prompt(qkv, lam_p, g_sub, lam_init, *, tq=256):
    w = 2 * DIFF_HEAD_DIM
    kern = functools.partial(_diff_prompt_kernel, lam_init=lam_init, tq=tq)
    return pl.pallas_call(
        kern,
        out_shape=jax.ShapeDtypeStruct((N_PROMPT, D_MODEL), BF16),
        grid=(BATCH, DIFF_HEADS),
        in_specs=[
            pl.BlockSpec((SEQ, w), lambda b, h: (b, h)),
            pl.BlockSpec((SEQ, w), lambda b, h: (b, DIFF_HEADS + h)),
            pl.BlockSpec((SEQ, w), lambda b, h: (b, 2 * DIFF_HEADS + h)),
            pl.BlockSpec((4, DIFF_HEAD_DIM), lambda b, h: (0, 0)),
            pl.BlockSpec((1, w), lambda b, h: (0, 0)),
        ],
        out_specs=pl.BlockSpec((SEQ, w), lambda b, h: (b, h)),
        compiler_params=_params("parallel", "parallel"),
        name="diff_attn_prompt",
    )(qkv, qkv, qkv, lam_p, g_sub.reshape(1, w))


def _diff_sample_kernel(q_ref, kn_ref, vn_ref, kc0_ref, kc1_ref, vc0_ref, vc1_ref, lp_ref, gs_ref, o_ref, vh_ref,
                        *, lam_init):
    kc_refs = (kc0_ref, kc1_ref)
    h = pl.program_id(1)
    lam = _diff_lambda(lp_ref, lam_init)
    slope = _alibi_slope(h)
    d = DIFF_HEAD_DIM
    t = DEC_SEQ
    tq_p = lax.broadcasted_iota(jnp.int32, (t, PAST_LEN), 0)
    kp_p = lax.broadcasted_iota(jnp.int32, (t, PAST_LEN), 1)
    bias_p = -slope * (PAST_LEN + tq_p - kp_p).astype(F32)
    tq_n = lax.broadcasted_iota(jnp.int32, (t, t), 0)
    kp_n = lax.broadcasted_iota(jnp.int32, (t, t), 1)
    bias_n = -slope * jnp.abs(tq_n - kp_n).astype(F32)
    pp, pn = [], []
    for m in range(2):
        cols = slice(m * d, (m + 1) * d)
        q = q_ref[:, cols].astype(BF16)
        kc = kc_refs[m].reshape(PAST_LEN, d)[...]
        sp = _dot_nt(q, kc.astype(BF16)) * DIFF_SCALE + bias_p
        sn = _dot_nt(q, kn_ref[:, cols].astype(BF16)) * DIFF_SCALE + bias_n
        mx = jnp.maximum(jnp.max(sp, axis=-1, keepdims=True), jnp.max(sn, axis=-1, keepdims=True))
        ep = jnp.exp(sp - mx)
        en = jnp.exp(sn - mx)
        den = jnp.sum(ep, axis=-1, keepdims=True) + jnp.sum(en, axis=-1, keepdims=True)
        pp.append(ep / den)
        pn.append(en / den)
    ap = (pp[0] - lam * pp[1]).astype(BF16)
    an = (pn[0] - lam * pn[1]).astype(BF16)
    for k in range(DIFF_HEADS):
        @pl.when(h == k)
        def _(k=k):
            for half, vc_ref in enumerate((vc0_ref, vc1_ref)):
                rows = vc_ref.reshape(PAST_LEN * DIFF_HEADS, d)
                vh_ref[:, half * d:(half + 1) * d] = rows[pl.ds(k, PAST_LEN, stride=DIFF_HEADS), :].astype(BF16)

    o = _dot(ap, vh_ref[...]) + _dot(an, vn_ref[...].astype(BF16))
    o_ref[...] = (_rms_rows(o, gs_ref[...]) * (1.0 - lam_init)).astype(BF16)


def diff_attn_sample(qkv, cache_k, cache_v, lam_p, g_sub, lam_init):
    w = 2 * DIFF_HEAD_DIM
    r0 = N_PROMPT // DEC_SEQ
    kern = functools.partial(_diff_sample_kernel, lam_init=lam_init)
    k_rows = cache_k.reshape(DEC_BATCH, PAST_LEN, 2 * DIFF_HEADS, 1, DIFF_HEAD_DIM)
    k_spec = lambda m: pl.BlockSpec((None, PAST_LEN, None, 1, DIFF_HEAD_DIM), lambda b, h: (b, 0, 2 * h + m, 0, 0))
    v_spec = lambda half: pl.BlockSpec((None, PAST_LEN, DIFF_HEADS, DIFF_HEAD_DIM), lambda b, h: (b, 0, 0, half))
    return pl.pallas_call(
        kern,
        out_shape=jax.ShapeDtypeStruct((N_SAMPLE, D_MODEL), BF16),
        grid=(DEC_BATCH, DIFF_HEADS),
        in_specs=[
            pl.BlockSpec((DEC_SEQ, w), lambda b, h: (r0 + b, h)),
            pl.BlockSpec((DEC_SEQ, w), lambda b, h: (r0 + b, DIFF_HEADS + h)),
            pl.BlockSpec((DEC_SEQ, w), lambda b, h: (r0 + b, 2 * DIFF_HEADS + h)),
            k_spec(0),
            k_spec(1),
            v_spec(0),
            v_spec(1),
            pl.BlockSpec((4, DIFF_HEAD_DIM), lambda b, h: (0, 0)),
            pl.BlockSpec((1, w), lambda b, h: (0, 0)),
        ],
        out_specs=pl.BlockSpec((DEC_SEQ, w), lambda b, h: (b, h)),
        scratch_shapes=[pltpu.VMEM((PAST_LEN, w), BF16)],
        compiler_params=_params("parallel", "arbitrary"),
        name="diff_attn_sample",
    )(qkv, qkv, qkv, k_rows, k_rows, cache_v, cache_v, lam_p, g_sub.reshape(1, w))


BAND_SCALE = BAND_HEAD_DIM ** -0.5
BAND_TQ = 256
BAND_E = 1024


def _band_bias(e_row, tq):
    eb = jnp.broadcast_to(e_row, (tq, BAND_E))
    return pltpu.roll(eb, BAND_E - (BAND_TQ - 1), 1, stride=1, stride_axis=0)


def _band_prompt_kernel(q_ref, k_ref, v_ref, e_ref, o_ref):
    tq = BAND_TQ
    win = BAND_WINDOW + tq
    bias = _band_bias(e_ref[0], tq)[:, :win]
    qc = lax.broadcasted_iota(jnp.int32, (tq, win), 0) // CHUNK
    jc = lax.broadcasted_iota(jnp.int32, (tq, win), 1) // CHUNK
    bias = jnp.where(jnp.logical_and(jc >= qc, jc <= qc + BAND_WINDOW // CHUNK), bias, NEG)
    for qb in range(SEQ // tq):
        q0 = qb * tq
        k0 = max(0, q0 - BAND_WINDOW)
        nk = q0 + tq - k0
        rows = slice(q0, q0 + tq)
        sc = _dot_nt(q_ref[rows, :].astype(BF16), k_ref[k0:k0 + nk, :].astype(BF16)) * BAND_SCALE
        sc = sc + bias[:, win - nk:]
        e = jnp.exp(sc - jnp.max(sc, axis=-1, keepdims=True))
        p = (e / jnp.sum(e, axis=-1, keepdims=True)).astype(BF16)
        o_ref[rows, :] = _dot(p, v_ref[k0:k0 + nk, :].astype(BF16)).astype(BF16)


def _band_e_rows(table):
    edge = jnp.broadcast_to(table[:, 2 * REL_CLIP:], (BAND_HEADS, BAND_E - (2 * REL_CLIP + 1)))
    return jnp.concatenate([edge, table[:, ::-1]], axis=1).reshape(BAND_HEADS, 1, BAND_E)


def band_attn_prompt(qkv, table):
    d = BAND_HEAD_DIM
    return pl.pallas_call(
        _band_prompt_kernel,
        out_shape=jax.ShapeDtypeStruct((N_PROMPT, D_MODEL), BF16),
        grid=(BATCH, BAND_HEADS),
        in_specs=[
            pl.BlockSpec((SEQ, d), lambda b, h: (b, h)),
            pl.BlockSpec((SEQ, d), lambda b, h: (b, BAND_HEADS + h)),
            pl.BlockSpec((SEQ, d), lambda b, h: (b, 2 * BAND_HEADS + h)),
            pl.BlockSpec((1, 1, BAND_E), lambda b, h: (h, 0, 0)),
        ],
        out_specs=pl.BlockSpec((SEQ, d), lambda b, h: (b, h)),
        compiler_params=_params("parallel", "parallel"),
        name="band_attn_prompt",
    )(qkv, qkv, qkv, _band_e_rows(table))


def _band_sample_kernel(q_ref, kn_ref, vn_ref, kc_ref, vc_ref, e_ref, o_ref):
    t = DEC_SEQ
    d = BAND_HEAD_DIM
    kc_rows = kc_ref.reshape(BAND_WINDOW * BAND_HEADS, d)
    vc_rows = vc_ref.reshape(BAND_WINDOW * BAND_HEADS, d)
    for h in range(BAND_HEADS):
        cols = slice(h * d, (h + 1) * d)
        bias = _band_bias(e_ref[h], t)
        q = q_ref[:, cols].astype(BF16)
        kc = kc_rows[pl.ds(h, BAND_WINDOW, stride=BAND_HEADS), :]
        vc = vc_rows[pl.ds(h, BAND_WINDOW, stride=BAND_HEADS), :]
        sp = _dot_nt(q, kc.astype(BF16)) * BAND_SCALE + bias[:, :BAND_WINDOW]
        sn = _dot_nt(q, kn_ref[:, cols].astype(BF16)) * BAND_SCALE + bias[:, BAND_WINDOW:BAND_WINDOW + t]
        mx = jnp.maximum(jnp.max(sp, axis=-1, keepdims=True), jnp.max(sn, axis=-1, keepdims=True))
        ep = jnp.exp(sp - mx)
        en = jnp.exp(sn - mx)
        den = jnp.sum(ep, axis=-1, keepdims=True) + jnp.sum(en, axis=-1, keepdims=True)
        o = (_dot((ep / den).astype(BF16), vc.astype(BF16))
             + _dot((en / den).astype(BF16), vn_ref[:, cols].astype(BF16)))
        o_ref[:, cols] = o.astype(BF16)


def band_attn_sample(qkv, cache_k, cache_v, table):
    d = BAND_HEAD_DIM
    r0 = N_PROMPT // DEC_SEQ
    rows_view = (DEC_BATCH, BAND_WINDOW, BAND_HEADS, 1, d)
    cache_spec = pl.BlockSpec((None, BAND_WINDOW, BAND_HEADS, 1, d), lambda b: (b, 0, 0, 0, 0))
    return pl.pallas_call(
        _band_sample_kernel,
        out_shape=jax.ShapeDtypeStruct((N_SAMPLE, D_MODEL), BF16),
        grid=(DEC_BATCH,),
        in_specs=[
            pl.BlockSpec((DEC_SEQ, D_MODEL), lambda b: (r0 + b, 0)),
            pl.BlockSpec((DEC_SEQ, D_MODEL), lambda b: (r0 + b, 1)),
            pl.BlockSpec((DEC_SEQ, D_MODEL), lambda b: (r0 + b, 2)),
            cache_spec,
            cache_spec,
            pl.BlockSpec((BAND_HEADS, 1, BAND_E), lambda b: (0, 0, 0)),
        ],
        out_specs=pl.BlockSpec((DEC_SEQ, D_MODEL), lambda b: (b, 0)),
        compiler_params=_params("parallel"),
        name="band_attn_sample",
    )(qkv, qkv, qkv, cache_k.reshape(rows_view), cache_v.reshape(rows_view), _band_e_rows(table))


def kernel(x_prompt, x_sample, p_prompt, p_sample, state_ssm, state_conv, cache_k_diff, cache_v_diff,
           cache_k_band, cache_v_band, g_ffn, w_ffn_gate, w_ffn_up, w_ffn_down, g_mix,
           ssd_w_in, ssd_conv_w, ssd_conv_b, ssd_dt_bias, ssd_a_log, ssd_d, ssd_g_norm, ssd_w_out,
           diff_w_in, diff_g_q, diff_g_k, diff_lambda, diff_g_sub, diff_w_out,
           band_w_in, band_g_q, band_g_k, band_rel_bias, band_w_out,
           g_ple, w_ple, w_ple_gate):
    h = jnp.concatenate([x_prompt.reshape(N_PROMPT, D_MODEL), x_sample.reshape(N_SAMPLE, D_MODEL)], axis=0)
    p_all = jnp.concatenate([p_prompt.reshape(DEPTH, N_PROMPT, PLE_DIM),
                             p_sample.reshape(DEPTH, N_SAMPLE, PLE_DIM)], axis=1)
    bf = lambda w: w.astype(BF16)
    wg_all, wu_all, wd_all = w_ffn_gate, w_ffn_up, w_ffn_down
    w_ple_all, w_gate_all = w_ple, w_ple_gate
    ssd_w_main = bf(ssd_w_in)
    ssd_w_dt = bf(jnp.pad(ssd_w_in[:, :, SSD_MAIN_WIDTH:], ((0, 0), (0, 0), (0, 128 - SSD_HEADS))))
    ssd_w_out_all, diff_w_in_all, diff_w_out_all = bf(ssd_w_out), diff_w_in, diff_w_out
    band_w_in_all, band_w_out_all = band_w_in, band_w_out

    def prompt_rows(a):
        return a[:N_PROMPT].reshape(BATCH, SEQ, -1)

    def sample_rows(a):
        return a[N_PROMPT:].reshape(DEC_BATCH, DEC_SEQ, -1)

    ssm_p, conv_p, ssm_s, conv_s = [], [], [], []
    kd_p = vd_p = kb_p = vb_p = kd_s = vd_s = kb_s = vb_s = None
    for i in range(DEPTH):
        kind, j = i % 3, i // 3
        h = ffn_half(h, g_ffn, wg_all, wu_all, wd_all, i, 0)
        if kind == 0:
            zx, dt_raw = inproj(h, g_mix[i], ssd_w_main, j, n_cols=SSD_MAIN_WIDTH, w_extra=ssd_w_dt)
            y, st, cvt = ssd_core(zx, dt_raw, state_ssm, state_conv, ssd_conv_w, j, ssd_conv_b[j], ssd_dt_bias[j],
                                  ssd_a_log[j], ssd_d[j], ssd_g_norm[j])
            h = outproj(h, y, ssd_w_out_all, j)
            ssm_p.append(st[:BATCH])
            ssm_s.append(st[BATCH:])
            conv_p.append(cvt[:BATCH])
            conv_s.append(cvt[BATCH:])
        elif kind == 1:
            lam_init = 0.8 - 0.6 * math.exp(-0.3 * i)
            gains = jnp.concatenate([jnp.tile(diff_g_q[j].reshape(-1), DIFF_HEADS),
                                     jnp.tile(diff_g_k[j].reshape(-1), DIFF_HEADS),
                                     jnp.ones((D_MODEL,), F32)]).reshape(1, 3 * D_MODEL)
            qkv = inproj(h, g_mix[i], diff_w_in_all, j, gains, norm_cols=2 * D_MODEL)
            o_p = diff_attn_prompt(qkv, diff_lambda[j], diff_g_sub[j], lam_init)
            o_s = diff_attn_sample(qkv, cache_k_diff[j], cache_v_diff[j], diff_lambda[j], diff_g_sub[j], lam_init)
            h = outproj(h, (o_p, o_s), diff_w_out_all, j)
            k_new, v_new = qkv[:, D_MODEL:2 * D_MODEL], qkv[:, 2 * D_MODEL:]
            kd_p = prompt_rows(k_new).reshape(1, BATCH, SEQ, DIFF_HEADS, 2, DIFF_HEAD_DIM)
            vd_p = prompt_rows(v_new).reshape(1, BATCH, SEQ, DIFF_HEADS, 2 * DIFF_HEAD_DIM)
            kd_s = sample_rows(k_new).reshape(1, DEC_BATCH, DEC_SEQ, DIFF_HEADS, 2, DIFF_HEAD_DIM)
            vd_s = sample_rows(v_new).reshape(1, DEC_BATCH, DEC_SEQ, DIFF_HEADS, 2 * DIFF_HEAD_DIM)
        else:
            gains = jnp.concatenate([jnp.tile(band_g_q[j], BAND_HEADS), jnp.tile(band_g_k[j], BAND_HEADS),
                                     jnp.ones((D_MODEL,), F32)]).reshape(1, 3 * D_MODEL)
            qkv = inproj(h, g_mix[i], band_w_in_all, j, gains, norm_cols=2 * D_MODEL)
            o_p = band_attn_prompt(qkv, band_rel_bias[j])
            o_s = band_attn_sample(qkv, cache_k_band[j], cache_v_band[j], band_rel_bias[j])
            h = outproj(h, (o_p, o_s), band_w_out_all, j)
            tail = jnp.stack([lax.slice(qkv, ((b + 1) * SEQ - BAND_WINDOW, D_MODEL), ((b + 1) * SEQ, 3 * D_MODEL))
                              for b in range(BATCH)])
            kb_p = tail[:, :, :D_MODEL].reshape(1, BATCH, BAND_WINDOW, BAND_HEADS, BAND_HEAD_DIM)
            vb_p = tail[:, :, D_MODEL:].reshape(1, BATCH, BAND_WINDOW, BAND_HEADS, BAND_HEAD_DIM)
            new_s = qkv[N_PROMPT:, D_MODEL:]
            kb_s = new_s[:, :D_MODEL].reshape(1, DEC_BATCH, DEC_SEQ, BAND_HEADS, BAND_HEAD_DIM)
            vb_s = new_s[:, D_MODEL:].reshape(1, DEC_BATCH, DEC_SEQ, BAND_HEADS, BAND_HEAD_DIM)
        h = ffn_half(h, g_ffn, wg_all, wu_all, wd_all, i, 1)
        h = ple_add(h, p_all, g_ple, w_ple_all, w_gate_all, i, split=(i == DEPTH - 1))
    h_p, h_s = h
    return (h_p.reshape(BATCH, SEQ, D_MODEL), h_s.reshape(DEC_BATCH, DEC_SEQ, D_MODEL),
            jnp.stack(ssm_p), jnp.stack(conv_p), kd_p, vd_p, kb_p, vb_p,
            jnp.stack(ssm_s), jnp.stack(conv_s), kd_s, vd_s, kb_s, vb_s)
```

```python
import functools
import math

import jax
import jax.numpy as jnp
from jax import lax
from jax.experimental import pallas as pl
from jax.experimental.pallas import tpu as pltpu

F32 = jnp.float32
BF16 = jnp.bfloat16

D_MODEL = 2048
BATCH = 4
SEQ = 2048
DEPTH = 4
DEC_BATCH = 8
DEC_SEQ = 64
PAST_LEN = 2048
CHUNK = 64
NORM_EPS = 1e-6
FFN_DIM = 5632
PLE_DIM = 256
SSD_INNER = 4096
SSD_HEADS = 64
SSD_HEAD_DIM = 64
SSD_GROUPS = 8
SSD_STATE = 128
SSD_GROUP_WIDTH = SSD_INNER // SSD_GROUPS
SSD_CONV_DIM = SSD_INNER + 2 * SSD_GROUPS * SSD_STATE
SSD_MAIN_WIDTH = SSD_INNER + SSD_CONV_DIM
SSD_EXT_WIDTH = SSD_MAIN_WIDTH + SSD_INNER
DIFF_HEADS = 8
DIFF_HEAD_DIM = 128
BAND_HEADS = 16
BAND_HEAD_DIM = 128
BAND_WINDOW = 512
REL_CLIP = 256

N_PROMPT = BATCH * SEQ
N_SAMPLE = DEC_BATCH * DEC_SEQ
N_TOK = N_PROMPT + N_SAMPLE
CHUNKS_PER_SEQ = SEQ // CHUNK
N_PROMPT_CHUNKS = N_PROMPT // CHUNK
N_CHUNKS = N_TOK // CHUNK

TM_WIDE = 1088
TM = 544
TM_SPLIT = 512
INPROJ_NORM_CHUNKS = 2
N_PROMPT_TILES = N_PROMPT // TM_SPLIT
VMEM_LIMIT = 56 * 1024 * 1024
NEG = -1e30


def _params(*sem, vmem=VMEM_LIMIT):
    return pltpu.CompilerParams(dimension_semantics=sem, vmem_limit_bytes=vmem)


def _rms_rows(x, g):
    ms = jnp.mean(x * x, axis=-1, keepdims=True)
    return x * lax.rsqrt(ms + NORM_EPS) * g


def _dot(a, b):
    return jnp.dot(a, b, preferred_element_type=F32)


def _dot_nt(a, b):
    return lax.dot_general(a, b, (((1,), (1,)), ((), ())), preferred_element_type=F32)


def _dot_tn(a, b):
    return lax.dot_general(a, b, (((0,), (0,)), ((), ())), preferred_element_type=F32)


def _ffn_step(first, h_ref, g_ref, wg_ref, wu_ref, wd_ref, o_ref, xn_ref):
    @pl.when(first)
    def _():
        x = h_ref[...]
        xn_ref[...] = _rms_rows(x, g_ref[...]).astype(BF16)
        o_ref[...] = x

    xn = xn_ref[...]
    gate = _dot(xn, wg_ref[...])
    up = _dot(xn, wu_ref[...])
    act = (0.5 * (gate * jax.nn.sigmoid(gate)) * up).astype(BF16)
    o_ref[...] += _dot(act, wd_ref[...])


def _ffn_first_kernel(h_ref, g_ref, wg_ref, wu_ref, wd_ref, o_ref, wg_bf_ref, wu_bf_ref, wd_bf_ref, xn_ref):
    wg_bf_ref[...] = wg_ref[...].astype(BF16)
    wu_bf_ref[...] = wu_ref[...].astype(BF16)
    wd_bf_ref[...] = wd_ref[...].astype(BF16)
    _ffn_step(pl.program_id(0) == 0, h_ref, g_ref, wg_bf_ref, wu_bf_ref, wd_bf_ref, o_ref, xn_ref)


def _ffn_rest_kernel(h_ref, g_ref, wg_ref, wu_ref, wd_ref, partial_ref, o_ref, xn_ref):
    del partial_ref
    _ffn_step(pl.program_id(1) == 0, h_ref, g_ref, wg_ref, wu_ref, wd_ref, o_ref, xn_ref)


def ffn_half(h, g_all, wg_all, wu_all, wd_all, layer, half, *, tf=512, tf_first=256):
    tm = TM
    tm_first = TM_WIDE
    skip = tm_first // tm
    g4 = g_all.reshape(DEPTH, 2, 1, D_MODEL)
    g_spec1 = pl.BlockSpec((None, None, 1, D_MODEL), lambda f: (layer, half, 0, 0))
    partial, wg_bf, wu_bf, wd_bf = pl.pallas_call(
        _ffn_first_kernel,
        out_shape=(jax.ShapeDtypeStruct((N_TOK, D_MODEL), F32),
                   jax.ShapeDtypeStruct((D_MODEL, FFN_DIM), BF16),
                   jax.ShapeDtypeStruct((D_MODEL, FFN_DIM), BF16),
                   jax.ShapeDtypeStruct((FFN_DIM, D_MODEL), BF16)),
        grid=(FFN_DIM // tf_first,),
        in_specs=[
            pl.BlockSpec((tm_first, D_MODEL), lambda f: (0, 0), pipeline_mode=pl.Buffered(1)),
            g_spec1,
            pl.BlockSpec((None, None, D_MODEL, tf_first), lambda f: (layer, half, 0, f)),
            pl.BlockSpec((None, None, D_MODEL, tf_first), lambda f: (layer, half, 0, f)),
            pl.BlockSpec((None, None, tf_first, D_MODEL), lambda f: (layer, half, f, 0)),
        ],
        out_specs=(pl.BlockSpec((tm_first, D_MODEL), lambda f: (0, 0)),
                   pl.BlockSpec((D_MODEL, tf_first), lambda f: (0, f)),
                   pl.BlockSpec((D_MODEL, tf_first), lambda f: (0, f)),
                   pl.BlockSpec((tf_first, D_MODEL), lambda f: (f, 0))),
        scratch_shapes=[pltpu.VMEM((tm_first, D_MODEL), BF16)],
        compiler_params=_params("arbitrary"),
        name="ffn_first",
    )(h, g4, wg_all, wu_all, wd_all)
    return pl.pallas_call(
        _ffn_rest_kernel,
        out_shape=jax.ShapeDtypeStruct((N_TOK, D_MODEL), F32),
        grid=(N_TOK // tm - skip, FFN_DIM // tf),
        in_specs=[
            pl.BlockSpec((tm, D_MODEL), lambda i, f: (i + skip, 0)),
            pl.BlockSpec((None, None, 1, D_MODEL), lambda i, f: (layer, half, 0, 0)),
            pl.BlockSpec((D_MODEL, tf), lambda i, f: (0, f)),
            pl.BlockSpec((D_MODEL, tf), lambda i, f: (0, f)),
            pl.BlockSpec((tf, D_MODEL), lambda i, f: (f, 0)),
            pl.BlockSpec(memory_space=pl.ANY),
        ],
        out_specs=pl.BlockSpec((tm, D_MODEL), lambda i, f: (i + skip, 0)),
        scratch_shapes=[pltpu.VMEM((tm, D_MODEL), BF16)],
        input_output_aliases={5: 0},
        compiler_params=_params("parallel", "arbitrary"),
        name="ffn_rest",
    )(h, g4, wg_bf, wu_bf, wd_bf, partial)


def _inproj_kernel(h_ref, g_ref, w_ref, hg_ref, *rest, norm_tiles, tn, has_extra, first, n_alias=0):
    rest = list(rest)
    wx_ref = rest.pop(0) if has_extra else None
    del rest[:n_alias]
    o_ref = rest.pop(0)
    ox_ref = rest.pop(0) if has_extra else None
    wbf_ref = rest.pop(0) if first else None
    (xn_ref,) = rest
    j = pl.program_id(0 if first else 1)

    @pl.when(j == 0)
    def _():
        xn = _rms_rows(h_ref[...], g_ref[...]).astype(BF16)
        xn_ref[...] = xn
        if has_extra:
            ox_ref[...] = _dot(xn, wx_ref[...])

    if first:
        wbf_ref[...] = w_ref[...].astype(BF16)
        w_ref = wbf_ref

    def plain():
        o_ref[...] = _dot(xn_ref[...], w_ref[...])

    if norm_tiles == 0:
        plain()
        return
    pl.when(j >= norm_tiles)(plain)

    @pl.when(j < norm_tiles)
    def _():
        rc = xn_ref.shape[0] // INPROJ_NORM_CHUNKS
        for r in range(INPROJ_NORM_CHUNKS):
            rows = slice(r * rc, (r + 1) * rc)
            y = _dot(xn_ref[rows, :], w_ref[...])
            for c in range(tn // 128):
                sl = slice(c * 128, (c + 1) * 128)
                o_ref[rows, sl] = _rms_rows(y[:, sl], hg_ref[:, sl])


def inproj(h, g, w_all, layer, head_gain=None, *, n_cols=None, norm_cols=0, w_extra=None, tn=1024, tn_first=512):
    tm = TM_WIDE
    n = w_all.shape[2] if n_cols is None else n_cols
    if head_gain is None:
        head_gain = jnp.ones((1, n), F32)
    has_extra = w_extra is not None
    g2 = g.reshape(1, D_MODEL)
    nx = w_extra.shape[2] if has_extra else 0
    out_shapes = [jax.ShapeDtypeStruct((N_TOK, n), F32)]
    if has_extra:
        out_shapes.append(jax.ShapeDtypeStruct((N_TOK, nx), F32))

    partials = []
    if w_all.dtype == BF16:
        w_bf = w_all
        w_spec = pl.BlockSpec((None, D_MODEL, tn), lambda i, j: (layer, 0, j))
    else:
        kern = functools.partial(_inproj_kernel, norm_tiles=norm_cols // tn_first, tn=tn_first, has_extra=has_extra,
                                 first=True)
        in_specs = [
            pl.BlockSpec((tm, D_MODEL), lambda j: (0, 0)),
            pl.BlockSpec((1, D_MODEL), lambda j: (0, 0)),
            pl.BlockSpec((None, D_MODEL, tn_first), lambda j: (layer, 0, j)),
            pl.BlockSpec((1, tn_first), lambda j: (0, j)),
        ]
        args = [h, g2, w_all, head_gain]
        out_specs = [pl.BlockSpec((tm, tn_first), lambda j: (0, j))]
        if has_extra:
            in_specs.append(pl.BlockSpec((None, D_MODEL, nx), lambda j: (layer, 0, 0)))
            args.append(w_extra)
            out_specs.append(pl.BlockSpec((tm, nx), lambda j: (0, 0)))
        out_specs.append(pl.BlockSpec((D_MODEL, tn_first), lambda j: (0, j)))
        *partials, w_bf = pl.pallas_call(
            kern,
            out_shape=tuple(out_shapes) + (jax.ShapeDtypeStruct((D_MODEL, n), BF16),),
            grid=(n // tn_first,),
            in_specs=in_specs,
            out_specs=tuple(out_specs),
            scratch_shapes=[pltpu.VMEM((tm, D_MODEL), BF16)],
            compiler_params=_params("arbitrary"),
            name="inproj_first",
        )(*args)
        w_spec = pl.BlockSpec((D_MODEL, tn), lambda i, j: (0, j))

    r0 = len(partials) and 1
    kern = functools.partial(_inproj_kernel, norm_tiles=norm_cols // tn, tn=tn, has_extra=has_extra, first=False,
                             n_alias=len(partials))
    in_specs = [
        pl.BlockSpec((tm, D_MODEL), lambda i, j: (i + r0, 0)),
        pl.BlockSpec((1, D_MODEL), lambda i, j: (0, 0)),
        w_spec,
        pl.BlockSpec((1, tn), lambda i, j: (0, j)),
    ]
    args = [h, g2, w_bf, head_gain]
    out_specs = [pl.BlockSpec((tm, tn), lambda i, j: (i + r0, j))]
    if has_extra:
        in_specs.append(pl.BlockSpec((None, D_MODEL, nx), lambda i, j: (layer, 0, 0)))
        args.append(w_extra)
        out_specs.append(pl.BlockSpec((tm, nx), lambda i, j: (i + r0, 0)))
    n_in = len(in_specs)
    in_specs += [pl.BlockSpec(memory_space=pl.ANY)] * len(partials)
    args += partials
    outs = pl.pallas_call(
        kern,
        out_shape=tuple(out_shapes),
        grid=(N_TOK // tm - r0, n // tn),
        in_specs=in_specs,
        out_specs=tuple(out_specs),
        scratch_shapes=[pltpu.VMEM((tm, D_MODEL), BF16)],
        input_output_aliases={n_in + k: k for k in range(len(partials))},
        compiler_params=_params("parallel", "arbitrary"),
        name="inproj_main",
    )(*args)
    return outs if has_extra else outs[0]


def _cast_once(first, w_ref, wbf_ref):
    @pl.when(first)
    def _():
        wbf_ref[...] = w_ref[...].astype(BF16)


def _outproj_kernel(a_ref, w_ref, h_ref, o_ref, *scratch):
    wbf_ref = w_ref
    if scratch:
        (wbf_ref,) = scratch
        _cast_once(pl.program_id(1) == 0, w_ref, wbf_ref)
    o_ref[...] = h_ref[...] + _dot(a_ref[...], wbf_ref[...])


def _outproj_split_kernel(ap_ref, as_ref, w_ref, h_ref, o_ref, wbf_ref):
    i = pl.program_id(1)
    _cast_once(i == 0, w_ref, wbf_ref)

    @pl.when(i < N_PROMPT_TILES)
    def _():
        o_ref[...] = h_ref[...] + _dot(ap_ref[...], wbf_ref[...])

    @pl.when(i >= N_PROMPT_TILES)
    def _():
        o_ref[...] = h_ref[...] + _dot(as_ref[...], wbf_ref[...])


def outproj(h, a, w_all, layer):
    tm = TM_SPLIT
    k = w_all.shape[1]
    tn = 1024
    scratch = [] if w_all.dtype == BF16 else [pltpu.VMEM((k, tn), BF16)]
    w_spec = pl.BlockSpec((None, k, tn), lambda j, i: (layer, 0, j))
    h_spec = pl.BlockSpec((tm, tn), lambda j, i: (i, j))
    if isinstance(a, tuple):
        kern = _outproj_split_kernel
        a_specs = [pl.BlockSpec((tm, k), lambda j, i: (jnp.minimum(i, N_PROMPT_TILES - 1), 0)),
                   pl.BlockSpec((tm, k), lambda j, i: (0, 0))]
        args = list(a)
    else:
        kern = _outproj_kernel
        a_specs = [pl.BlockSpec((tm, k), lambda j, i: (i, 0))]
        args = [a]
    return pl.pallas_call(
        kern,
        out_shape=jax.ShapeDtypeStruct((N_TOK, D_MODEL), F32),
        grid=(D_MODEL // tn, N_TOK // tm),
        in_specs=a_specs + [w_spec, h_spec],
        out_specs=h_spec,
        scratch_shapes=scratch,
        compiler_params=_params("parallel", "arbitrary"),
        name="outproj",
    )(*args, w_all, h)


def _ple_update(h_ref, p_ref, g_ref, wp_ref, wgt_ref):
    x = h_ref[...]
    xn = _rms_rows(x, g_ref[...]).astype(BF16)
    gate = jax.nn.sigmoid(_dot(xn, wgt_ref[...]))
    emb = _dot(p_ref[...].astype(BF16), wp_ref[...])
    return x + emb * gate


def _ple_kernel(h_ref, p_ref, g_ref, wp_ref, wgt_ref, o_ref, wp_bf_ref, wgt_bf_ref):
    first = pl.program_id(0) == 0
    _cast_once(first, wp_ref, wp_bf_ref)
    _cast_once(first, wgt_ref, wgt_bf_ref)
    o_ref[...] = _ple_update(h_ref, p_ref, g_ref, wp_bf_ref, wgt_bf_ref)


def _ple_split_kernel(h_ref, p_ref, g_ref, wp_ref, wgt_ref, op_ref, os_ref, wp_bf_ref, wgt_bf_ref):
    i = pl.program_id(0)
    _cast_once(i == 0, wp_ref, wp_bf_ref)
    _cast_once(i == 0, wgt_ref, wgt_bf_ref)

    @pl.when(i < N_PROMPT_TILES)
    def _():
        op_ref[...] = _ple_update(h_ref, p_ref, g_ref, wp_bf_ref, wgt_bf_ref)

    @pl.when(i >= N_PROMPT_TILES)
    def _():
        os_ref[...] = _ple_update(h_ref, p_ref, g_ref, wp_bf_ref, wgt_bf_ref)


def ple_add(h, p_all, g_all, w_ple_all, w_gate_all, layer, *, split=False):
    tm = TM_SPLIT if split else TM
    in_specs = [
        pl.BlockSpec((tm, D_MODEL), lambda i: (i, 0)),
        pl.BlockSpec((None, tm, PLE_DIM), lambda i: (layer, i, 0)),
        pl.BlockSpec((None, 1, D_MODEL), lambda i: (layer, 0, 0)),
        pl.BlockSpec((None, PLE_DIM, D_MODEL), lambda i: (layer, 0, 0), pipeline_mode=pl.Buffered(1)),
        pl.BlockSpec((None, D_MODEL, D_MODEL), lambda i: (layer, 0, 0), pipeline_mode=pl.Buffered(1)),
    ]
    if split:
        kern = _ple_split_kernel
        out_shape = (jax.ShapeDtypeStruct((N_PROMPT, D_MODEL), F32), jax.ShapeDtypeStruct((N_SAMPLE, D_MODEL), F32))
        out_specs = (pl.BlockSpec((tm, D_MODEL), lambda i: (jnp.minimum(i, N_PROMPT_TILES - 1), 0)),
                     pl.BlockSpec((tm, D_MODEL), lambda i: (0, 0)))
    else:
        kern = _ple_kernel
        out_shape = jax.ShapeDtypeStruct((N_TOK, D_MODEL), F32)
        out_specs = pl.BlockSpec((tm, D_MODEL), lambda i: (i, 0))
    return pl.pallas_call(
        kern,
        out_shape=out_shape,
        grid=(N_TOK // tm,),
        in_specs=in_specs,
        out_specs=out_specs,
        scratch_shapes=[pltpu.VMEM((PLE_DIM, D_MODEL), BF16), pltpu.VMEM((D_MODEL, D_MODEL), BF16)],
        compiler_params=_params("arbitrary"),
        name="ple_add",
    )(h, p_all, g_all.reshape(DEPTH, 1, D_MODEL), w_ple_all, w_gate_all)


def _ssd_kernel(zx_ref, dtr_ref, st0_ref, cv0_ref, cw_ref, cb_ref, dtb_ref, alog_ref, rep_ref, dsk_ref, gn_ref,
                y_ref, st_ref, cvt_ref, s_scr, xpad, act, pcum, dts):
    s = pl.program_id(0)
    is_sample = s >= N_PROMPT_CHUNKS
    pos = s % CHUNKS_PER_SEQ
    first = jnp.logical_or(is_sample, pos == 0)
    last = jnp.logical_or(is_sample, pos == CHUNKS_PER_SEQ - 1)

    @pl.when(jnp.logical_and(first, jnp.logical_not(is_sample)))
    def _():
        s_scr[...] = jnp.zeros_like(s_scr)
        xpad[0:8, :] = jnp.zeros((8, SSD_CONV_DIM), F32)

    @pl.when(is_sample)
    def _():
        for g in range(SSD_GROUPS):
            s_scr[g] = st0_ref[0, g].T
        xpad[0:8, :] = jnp.zeros((8, SSD_CONV_DIM), F32)
        xpad[5:8, :] = cv0_ref[0]

    xpad[8:72, :] = zx_ref[:, SSD_INNER:SSD_MAIN_WIDTH]
    conv = cb_ref[...] + cw_ref[3:4, :] * xpad[8:72, :]
    conv = conv + cw_ref[2:3, :] * xpad[7:71, :]
    conv = conv + cw_ref[1:2, :] * xpad[6:70, :]
    conv = conv + cw_ref[0:1, :] * xpad[5:69, :]
    act[...] = conv * jax.nn.sigmoid(conv)
    xpad[0:8, :] = xpad[64:72, :]

    x = dtr_ref[...] + dtb_ref[...]
    dt = jnp.maximum(x, 0.0) + jnp.log1p(jnp.exp(-jnp.abs(x)))
    run = dt * (-jnp.exp(alog_ref[...]))
    row = lax.broadcasted_iota(jnp.int32, (CHUNK, 128), 0)
    for k in (1, 2, 4, 8, 16, 32):
        run = run + jnp.where(row >= k, pltpu.roll(run, k, 0), 0.0)
    both = jnp.concatenate([dt, run], axis=0)
    hi = both.astype(BF16)
    r1 = both - hi.astype(F32)
    mid = r1.astype(BF16)
    lo = (r1 - mid.astype(F32)).astype(BF16)
    wide = _dot(jnp.concatenate([hi, mid, lo], axis=0), rep_ref[...])
    wide = (wide[0:2 * CHUNK] + wide[2 * CHUNK:4 * CHUNK]) + wide[4 * CHUNK:6 * CHUNK]
    dts[...] = wide[0:CHUNK]
    pcum[...] = wide[CHUNK:2 * CHUNK]

    gw = SSD_GROUP_WIDTH
    row_g = lax.broadcasted_iota(jnp.int32, (CHUNK, gw), 0)
    lane_g = lax.broadcasted_iota(jnp.int32, (CHUNK, gw), 1) % CHUNK
    diag = row_g == lane_g
    causal = row_g >= lane_g
    r4 = lax.broadcasted_iota(jnp.int32, (256, 256), 0) // CHUNK
    c4 = lax.broadcasted_iota(jnp.int32, (256, 256), 1) // SSD_HEAD_DIM
    head_diag = r4 == c4

    for g in range(SSD_GROUPS):
        cs = slice(g * gw, (g + 1) * gw)
        p_g = pcum[:, cs]
        dt_g = dts[:, cs]
        x_g = act[:, cs]
        b_g = act[:, SSD_INNER + g * SSD_STATE:SSD_INNER + (g + 1) * SSD_STATE].astype(BF16)
        c_g = act[:, SSD_INNER + SSD_GROUPS * SSD_STATE + g * SSD_STATE:
                  SSD_INNER + SSD_GROUPS * SSD_STATE + (g + 1) * SSD_STATE].astype(BF16)
        p_s = jnp.sum(jnp.where(diag, p_g, 0.0), axis=0, keepdims=True)
        dt_s = jnp.sum(jnp.where(diag, dt_g, 0.0), axis=0, keepdims=True)
        p_last = p_g[CHUNK - 1:CHUNK, :]
        cb = _dot_nt(c_g, jnp.concatenate([b_g] * 8, axis=0))
        m = (cb * jnp.exp(jnp.where(causal, p_g - p_s, NEG)) * dt_s).astype(BF16)
        x_bf = x_g.astype(BF16)
        halves = []
        for hh in range(2):
            hs = slice(hh * 256, (hh + 1) * 256)
            xh = x_bf[:, hs]
            x_bd = jnp.where(head_diag, jnp.concatenate([xh] * 4, axis=0), jnp.zeros((), BF16))
            halves.append(_dot(m[:, hs], x_bd))
        y = jnp.concatenate(halves, axis=1)
        st = s_scr[g]
        y = y + _dot(c_g, st.astype(BF16)) * jnp.exp(p_g)
        wx = (jnp.exp(p_last - p_g) * dt_g * x_g).astype(BF16)
        s_scr[g] = st * jnp.exp(p_last) + _dot_tn(b_g, wx)
        y = y + dsk_ref[:, cs] * x_g
        z = zx_ref[:, cs]
        y = y * (z * jax.nn.sigmoid(z))
        y_ref[:, cs] = _rms_rows(y, gn_ref[:, cs]).astype(BF16)

    @pl.when(last)
    def _():
        cvt_ref[0] = xpad[0:8, :]
        for g in range(SSD_GROUPS):
            st_ref[0, g] = s_scr[g].T


def _ssd_seq(s):
    return jnp.where(s < N_PROMPT_CHUNKS, s // CHUNKS_PER_SEQ, BATCH + s - N_PROMPT_CHUNKS)


def ssd_core(zx, dt_raw, state_all, conv_all, conv_w_all, layer, conv_b, dt_bias, a_log, d_skip, g_norm):
    n_seq = BATCH + DEC_BATCH
    pad_heads = lambda v: jnp.pad(v, (0, 128 - SSD_HEADS)).reshape(1, 128)
    samp = lambda s: jnp.maximum(s - N_PROMPT_CHUNKS, 0)
    row_spec = lambda w: pl.BlockSpec((1, w), lambda s: (0, 0))
    st_all = state_all.reshape(-1, DEC_BATCH, SSD_GROUPS, SSD_GROUP_WIDTH, SSD_STATE)
    head_of_channel = jnp.arange(SSD_INNER, dtype=jnp.int32) // SSD_HEAD_DIM
    rep = (jnp.arange(128, dtype=jnp.int32)[:, None] == head_of_channel[None, :]).astype(BF16)
    y, st, cvt = pl.pallas_call(
        _ssd_kernel,
        out_shape=(jax.ShapeDtypeStruct((N_TOK, SSD_INNER), BF16),
                   jax.ShapeDtypeStruct((n_seq, SSD_GROUPS, SSD_GROUP_WIDTH, SSD_STATE), F32),
                   jax.ShapeDtypeStruct((n_seq, 8, SSD_CONV_DIM), F32)),
        grid=(N_CHUNKS,),
        in_specs=[
            pl.BlockSpec((CHUNK, SSD_MAIN_WIDTH), lambda s: (s, 0)),
            pl.BlockSpec((CHUNK, 128), lambda s: (s, 0)),
            pl.BlockSpec((None, 1, SSD_GROUPS, SSD_GROUP_WIDTH, SSD_STATE), lambda s: (layer, samp(s), 0, 0, 0)),
            pl.BlockSpec((None, 1, 3, SSD_CONV_DIM), lambda s: (layer, samp(s), 0, 0)),
            pl.BlockSpec((None, 4, SSD_CONV_DIM), lambda s: (layer, 0, 0)),
            row_spec(SSD_CONV_DIM), row_spec(128), row_spec(128),
            pl.BlockSpec((128, SSD_INNER), lambda s: (0, 0)),
            row_spec(SSD_INNER), row_spec(SSD_INNER),
        ],
        out_specs=(pl.BlockSpec((CHUNK, SSD_INNER), lambda s: (s, 0)),
                   pl.BlockSpec((1, SSD_GROUPS, SSD_GROUP_WIDTH, SSD_STATE), lambda s: (_ssd_seq(s), 0, 0, 0)),
                   pl.BlockSpec((1, 8, SSD_CONV_DIM), lambda s: (_ssd_seq(s), 0, 0))),
        scratch_shapes=[
            pltpu.VMEM((SSD_GROUPS, SSD_STATE, SSD_GROUP_WIDTH), F32),
            pltpu.VMEM((CHUNK + 8, SSD_CONV_DIM), F32),
            pltpu.VMEM((CHUNK, SSD_CONV_DIM), F32),
            pltpu.VMEM((CHUNK, SSD_INNER), F32),
            pltpu.VMEM((CHUNK, SSD_INNER), F32),
        ],
        compiler_params=_params("arbitrary"),
        name="ssd_core",
    )(zx, dt_raw, st_all, conv_all, conv_w_all, conv_b.reshape(1, SSD_CONV_DIM), pad_heads(dt_bias),
      pad_heads(a_log), rep, jnp.repeat(d_skip, SSD_HEAD_DIM).reshape(1, SSD_INNER), g_norm.reshape(1, SSD_INNER))
    return y, st.reshape(n_seq, SSD_HEADS, SSD_HEAD_DIM, SSD_STATE), cvt[:, 5:8]


DIFF_SCALE = DIFF_HEAD_DIM ** -0.5
LOG2E = math.log2(math.e)


def _diff_lambda(lp_ref, lam_init):
    lp = lp_ref[...]
    a = jnp.sum(lp[0:1] * lp[1:2], axis=-1, keepdims=True)
    b = jnp.sum(lp[2:3] * lp[3:4], axis=-1, keepdims=True)
    return jnp.exp(a) - jnp.exp(b) + lam_init


def _alibi_slope(h):
    return jnp.exp2(-(jnp.zeros((1, 1), F32) + (h + 1).astype(F32)))


def _diff_prompt_kernel(q_ref, k_ref, v_ref, lp_ref, gs_ref, o_ref, *, lam_init, tq):
    h = pl.program_id(1)
    lam = _diff_lambda(lp_ref, lam_init)
    slope2 = _alibi_slope(h) * LOG2E
    c = DIFF_SCALE * LOG2E
    d = DIFF_HEAD_DIM
    t_i = lax.broadcasted_iota(jnp.int32, (tq, tq), 0)
    j_i = lax.broadcasted_iota(jnp.int32, (tq, tq), 1)
    bias_own = jnp.where(j_i // CHUNK <= t_i // CHUNK,
                         slope2 * (t_i - jnp.abs(t_i - j_i)).astype(F32), NEG)
    for qb in range(SEQ // tq):
        q0 = qb * tq
        rows = slice(q0, q0 + tq)
        if qb:
            bias_past = slope2 * (lax.broadcasted_iota(jnp.int32, (1, q0), 1) - q0).astype(F32)
        weights = []
        for m in range(2):
            cols = slice(m * d, (m + 1) * d)
            q = q_ref[rows, cols].astype(BF16)
            s_own = _dot_nt(q, k_ref[rows, cols].astype(BF16)) * c + bias_own
            mx = jnp.max(s_own, axis=-1, keepdims=True)
            if qb:
                s_past = _dot_nt(q, k_ref[0:q0, cols].astype(BF16)) * c + bias_past
                mx = jnp.maximum(mx, jnp.max(s_past, axis=-1, keepdims=True))
                e_past = jnp.exp2(s_past - mx)
            e_own = jnp.exp2(s_own - mx)
            den = jnp.sum(e_own, axis=-1, keepdims=True)
            if qb:
                den = den + jnp.sum(e_past, axis=-1, keepdims=True)
            weights.append((e_own, e_past if qb else None, 1.0 / den))
        (eo0, ep0, r0), (eo1, ep1, r1) = weights
        r1 = lam * r1
        o = _dot((eo0 * r0 - eo1 * r1).astype(BF16), v_ref[rows, :].astype(BF16))
        if qb:
            o = o + _dot((ep0 * r0 - ep1 * r1).astype(BF16), v_ref[0:q0, :].astype(BF16))
        o_ref[rows, :] = (_rms_rows(o, gs_ref[...]) * (1.0 - lam_init)).astype(BF16)


def diff_attn_prompt(qkv, lam_p, g_sub, lam_init, *, tq=256):
    w = 2 * DIFF_HEAD_DIM
    kern = functools.partial(_diff_prompt_kernel, lam_init=lam_init, tq=tq)
    return pl.pallas_call(
        kern,
        out_shape=jax.ShapeDtypeStruct((N_PROMPT, D_MODEL), BF16),
        grid=(BATCH, DIFF_HEADS),
        in_specs=[
            pl.BlockSpec((SEQ, w), lambda b, h: (b, h)),
            pl.BlockSpec((SEQ, w), lambda b, h: (b, DIFF_HEADS + h)),
            pl.BlockSpec((SEQ, w), lambda b, h: (b, 2 * DIFF_HEADS + h)),
            pl.BlockSpec((4, DIFF_HEAD_DIM), lambda b, h: (0, 0)),
            pl.BlockSpec((1, w), lambda b, h: (0, 0)),
        ],
        out_specs=pl.BlockSpec((SEQ, w), lambda b, h: (b, h)),
        compiler_params=_params("parallel", "parallel"),
        name="diff_attn_prompt",
    )(qkv, qkv, qkv, lam_p, g_sub.reshape(1, w))


def _diff_sample_kernel(q_ref, kn_ref, vn_ref, kc0_ref, kc1_ref, vc0_ref, vc1_ref, lp_ref, gs_ref, o_ref, vh_ref,
                        *, lam_init):
    kc_refs = (kc0_ref, kc1_ref)
    h = pl.program_id(1)
    lam = _diff_lambda(lp_ref, lam_init)
    slope = _alibi_slope(h)
    d = DIFF_HEAD_DIM
    t = DEC_SEQ
    tq_p = lax.broadcasted_iota(jnp.int32, (t, PAST_LEN), 0)
    kp_p = lax.broadcasted_iota(jnp.int32, (t, PAST_LEN), 1)
    bias_p = -slope * (PAST_LEN + tq_p - kp_p).astype(F32)
    tq_n = lax.broadcasted_iota(jnp.int32, (t, t), 0)
    kp_n = lax.broadcasted_iota(jnp.int32, (t, t), 1)
    bias_n = -slope * jnp.abs(tq_n - kp_n).astype(F32)
    pp, pn = [], []
    for m in range(2):
        cols = slice(m * d, (m + 1) * d)
        q = q_ref[:, cols].astype(BF16)
        kc = kc_refs[m].reshape(PAST_LEN, d)[...]
        sp = _dot_nt(q, kc.astype(BF16)) * DIFF_SCALE + bias_p
        sn = _dot_nt(q, kn_ref[:, cols].astype(BF16)) * DIFF_SCALE + bias_n
        mx = jnp.maximum(jnp.max(sp, axis=-1, keepdims=True), jnp.max(sn, axis=-1, keepdims=True))
        ep = jnp.exp(sp - mx)
        en = jnp.exp(sn - mx)
        den = jnp.sum(ep, axis=-1, keepdims=True) + jnp.sum(en, axis=-1, keepdims=True)
        pp.append(ep / den)
        pn.append(en / den)
    ap = (pp[0] - lam * pp[1]).astype(BF16)
    an = (pn[0] - lam * pn[1]).astype(BF16)
    for k in range(DIFF_HEADS):
        @pl.when(h == k)
        def _(k=k):
            for half, vc_ref in enumerate((vc0_ref, vc1_ref)):
                rows = vc_ref.reshape(PAST_LEN * DIFF_HEADS, d)
                vh_ref[:, half * d:(half + 1) * d] = rows[pl.ds(k, PAST_LEN, stride=DIFF_HEADS), :].astype(BF16)

    o = _dot(ap, vh_ref[...]) + _dot(an, vn_ref[...].astype(BF16))
    o_ref[...] = (_rms_rows(o, gs_ref[...]) * (1.0 - lam_init)).astype(BF16)


def diff_attn_sample(qkv, cache_k, cache_v, lam_p, g_sub, lam_init):
    w = 2 * DIFF_HEAD_DIM
    r0 = N_PROMPT // DEC_SEQ
    kern = functools.partial(_diff_sample_kernel, lam_init=lam_init)
    k_rows = cache_k.reshape(DEC_BATCH, PAST_LEN, 2 * DIFF_HEADS, 1, DIFF_HEAD_DIM)
    k_spec = lambda m: pl.BlockSpec((None, PAST_LEN, None, 1, DIFF_HEAD_DIM), lambda b, h: (b, 0, 2 * h + m, 0, 0))
    v_spec = lambda half: pl.BlockSpec((None, PAST_LEN, DIFF_HEADS, DIFF_HEAD_DIM), lambda b, h: (b, 0, 0, half))
    return pl.pallas_call(
        kern,
        out_shape=jax.ShapeDtypeStruct((N_SAMPLE, D_MODEL), BF16),
        grid=(DEC_BATCH, DIFF_HEADS),
        in_specs=[
            pl.BlockSpec((DEC_SEQ, w), lambda b, h: (r0 + b, h)),
            pl.BlockSpec((DEC_SEQ, w), lambda b, h: (r0 + b, DIFF_HEADS + h)),
            pl.BlockSpec((DEC_SEQ, w), lambda b, h: (r0 + b, 2 * DIFF_HEADS + h)),
            k_spec(0),
            k_spec(1),
            v_spec(0),
            v_spec(1),
            pl.BlockSpec((4, DIFF_HEAD_DIM), lambda b, h: (0, 0)),
            pl.BlockSpec((1, w), lambda b, h: (0, 0)),
        ],
        out_specs=pl.BlockSpec((DEC_SEQ, w), lambda b, h: (b, h)),
        scratch_shapes=[pltpu.VMEM((PAST_LEN, w), BF16)],
        compiler_params=_params("parallel", "arbitrary"),
        name="diff_attn_sample",
    )(qkv, qkv, qkv, k_rows, k_rows, cache_v, cache_v, lam_p, g_sub.reshape(1, w))


BAND_SCALE = BAND_HEAD_DIM ** -0.5
BAND_TQ = 256
BAND_E = 1024


def _band_bias(e_row, tq):
    eb = jnp.broadcast_to(e_row, (tq, BAND_E))
    return pltpu.roll(eb, BAND_E - (BAND_TQ - 1), 1, stride=1, stride_axis=0)


def _band_prompt_kernel(q_ref, k_ref, v_ref, e_ref, o_ref):
    tq = BAND_TQ
    win = BAND_WINDOW + tq
    bias = _band_bias(e_ref[0], tq)[:, :win]
    qc = lax.broadcasted_iota(jnp.int32, (tq, win), 0) // CHUNK
    jc = lax.broadcasted_iota(jnp.int32, (tq, win), 1) // CHUNK
    bias = jnp.where(jnp.logical_and(jc >= qc, jc <= qc + BAND_WINDOW // CHUNK), bias, NEG)
    k_bf = k_ref[...].astype(BF16)
    v_bf = v_ref[...].astype(BF16)
    spans = []
    for qb in range(SEQ // tq):
        q0 = qb * tq
        k0 = max(0, q0 - BAND_WINDOW)
        spans.append((q0, k0, q0 + tq - k0))
    scores = [_dot_nt(q_ref[q0:q0 + tq, :].astype(BF16), k_bf[k0:k0 + nk, :]) for q0, k0, nk in spans]
    probs = []
    for sc, (q0, k0, nk) in zip(scores, spans):
        sc = sc * BAND_SCALE + bias[:, win - nk:]
        e = jnp.exp(sc - jnp.max(sc, axis=-1, keepdims=True))
        probs.append((e / jnp.sum(e, axis=-1, keepdims=True)).astype(BF16))
    outs = [_dot(p, v_bf[k0:k0 + nk, :]) for p, (q0, k0, nk) in zip(probs, spans)]
    o_ref[...] = jnp.concatenate(outs, axis=0).astype(BF16)


def _band_e_rows(table):
    edge = jnp.broadcast_to(table[:, 2 * REL_CLIP:], (BAND_HEADS, BAND_E - (2 * REL_CLIP + 1)))
    return jnp.concatenate([edge, table[:, ::-1]], axis=1).reshape(BAND_HEADS, 1, BAND_E)


def band_attn_prompt(qkv, table):
    d = BAND_HEAD_DIM
    return pl.pallas_call(
        _band_prompt_kernel,
        out_shape=jax.ShapeDtypeStruct((N_PROMPT, D_MODEL), BF16),
        grid=(BATCH, BAND_HEADS),
        in_specs=[
            pl.BlockSpec((SEQ, d), lambda b, h: (b, h)),
            pl.BlockSpec((SEQ, d), lambda b, h: (b, BAND_HEADS + h)),
            pl.BlockSpec((SEQ, d), lambda b, h: (b, 2 * BAND_HEADS + h)),
            pl.BlockSpec((1, 1, BAND_E), lambda b, h: (h, 0, 0)),
        ],
        out_specs=pl.BlockSpec((SEQ, d), lambda b, h: (b, h)),
        compiler_params=_params("parallel", "parallel"),
        name="band_attn_prompt",
    )(qkv, qkv, qkv, _band_e_rows(table))


def _band_sample_kernel(q_ref, kn_ref, vn_ref, kc_ref, vc_ref, e_ref, o_ref):
    t = DEC_SEQ
    d = BAND_HEAD_DIM
    kc_rows = kc_ref.reshape(BAND_WINDOW * BAND_HEADS, d)
    vc_rows = vc_ref.reshape(BAND_WINDOW * BAND_HEADS, d)
    for h in range(BAND_HEADS):
        cols = slice(h * d, (h + 1) * d)
        bias = _band_bias(e_ref[h], t)
        q = q_ref[:, cols].astype(BF16)
        kc = kc_rows[pl.ds(h, BAND_WINDOW, stride=BAND_HEADS), :]
        vc = vc_rows[pl.ds(h, BAND_WINDOW, stride=BAND_HEADS), :]
        sp = _dot_nt(q, kc.astype(BF16)) * BAND_SCALE + bias[:, :BAND_WINDOW]
        sn = _dot_nt(q, kn_ref[:, cols].astype(BF16)) * BAND_SCALE + bias[:, BAND_WINDOW:BAND_WINDOW + t]
        mx = jnp.maximum(jnp.max(sp, axis=-1, keepdims=True), jnp.max(sn, axis=-1, keepdims=True))
        ep = jnp.exp(sp - mx)
        en = jnp.exp(sn - mx)
        den = jnp.sum(ep, axis=-1, keepdims=True) + jnp.sum(en, axis=-1, keepdims=True)
        o = (_dot((ep / den).astype(BF16), vc.astype(BF16))
             + _dot((en / den).astype(BF16), vn_ref[:, cols].astype(BF16)))
        o_ref[:, cols] = o.astype(BF16)


def band_attn_sample(qkv, cache_k, cache_v, table):
    d = BAND_HEAD_DIM
    r0 = N_PROMPT // DEC_SEQ
    rows_view = (DEC_BATCH, BAND_WINDOW, BAND_HEADS, 1, d)
    cache_spec = pl.BlockSpec((None, BAND_WINDOW, BAND_HEADS, 1, d), lambda b: (b, 0, 0, 0, 0))
    return pl.pallas_call(
        _band_sample_kernel,
        out_shape=jax.ShapeDtypeStruct((N_SAMPLE, D_MODEL), BF16),
        grid=(DEC_BATCH,),
        in_specs=[
            pl.BlockSpec((DEC_SEQ, D_MODEL), lambda b: (r0 + b, 0)),
            pl.BlockSpec((DEC_SEQ, D_MODEL), lambda b: (r0 + b, 1)),
            pl.BlockSpec((DEC_SEQ, D_MODEL), lambda b: (r0 + b, 2)),
            cache_spec,
            cache_spec,
            pl.BlockSpec((BAND_HEADS, 1, BAND_E), lambda b: (0, 0, 0)),
        ],
        out_specs=pl.BlockSpec((DEC_SEQ, D_MODEL), lambda b: (b, 0)),
        compiler_params=_params("parallel"),
        name="band_attn_sample",
    )(qkv, qkv, qkv, cache_k.reshape(rows_view), cache_v.reshape(rows_view), _band_e_rows(table))


def kernel(x_prompt, x_sample, p_prompt, p_sample, state_ssm, state_conv, cache_k_diff, cache_v_diff,
           cache_k_band, cache_v_band, g_ffn, w_ffn_gate, w_ffn_up, w_ffn_down, g_mix,
           ssd_w_in, ssd_conv_w, ssd_conv_b, ssd_dt_bias, ssd_a_log, ssd_d, ssd_g_norm, ssd_w_out,
           diff_w_in, diff_g_q, diff_g_k, diff_lambda, diff_g_sub, diff_w_out,
           band_w_in, band_g_q, band_g_k, band_rel_bias, band_w_out,
           g_ple, w_ple, w_ple_gate):
    h = jnp.concatenate([x_prompt.reshape(N_PROMPT, D_MODEL), x_sample.reshape(N_SAMPLE, D_MODEL)], axis=0)
    p_all = jnp.concatenate([p_prompt.reshape(DEPTH, N_PROMPT, PLE_DIM),
                             p_sample.reshape(DEPTH, N_SAMPLE, PLE_DIM)], axis=1)
    bf = lambda w: w.astype(BF16)
    wg_all, wu_all, wd_all = w_ffn_gate, w_ffn_up, w_ffn_down
    w_ple_all, w_gate_all = w_ple, w_ple_gate
    ssd_w_main = bf(ssd_w_in)
    ssd_w_dt = bf(jnp.pad(ssd_w_in[:, :, SSD_MAIN_WIDTH:], ((0, 0), (0, 0), (0, 128 - SSD_HEADS))))
    ssd_w_out_all, diff_w_in_all, diff_w_out_all = bf(ssd_w_out), diff_w_in, diff_w_out
    band_w_in_all, band_w_out_all = band_w_in, band_w_out

    def prompt_rows(a):
        return a[:N_PROMPT].reshape(BATCH, SEQ, -1)

    def sample_rows(a):
        return a[N_PROMPT:].reshape(DEC_BATCH, DEC_SEQ, -1)

    ssm_p, conv_p, ssm_s, conv_s = [], [], [], []
    kd_p = vd_p = kb_p = vb_p = kd_s = vd_s = kb_s = vb_s = None
    for i in range(DEPTH):
        kind, j = i % 3, i // 3
        h = ffn_half(h, g_ffn, wg_all, wu_all, wd_all, i, 0)
        if kind == 0:
            zx, dt_raw = inproj(h, g_mix[i], ssd_w_main, j, n_cols=SSD_MAIN_WIDTH, w_extra=ssd_w_dt)
            y, st, cvt = ssd_core(zx, dt_raw, state_ssm, state_conv, ssd_conv_w, j, ssd_conv_b[j], ssd_dt_bias[j],
                                  ssd_a_log[j], ssd_d[j], ssd_g_norm[j])
            h = outproj(h, y, ssd_w_out_all, j)
            ssm_p.append(st[:BATCH])
            ssm_s.append(st[BATCH:])
            conv_p.append(cvt[:BATCH])
            conv_s.append(cvt[BATCH:])
        elif kind == 1:
            lam_init = 0.8 - 0.6 * math.exp(-0.3 * i)
            gains = jnp.concatenate([jnp.tile(diff_g_q[j].reshape(-1), DIFF_HEADS),
                                     jnp.tile(diff_g_k[j].reshape(-1), DIFF_HEADS),
                                     jnp.ones((D_MODEL,), F32)]).reshape(1, 3 * D_MODEL)
            qkv = inproj(h, g_mix[i], diff_w_in_all, j, gains, norm_cols=2 * D_MODEL)
            o_p = diff_attn_prompt(qkv, diff_lambda[j], diff_g_sub[j], lam_init)
            o_s = diff_attn_sample(qkv, cache_k_diff[j], cache_v_diff[j], diff_lambda[j], diff_g_sub[j], lam_init)
            h = outproj(h, (o_p, o_s), diff_w_out_all, j)
            k_new, v_new = qkv[:, D_MODEL:2 * D_MODEL], qkv[:, 2 * D_MODEL:]
            kd_p = prompt_rows(k_new).reshape(1, BATCH, SEQ, DIFF_HEADS, 2, DIFF_HEAD_DIM)
            vd_p = prompt_rows(v_new).reshape(1, BATCH, SEQ, DIFF_HEADS, 2 * DIFF_HEAD_DIM)
            kd_s = sample_rows(k_new).reshape(1, DEC_BATCH, DEC_SEQ, DIFF_HEADS, 2, DIFF_HEAD_DIM)
            vd_s = sample_rows(v_new).reshape(1, DEC_BATCH, DEC_SEQ, DIFF_HEADS, 2 * DIFF_HEAD_DIM)
        else:
            gains = jnp.concatenate([jnp.tile(band_g_q[j], BAND_HEADS), jnp.tile(band_g_k[j], BAND_HEADS),
                                     jnp.ones((D_MODEL,), F32)]).reshape(1, 3 * D_MODEL)
            qkv = inproj(h, g_mix[i], band_w_in_all, j, gains, norm_cols=2 * D_MODEL)
            o_p = band_attn_prompt(qkv, band_rel_bias[j])
            o_s = band_attn_sample(qkv, cache_k_band[j], cache_v_band[j], band_rel_bias[j])
            h = outproj(h, (o_p, o_s), band_w_out_all, j)
            tail = jnp.stack([lax.slice(qkv, ((b + 1) * SEQ - BAND_WINDOW, D_MODEL), ((b + 1) * SEQ, 3 * D_MODEL))
                              for b in range(BATCH)])
            kb_p = tail[:, :, :D_MODEL].reshape(1, BATCH, BAND_WINDOW, BAND_HEADS, BAND_HEAD_DIM)
            vb_p = tail[:, :, D_MODEL:].reshape(1, BATCH, BAND_WINDOW, BAND_HEADS, BAND_HEAD_DIM)
            new_s = qkv[N_PROMPT:, D_MODEL:]
            kb_s = new_s[:, :D_MODEL].reshape(1, DEC_BATCH, DEC_SEQ, BAND_HEADS, BAND_HEAD_DIM)
            vb_s = new_s[:, D_MODEL:].reshape(1, DEC_BATCH, DEC_SEQ, BAND_HEADS, BAND_HEAD_DIM)
        h = ffn_half(h, g_ffn, wg_all, wu_all, wd_all, i, 1)
        h = ple_add(h, p_all, g_ple, w_ple_all, w_gate_all, i, split=(i == DEPTH - 1))
    h_p, h_s = h
    return (h_p.reshape(BATCH, SEQ, D_MODEL), h_s.reshape(DEC_BATCH, DEC_SEQ, D_MODEL),
            jnp.stack(ssm_p), jnp.stack(conv_p), kd_p, vd_p, kb_p, vb_p,
            jnp.stack(ssm_s), jnp.stack(conv_s), kd_s, vd_s, kb_s, vb_s)
```
